```python
import jax, jax.numpy as jnp
from jax import lax
import numpy as np

D_MODEL = 2048
BATCH = 16
SEQ = 256
DEPTH = 4
DEC_BATCH = 2
DEC_SEQ = 4096
PAST_LEN = 512

GRID_W = 64
HEAD_DIM = 128
ROPE_THETA = 10000.0
EPS = 1e-6
Q_BLOCK = 128
A_HEADS = 8
A_KV_HEADS = 2
B_HEADS = 4
B_QK_DIM = 128
B_V_DIM = 256
B_CHUNK = 128
C_HEADS = 8
C_Q_RANK = 512
C_KV_RANK = 256
C_NOPE_DIM = 128
C_ROPE_DIM = 64
C_V_DIM = 128
D_FF = 4 * D_MODEL
N_BRANCH = 3
A_Q_W = A_HEADS * HEAD_DIM
A_KV_W = A_KV_HEADS * HEAD_DIM
B_QK_W = B_HEADS * B_QK_DIM
B_V_W = B_HEADS * B_V_DIM
GATE_W = N_BRANCH * D_MODEL
IN_WIDTH = A_Q_W + 2 * A_KV_W + 2 * B_QK_W + 2 * B_V_W + C_Q_RANK + C_KV_RANK + C_ROPE_DIM + GATE_W

kernel_name = "hybrid_diffusion_gqa_retention_mla_step"


def rms_norm(x, g):
    xf = x.astype(jnp.float32)
    y = xf * lax.rsqrt(jnp.mean(xf * xf, axis=-1, keepdims=True) + EPS)
    return (y * g.astype(jnp.float32)).astype(x.dtype)


def grid_positions(n_tokens):
    rows = n_tokens // GRID_W
    row = jnp.repeat(jnp.arange(rows, dtype=jnp.int32), GRID_W)
    col = jnp.tile(jnp.arange(GRID_W, dtype=jnp.int32), rows)
    return row, col


def rope_1d(x, pos):
    half = x.shape[-1] // 2
    inv = ROPE_THETA ** (-jnp.arange(half, dtype=jnp.float32) / half)
    ang = pos.astype(jnp.float32)[:, None] * inv[None, :]
    cos = jnp.cos(ang)[None, :, None, :]
    sin = jnp.sin(ang)[None, :, None, :]
    x1 = x[..., :half].astype(jnp.float32)
    x2 = x[..., half:].astype(jnp.float32)
    return jnp.concatenate([x1 * cos - x2 * sin, x2 * cos + x1 * sin], axis=-1).astype(x.dtype)


def rope_2d(x, row, col):
    h = x.shape[-1] // 2
    return jnp.concatenate([rope_1d(x[..., :h], row), rope_1d(x[..., h:], col)], axis=-1)


def block_attention(q, k, v, scale):
    b, nq, hq, dq = q.shape
    g = k.shape[2]
    rep = hq // g
    nblk = nq // Q_BLOCK
    qb = q.reshape(b, nblk, Q_BLOCK, g, rep, dq).transpose(1, 0, 2, 3, 4, 5)

    def one_block(qblk):
        s = jnp.einsum('bqgrd,bkgd->bgrqk', qblk, k, preferred_element_type=jnp.float32) * scale
        p = jax.nn.softmax(s, axis=-1).astype(v.dtype)
        return jnp.einsum('bgrqk,bkge->bqgre', p, v)

    o = lax.map(one_block, qb)
    return o.transpose(1, 0, 2, 3, 4, 5).reshape(b, nq, hq, v.shape[-1])


def retention_scan(q, k, v, log_gamma, s0):
    b, n, h, dk = q.shape
    dv = v.shape[-1]
    nc = n // B_CHUNK
    qc = q.reshape(b, nc, B_CHUNK, h, dk).astype(jnp.float32)
    kc = k.reshape(b, nc, B_CHUNK, h, dk).astype(jnp.float32) * (dk ** -0.5)
    vc = v.reshape(b, nc, B_CHUNK, h, dv).astype(jnp.float32)
    lg = log_gamma.astype(jnp.float32)
    idx = jnp.arange(B_CHUNK, dtype=jnp.float32)
    diff = idx[:, None] - idx[None, :]
    decay_in = jnp.where(diff[None] >= 0, jnp.exp(jnp.maximum(diff, 0.0)[None] * lg[:, None, None]), 0.0)
    q_decay = jnp.exp((idx + 1.0)[None, :] * lg[:, None])
    k_decay = jnp.exp((B_CHUNK - 1.0 - idx)[None, :] * lg[:, None])
    chunk_decay = jnp.exp(B_CHUNK * lg)
    scores = jnp.einsum('bnihd,bnjhd->bnhij', qc, kc) * decay_in
    o_inner = jnp.einsum('bnhij,bnjhe->bnihe', scores, vc)
    kv = jnp.einsum('bnjhd,hj,bnjhe->bnhde', kc, k_decay, vc)

    def step(s, kv_n):
        return chunk_decay[None, :, None, None] * s + kv_n, s

    s_final, s_prev = lax.scan(step, s0.astype(jnp.float32), jnp.moveaxis(kv, 1, 0))
    s_prev = jnp.moveaxis(s_prev, 0, 1)
    o_cross = jnp.einsum('bnihd,bnhde->bnihe', qc, s_prev) * q_decay.T[None, None, :, :, None]
    o = (o_inner + o_cross).reshape(b, n, h, dv).astype(v.dtype)
    return o, s_final


def bidir_retention(q, k, v, lg_f, lg_b, s0_f, s0_b):
    o_f, s_f = retention_scan(q, k, v, lg_f, s0_f)
    o_b, s_b = retention_scan(q[:, ::-1], k[:, ::-1], v[:, ::-1], lg_b, s0_b)
    return o_f + o_b[:, ::-1], s_f, s_b


def mixer(h, p, pos, ctx):
    b, n, _ = h.shape
    widths = [A_Q_W, A_KV_W, A_KV_W, B_QK_W, B_QK_W, B_V_W, B_V_W, C_Q_RANK, C_KV_RANK, C_ROPE_DIM]
    offsets = []
    acc = 0
    for w in widths:
        acc += w
        offsets.append(acc)
    proj = h @ p['w_in']
    a_q, a_k, a_v, b_q, b_k, b_v, b_g, c_ql, c_kvl, c_kr, gates = jnp.split(proj, offsets, axis=-1)

    a_q = rms_norm(a_q.reshape(b, n, A_HEADS, HEAD_DIM), p['attn_q_norm'])
    a_k = rms_norm(a_k.reshape(b, n, A_KV_HEADS, HEAD_DIM), p['attn_k_norm'])
    a_v = a_v.reshape(b, n, A_KV_HEADS, HEAD_DIM)
    b_q = b_q.reshape(b, n, B_HEADS, B_QK_DIM)
    b_k = b_k.reshape(b, n, B_HEADS, B_QK_DIM)
    b_v = b_v.reshape(b, n, B_HEADS, B_V_DIM)
    lg_f = jax.nn.log_sigmoid(p['ret_decay_fwd'].astype(jnp.float32))
    lg_b = jax.nn.log_sigmoid(p['ret_decay_bwd'].astype(jnp.float32))
    cq = rms_norm(c_ql, p['mla_q_norm']) @ p['w_mla_uq']
    cq = cq.reshape(b, n, C_HEADS, C_NOPE_DIM + C_ROPE_DIM)
    q_nope, q_rope = cq[..., :C_NOPE_DIM], cq[..., C_NOPE_DIM:]
    ckv = rms_norm(c_kvl, p['mla_kv_norm'])
    krope = c_kr

    if pos is None:
        keys_a, vals_a = a_k, a_v
        s0_f = jnp.zeros((b, B_HEADS, B_QK_DIM, B_V_DIM), jnp.float32)
        s0_b = s0_f
        ckv_all, krope_all = ckv, krope
    else:
        row, col = pos
        ctx_k, ctx_v, s0_f, s0_b, ctx_ckv, ctx_krope = ctx
        a_q = rope_2d(a_q, row, col)
        keys_a = jnp.concatenate([rope_2d(a_k, row, col), ctx_k.astype(a_k.dtype)], axis=1)
        vals_a = jnp.concatenate([a_v, ctx_v.astype(a_v.dtype)], axis=1)
        b_q = rope_2d(b_q, row, col)
        b_k = rope_2d(b_k, row, col)
        q_rope = rope_2d(q_rope, row, col)
        krope_lat = rope_2d(krope[:, :, None, :], row, col)[:, :, 0, :]
        ckv_all = jnp.concatenate([ckv, ctx_ckv.astype(ckv.dtype)], axis=1)
        krope_all = jnp.concatenate([krope_lat, ctx_krope.astype(krope.dtype)], axis=1)

    o_a = block_attention(a_q, keys_a, vals_a, HEAD_DIM ** -0.5).reshape(b, n, A_Q_W)

    o_b, s_f, s_b = bidir_retention(b_q, b_k, b_v, lg_f, lg_b, s0_f, s0_b)
    o_b = rms_norm(o_b, p['ret_gn'].reshape(B_HEADS, B_V_DIM)).reshape(b, n, B_V_W)
    o_b = jax.nn.silu(b_g) * o_b

    nk = ckv_all.shape[1]
    kv_up = (ckv_all @ p['w_mla_ukv']).reshape(b, nk, C_HEADS, C_NOPE_DIM + C_V_DIM)
    k_nope, v_c = kv_up[..., :C_NOPE_DIM], kv_up[..., C_NOPE_DIM:]
    k_c = jnp.concatenate([k_nope, jnp.broadcast_to(krope_all[:, :, None, :], (b, nk, C_HEADS, C_ROPE_DIM))], axis=-1)
    q_c = jnp.concatenate([q_nope, q_rope], axis=-1)
    o_c = block_attention(q_c, k_c, v_c, (C_NOPE_DIM + C_ROPE_DIM) ** -0.5).reshape(b, n, C_HEADS * C_V_DIM)

    g = jax.nn.sigmoid(gates.reshape(b, n, N_BRANCH, D_MODEL))
    merged = (g[:, :, 0] * (o_a @ p['w_branch_a'])
              + g[:, :, 1] * (o_b @ p['w_branch_b'])
              + g[:, :, 2] * (o_c @ p['w_branch_c']))
    out = merged @ p['w_out']
    if pos is None:
        return out, (a_k, a_v, s_f, s_b, ckv, krope)
    return out, None


def layer(x, mod, p, pos, ctx):
    shift1, scale1, gate1, shift2, scale2, gate2 = jnp.split(mod[:, None, :].astype(x.dtype), 6, axis=-1)
    h = rms_norm(x, p['g_pre_mix']) * (1.0 + scale1) + shift1
    m, new_ctx = mixer(h, p, pos, ctx)
    x = x + gate1 * rms_norm(m, p['g_post_mix'])
    h = rms_norm(x, p['g_pre_mlp']) * (1.0 + scale2) + shift2
    f = jnp.square(jax.nn.relu(h @ p['w_mlp_up'])) @ p['w_mlp_down']
    x = x + gate2 * rms_norm(f, p['g_post_mlp'])
    return x, new_ctx


def _normal(k, shape, scale):
    return jax.random.normal(k, shape, jnp.float32) * scale


def setup_inputs(seed: int = 0) -> dict:
    key = jax.random.key(seed)
    ks = jax.random.split(key, 40)
    D = D_MODEL
    base_decay = jnp.log(2.0 ** (5.0 + jnp.arange(B_HEADS, dtype=jnp.float32)) - 1.0)
    return {
        'x_prompt': _normal(ks[0], (BATCH, SEQ, D), 1.0),
        'x_sample': _normal(ks[1], (DEC_BATCH, DEC_SEQ, D), 1.0),
        'c': _normal(ks[2], (DEC_BATCH, D), 1.0),
        'cache_attn_k': _normal(ks[3], (DEC_BATCH, DEPTH, PAST_LEN, A_KV_HEADS, HEAD_DIM), 1.0),
        'cache_attn_v': _normal(ks[4], (DEC_BATCH, DEPTH, PAST_LEN, A_KV_HEADS, HEAD_DIM), 1.0),
        'state_ret_fwd': _normal(ks[5], (DEC_BATCH, DEPTH, B_HEADS, B_QK_DIM, B_V_DIM), 0.3),
        'state_ret_bwd': _normal(ks[6], (DEC_BATCH, DEPTH, B_HEADS, B_QK_DIM, B_V_DIM), 0.3),
        'cache_mla_ckv': _normal(ks[7], (DEC_BATCH, DEPTH, PAST_LEN, C_KV_RANK), 1.0),
        'cache_mla_krope': _normal(ks[8], (DEC_BATCH, DEPTH, PAST_LEN, C_ROPE_DIM), 1.0),
        'c_ctx': _normal(ks[9], (D,), 1.0),
        'w_mod': _normal(ks[10], (DEPTH, D, 6 * D), 0.5 * D ** -0.5),
        'b_mod': _normal(ks[11], (DEPTH, 6 * D), 0.02),
        'g_pre_mix': 1.0 + _normal(ks[12], (DEPTH, D), 0.02),
        'g_post_mix': 1.0 + _normal(ks[13], (DEPTH, D), 0.02),
        'g_pre_mlp': 1.0 + _normal(ks[14], (DEPTH, D), 0.02),
        'g_post_mlp': 1.0 + _normal(ks[15], (DEPTH, D), 0.02),
        'w_in': _normal(ks[16], (DEPTH, D, IN_WIDTH), D ** -0.5),
        'attn_q_norm': 1.0 + _normal(ks[17], (DEPTH, HEAD_DIM), 0.02),
        'attn_k_norm': 1.0 + _normal(ks[18], (DEPTH, HEAD_DIM), 0.02),
        'ret_decay_fwd': base_decay[None, :] + _normal(ks[19], (DEPTH, B_HEADS), 0.1),
        'ret_decay_bwd': base_decay[None, :] + _normal(ks[20], (DEPTH, B_HEADS), 0.1),
        'ret_gn': 1.0 + _normal(ks[21], (DEPTH, B_V_W), 0.02),
        'mla_q_norm': 1.0 + _normal(ks[22], (DEPTH, C_Q_RANK), 0.02),
        'mla_kv_norm': 1.0 + _normal(ks[23], (DEPTH, C_KV_RANK), 0.02),
        'w_mla_uq': _normal(ks[24], (DEPTH, C_Q_RANK, C_HEADS * (C_NOPE_DIM + C_ROPE_DIM)), C_Q_RANK ** -0.5),
        'w_mla_ukv': _normal(ks[25], (DEPTH, C_KV_RANK, C_HEADS * (C_NOPE_DIM + C_V_DIM)), C_KV_RANK ** -0.5),
        'w_branch_a': _normal(ks[26], (DEPTH, A_Q_W, D), A_Q_W ** -0.5),
        'w_branch_b': _normal(ks[27], (DEPTH, B_V_W, D), B_V_W ** -0.5),
        'w_branch_c': _normal(ks[28], (DEPTH, C_HEADS * C_V_DIM, D), (C_HEADS * C_V_DIM) ** -0.5),
        'w_out': _normal(ks[29], (DEPTH, D, D), D ** -0.5),
        'w_mlp_up': _normal(ks[30], (DEPTH, D, D_FF), D ** -0.5),
        'w_mlp_down': _normal(ks[31], (DEPTH, D_FF, D), D_FF ** -0.5),
    }


def reference(x_prompt, x_sample, c, cache_attn_k, cache_attn_v, state_ret_fwd, state_ret_bwd,
              cache_mla_ckv, cache_mla_krope, c_ctx, w_mod, b_mod, g_pre_mix, g_post_mix,
              g_pre_mlp, g_post_mlp, w_in, attn_q_norm, attn_k_norm, ret_decay_fwd, ret_decay_bwd,
              ret_gn, mla_q_norm, mla_kv_norm, w_mla_uq, w_mla_ukv, w_branch_a, w_branch_b,
              w_branch_c, w_out, w_mlp_up, w_mlp_down):
    row, col = grid_positions(x_sample.shape[1])
    y_p = x_prompt
    y_s = x_sample
    ak_l, av_l, sf_l, sb_l, ckv_l, kr_l = [], [], [], [], [], []
    for l in range(DEPTH):
        p = {
            'g_pre_mix': g_pre_mix[l], 'g_post_mix': g_post_mix[l],
            'g_pre_mlp': g_pre_mlp[l], 'g_post_mlp': g_post_mlp[l],
            'w_in': w_in[l], 'attn_q_norm': attn_q_norm[l], 'attn_k_norm': attn_k_norm[l],
            'ret_decay_fwd': ret_decay_fwd[l], 'ret_decay_bwd': ret_decay_bwd[l], 'ret_gn': ret_gn[l],
            'mla_q_norm': mla_q_norm[l], 'mla_kv_norm': mla_kv_norm[l],
            'w_mla_uq': w_mla_uq[l], 'w_mla_ukv': w_mla_ukv[l],
            'w_branch_a': w_branch_a[l], 'w_branch_b': w_branch_b[l], 'w_branch_c': w_branch_c[l],
            'w_out': w_out[l], 'w_mlp_up': w_mlp_up[l], 'w_mlp_down': w_mlp_down[l],
        }
        mod_ctx = (jax.nn.silu(c_ctx) @ w_mod[l] + b_mod[l])[None, :]
        y_p, (ak, av, sf, sb, ckv, kr) = layer(y_p, mod_ctx, p, None, None)
        ak_l.append(ak); av_l.append(av); sf_l.append(sf); sb_l.append(sb); ckv_l.append(ckv); kr_l.append(kr)
        mod_lat = jax.nn.silu(c) @ w_mod[l] + b_mod[l]
        ctx = (cache_attn_k[:, l], cache_attn_v[:, l], state_ret_fwd[:, l], state_ret_bwd[:, l],
               cache_mla_ckv[:, l], cache_mla_krope[:, l])
        y_s, _ = layer(y_s, mod_lat, p, (row, col), ctx)
    new_attn_k = jnp.stack(ak_l, axis=1)
    new_attn_v = jnp.stack(av_l, axis=1)
    new_ret_fwd = jnp.stack(sf_l, axis=1)
    new_ret_bwd = jnp.stack(sb_l, axis=1)
    new_mla_ckv = jnp.stack(ckv_l, axis=1)
    new_mla_krope = jnp.stack(kr_l, axis=1)
    return (y_p, y_s, new_attn_k, new_attn_v, new_ret_fwd, new_ret_bwd, new_mla_ckv, new_mla_krope)
```

```python
import functools

import jax
import jax.numpy as jnp
from jax import lax
from jax.experimental import pallas as pl
from jax.experimental.pallas import tpu as pltpu

F32 = jnp.float32
BF16 = jnp.bfloat16

EPS = 1e-6
ROPE_THETA = 10000.0
GRID_W = 64
LANE = 128
HEAD_DIM = 128
A_HEADS = 8
A_KV_HEADS = 2
B_HEADS = 4
B_QK_DIM = 128
B_V_DIM = 256
B_CHUNK = 128
C_HEADS = 8
C_Q_RANK = 512
C_KV_RANK = 256
C_NOPE_DIM = 128
C_ROPE_DIM = 64
C_V_DIM = 128
C_QK_PAD = 256

OFF_AQ = 0
OFF_AK = OFF_AQ + A_HEADS * HEAD_DIM
OFF_AV = OFF_AK + A_KV_HEADS * HEAD_DIM
OFF_BQ = OFF_AV + A_KV_HEADS * HEAD_DIM
OFF_BK = OFF_BQ + B_HEADS * B_QK_DIM
OFF_BV = OFF_BK + B_HEADS * B_QK_DIM
OFF_BG = OFF_BV + B_HEADS * B_V_DIM
OFF_CQ = OFF_BG + B_HEADS * B_V_DIM
OFF_CKV = OFF_CQ + C_Q_RANK
OFF_CKR = OFF_CKV + C_KV_RANK
IN_MAIN = OFF_CKR + C_ROPE_DIM
OFF_GATE = 5632
MOD_ROWS = 8

VMEM_LIMIT = 56 * 1024 * 1024


def _params(*sem):
    return pltpu.CompilerParams(dimension_semantics=sem, vmem_limit_bytes=VMEM_LIMIT)


def _rms(x, g):
    return x * lax.rsqrt(jnp.mean(x * x, axis=-1, keepdims=True) + EPS) * g


def _rope(x, cos, sin_lo, sin_hi, shift):
    return (x * cos + pltpu.roll(x, LANE - shift, 1) * sin_lo + pltpu.roll(x, shift, 1) * sin_hi)


def _mod_kernel(c_ref, w_ref, b_ref, o_ref):
    c = c_ref[...]
    a = (c * jax.nn.sigmoid(c)).astype(BF16)
    o_ref[...] = jnp.dot(a, w_ref[...].astype(BF16), preferred_element_type=F32) + b_ref[...]


def _modulation(cvec, w_mod, b_mod):
    depth, d, n = w_mod.shape
    rows = cvec.shape[0]
    tn = min(1024, n)
    return pl.pallas_call(
        _mod_kernel,
        grid=(depth, n // tn),
        in_specs=[
            pl.BlockSpec((rows, d), lambda l, j: (0, 0)),
            pl.BlockSpec((None, d, tn), lambda l, j: (l, 0, j)),
            pl.BlockSpec((None, 1, tn), lambda l, j: (l, 0, j)),
        ],
        out_specs=pl.BlockSpec((None, rows, tn), lambda l, j: (l, 0, j)),
        out_shape=jax.ShapeDtypeStruct((depth, rows, n), F32),
        compiler_params=_params("arbitrary", "arbitrary"),
        name="modulation",
    )(cvec, w_mod, b_mod.reshape(depth, 1, n))


def _prenorm_kernel(x_ref, g_ref, mod_ref, h_ref):
    y = _rms(x_ref[...], g_ref[...])
    h_ref[...] = (y * (1.0 + mod_ref[1:2, :]) + mod_ref[0:1, :]).astype(BF16)


def _prenorm(x, g, mod, layer, grp):
    m, d = x.shape
    tm = min(512, m)
    return pl.pallas_call(
        _prenorm_kernel,
        grid=(m // tm,),
        in_specs=[
            pl.BlockSpec((tm, d), lambda i: (i, 0)),
            pl.BlockSpec((None, 1, d), lambda i: (layer, 0, 0)),
            pl.BlockSpec((None, None, MOD_ROWS, d), lambda i: (layer, grp(i * tm), 0, 0)),
        ],
        out_specs=pl.BlockSpec((tm, d), lambda i: (i, 0)),
        out_shape=jax.ShapeDtypeStruct((m, d), BF16),
        compiler_params=_params("arbitrary"),
        name="prenorm",
    )(x, g, mod)


def _mm_kernel(a_ref, w_ref, o_ref, *, relu2):
    y = jnp.dot(a_ref[...], w_ref[...], preferred_element_type=F32)
    if relu2:
        y = jnp.square(jnp.maximum(y, 0.0))
    o_ref[...] = y.astype(o_ref.dtype)


def _matmul(a, w, layer, *, tm, tn, relu2=False, name):
    m, k = a.shape
    n = w.shape[-1]
    tm, tn = min(tm, m), min(tn, n)
    return pl.pallas_call(
        functools.partial(_mm_kernel, relu2=relu2),
        grid=(m // tm, n // tn),
        in_specs=[
            pl.BlockSpec((tm, k), lambda i, j: (i, 0)),
            pl.BlockSpec((None, k, tn), lambda i, j: (layer, 0, j)),
        ],
        out_specs=pl.BlockSpec((tm, tn), lambda i, j: (i, j)),
        out_shape=jax.ShapeDtypeStruct((m, n), BF16),
        compiler_params=_params("arbitrary", "arbitrary"),
        name=name,
    )(a, w)


def _mm_row_kernel(a_ref, w_ref, x_ref, gpost_ref, mod_ref, gnext_ref, modn_ref, *rest,
                   gate_row, next_rows, nk):
    if next_rows is None:
        xo_ref, acc_ref = rest
    else:
        xo_ref, ho_ref, acc_ref = rest
    k = pl.program_id(1)

    @pl.when(k == 0)
    def _():
        acc_ref[...] = jnp.zeros_like(acc_ref)

    acc_ref[...] += jnp.dot(a_ref[...], w_ref[...], preferred_element_type=F32)

    @pl.when(k == nk - 1)
    def _():
        xn = x_ref[...] + mod_ref[gate_row:gate_row + 1, :] * _rms(acc_ref[...], gpost_ref[...])
        xo_ref[...] = xn
        if next_rows is not None:
            shift_row, scale_row = next_rows
            y = _rms(xn, gnext_ref[...])
            ho_ref[...] = (y * (1.0 + modn_ref[scale_row:scale_row + 1, :])
                           + modn_ref[shift_row:shift_row + 1, :]).astype(BF16)


def _matmul_row(a, w, x, gpost, gnext, mod, layer, next_layer, grp, *, gate_row, next_rows, tm, tk, name):
    m, kdim = a.shape
    d = w.shape[-1]
    tm, tk = min(tm, m), min(tk, kdim)
    nk = kdim // tk
    vec = pl.BlockSpec((None, 1, d), lambda i, k: (layer, 0, 0))
    vecn = pl.BlockSpec((None, 1, d), lambda i, k: (next_layer, 0, 0))
    out_specs = [pl.BlockSpec((tm, d), lambda i, k: (i, 0))]
    out_shape = [jax.ShapeDtypeStruct((m, d), F32)]
    if next_rows is not None:
        out_specs.append(pl.BlockSpec((tm, d), lambda i, k: (i, 0)))
        out_shape.append(jax.ShapeDtypeStruct((m, d), BF16))
    return pl.pallas_call(
        functools.partial(_mm_row_kernel, gate_row=gate_row, next_rows=next_rows, nk=nk),
        grid=(m // tm, nk),
        in_specs=[
            pl.BlockSpec((tm, tk), lambda i, k: (i, k)),
            pl.BlockSpec((None, tk, d), lambda i, k: (layer, k, 0)),
            pl.BlockSpec((tm, d), lambda i, k: (i, 0)),
            vec,
            pl.BlockSpec((None, None, MOD_ROWS, d), lambda i, k: (layer, grp(i * tm), 0, 0)),
            vecn,
            pl.BlockSpec((None, None, MOD_ROWS, d), lambda i, k: (next_layer, grp(i * tm), 0, 0)),
        ],
        out_specs=out_specs,
        out_shape=out_shape,
        scratch_shapes=[pltpu.VMEM((tm, d), F32)],
        compiler_params=_params("arbitrary", "arbitrary"),
        name=name,
    )(a, w, x, gpost, mod, gnext, mod)


def _prep_a_kernel(p_ref, gq_ref, gk_ref, *rest, rope, cache):
    rest = list(rest)
    tabs = [rest.pop(0) for _ in range(3)] if rope else None
    q_ref, k_ref = rest[:2]
    scale = HEAD_DIM ** -0.5
    for h in range(A_HEADS):
        y = _rms(p_ref[:, h * HEAD_DIM:(h + 1) * HEAD_DIM].astype(F32), gq_ref[...])
        if rope:
            y = _rope(y, tabs[0][...], tabs[1][...], tabs[2][...], HEAD_DIM // 4)
        q_ref[:, h * HEAD_DIM:(h + 1) * HEAD_DIM] = (y * scale).astype(BF16)
    for g in range(A_KV_HEADS):
        lo = OFF_AK + g * HEAD_DIM
        y = _rms(p_ref[:, lo:lo + HEAD_DIM].astype(F32), gk_ref[...])
        if cache:
            rest[2][:, g * HEAD_DIM:(g + 1) * HEAD_DIM] = y
        if rope:
            y = _rope(y, tabs[0][...], tabs[1][...], tabs[2][...], HEAD_DIM // 4)
        k_ref[:, g * HEAD_DIM:(g + 1) * HEAD_DIM] = y.astype(BF16)
    if cache:
        rest[3][...] = p_ref[:, OFF_AV:OFF_BQ].astype(F32)


def _prep_a(proj, gq, gk, layer, tabs, n_per, *, cache):
    m = proj.shape[0]
    tr = min(512, n_per)
    rope = tabs is not None
    width = OFF_BQ
    kvw = A_KV_HEADS * HEAD_DIM
    in_specs = [
        pl.BlockSpec((tr, width), lambda i: (i, 0)),
        pl.BlockSpec((None, 1, HEAD_DIM), lambda i: (layer, 0, 0)),
        pl.BlockSpec((None, 1, HEAD_DIM), lambda i: (layer, 0, 0)),
    ]
    args = [proj, gq, gk]
    if rope:
        nt = n_per // tr
        in_specs += [pl.BlockSpec((tr, LANE), lambda i: (i % nt, 0))] * 3
        args += list(tabs)
    out_specs = [pl.BlockSpec((tr, A_HEADS * HEAD_DIM), lambda i: (i, 0)),
                 pl.BlockSpec((tr, kvw), lambda i: (i, 0))]
    out_shape = [jax.ShapeDtypeStruct((m, A_HEADS * HEAD_DIM), BF16),
                 jax.ShapeDtypeStruct((m, kvw), BF16)]
    if cache:
        out_specs += [pl.BlockSpec((tr, kvw), lambda i: (i, 0))] * 2
        out_shape += [jax.ShapeDtypeStruct((m, kvw), F32)] * 2
    return pl.pallas_call(
        functools.partial(_prep_a_kernel, rope=rope, cache=cache),
        grid=(m // tr,),
        in_specs=in_specs,
        out_specs=out_specs,
        out_shape=out_shape,
        compiler_params=_params("arbitrary"),
        name="prep_a",
    )(*args)


def _prep_cq_kernel(p_ref, g_ref, w_ref, *rest, rope):
    rest = list(rest)
    tabs = [rest.pop(0) for _ in range(3)] if rope else None
    q_ref = rest[0]
    scale = (C_NOPE_DIM + C_ROPE_DIM) ** -0.5
    y = _rms(p_ref[...].astype(F32), g_ref[...]).astype(BF16)
    z = jnp.dot(y, w_ref[...], preferred_element_type=F32)
    for h in range(C_HEADS):
        lo = h * C_QK_PAD
        q_ref[:, lo:lo + C_NOPE_DIM] = (z[:, lo:lo + C_NOPE_DIM] * scale).astype(BF16)
        r = z[:, lo + C_NOPE_DIM:lo + C_QK_PAD]
        if rope:
            r = _rope(r, tabs[0][...], tabs[1][...], tabs[2][...], C_ROPE_DIM // 4)
        q_ref[:, lo + C_NOPE_DIM:lo + C_QK_PAD] = (r * scale).astype(BF16)


def _prep_cq(proj, g, w_uq, layer, tabs, n_per):
    m = proj.shape[0]
    tr = min(512, n_per)
    rope = tabs is not None
    width = C_HEADS * C_QK_PAD
    in_specs = [
        pl.BlockSpec((tr, C_Q_RANK), lambda i: (i, OFF_CQ // C_Q_RANK)),
        pl.BlockSpec((None, 1, C_Q_RANK), lambda i: (layer, 0, 0)),
        pl.BlockSpec((None, C_Q_RANK, width), lambda i: (layer, 0, 0)),
    ]
    args = [proj, g, w_uq]
    if rope:
        nt = n_per // tr
        in_specs += [pl.BlockSpec((tr, LANE), lambda i: (i % nt, 0))] * 3
        args += list(tabs)
    return pl.pallas_call(
        functools.partial(_prep_cq_kernel, rope=rope),
        grid=(m // tr,),
        in_specs=in_specs,
        out_specs=pl.BlockSpec((tr, width), lambda i: (i, 0)),
        out_shape=jax.ShapeDtypeStruct((m, width), BF16),
        compiler_params=_params("arbitrary"),
        name="prep_cq",
    )(*args)


def _prep_ckv_kernel(ckv_ref, kr_ref, g_ref, w_ref, *rest, norm, rope, cache):
    rest = list(rest)
    tabs = [rest.pop(0) for _ in range(3)] if rope else None
    kc_ref, vc_ref = rest[:2]
    x = ckv_ref[...].astype(F32)
    if norm:
        x = _rms(x, g_ref[...])
    kr = kr_ref[...].astype(F32)
    if cache:
        rest[2][...] = x
        rest[3][...] = kr[:, :C_ROPE_DIM]
    if rope:
        kr = _rope(kr, tabs[0][...], tabs[1][...], tabs[2][...], C_ROPE_DIM // 4)
    krb = kr.astype(BF16)
    z = jnp.dot(x.astype(BF16), w_ref[...], preferred_element_type=F32)
    up = C_NOPE_DIM + C_V_DIM
    for h in range(C_HEADS):
        kc_ref[:, h * C_QK_PAD:h * C_QK_PAD + C_NOPE_DIM] = z[:, h * up:h * up + C_NOPE_DIM].astype(BF16)
        kc_ref[:, h * C_QK_PAD + C_NOPE_DIM:(h + 1) * C_QK_PAD] = krb
        vc_ref[:, h * C_V_DIM:(h + 1) * C_V_DIM] = z[:, h * up + C_NOPE_DIM:(h + 1) * up].astype(BF16)


def _prep_ckv(ckv_src, ckv_blk, kr_src, kr_blk, g, w_ukv, layer, tabs, n_per, *, norm, cache):
    m = ckv_src.shape[0]
    tr = min(512, n_per)
    rope = tabs is not None
    in_specs = [
        pl.BlockSpec((tr, C_KV_RANK), lambda i: (i, ckv_blk)),
        pl.BlockSpec((tr, LANE), lambda i: (i, kr_blk)),
        pl.BlockSpec((None, 1, C_KV_RANK), lambda i: (layer, 0, 0)),
        pl.BlockSpec((None, C_KV_RANK, C_HEADS * (C_NOPE_DIM + C_V_DIM)), lambda i: (layer, 0, 0)),
    ]
    args = [ckv_src, kr_src, g, w_ukv]
    if rope:
        nt = n_per // tr
        in_specs += [pl.BlockSpec((tr, LANE), lambda i: (i % nt, 0))] * 3
        args += list(tabs)
    out_specs = [pl.BlockSpec((tr, C_HEADS * C_QK_PAD), lambda i: (i, 0)),
                 pl.BlockSpec((tr, C_HEADS * C_V_DIM), lambda i: (i, 0))]
    out_shape = [jax.ShapeDtypeStruct((m, C_HEADS * C_QK_PAD), BF16),
                 jax.ShapeDtypeStruct((m, C_HEADS * C_V_DIM), BF16)]
    if cache:
        out_specs += [pl.BlockSpec((tr, C_KV_RANK), lambda i: (i, 0)),
                      pl.BlockSpec((tr, C_ROPE_DIM), lambda i: (i, 0))]
        out_shape += [jax.ShapeDtypeStruct((m, C_KV_RANK), F32),
                      jax.ShapeDtypeStruct((m, C_ROPE_DIM), F32)]
    return pl.pallas_call(
        functools.partial(_prep_ckv_kernel, norm=norm, rope=rope, cache=cache),
        grid=(m // tr,),
        in_specs=in_specs,
        out_specs=out_specs,
        out_shape=out_shape,
        compiler_params=_params("arbitrary"),
        name="prep_ckv",
    )(*args)


def _attn_kernel(q_ref, k_ref, v_ref, *rest, rep, dqk, dv, tq, tk, ctx):
    if ctx:
        k2_ref, v2_ref, o_ref, acc_ref = rest
    else:
        o_ref, acc_ref = rest
    rows = rep * tq
    if rep == 1:
        q = q_ref[...]
    else:
        q = jnp.concatenate([q_ref[:, r * dqk:(r + 1) * dqk] for r in range(rep)], axis=0)
    acc_ref[...] = jnp.zeros_like(acc_ref)

    def step(kc, vc, m, l):
        s = lax.dot_general(q, kc, (((1,), (1,)), ((), ())), preferred_element_type=F32)
        m_new = jnp.maximum(m, jnp.max(s, axis=-1, keepdims=True))
        alpha = jnp.exp(m - m_new)
        p = jnp.exp(s - m_new)
        l_new = alpha * l + jnp.sum(p, axis=-1, keepdims=True)
        acc_ref[...] = alpha * acc_ref[...] + jnp.dot(p.astype(BF16), vc, preferred_element_type=F32)
        return m_new, l_new

    nk = k_ref.shape[0]
    m0 = jnp.full((rows, 1), -jnp.inf, F32)
    l0 = jnp.zeros((rows, 1), F32)

    def body(c, carry):
        lo = pl.multiple_of(c * tk, tk)
        return step(k_ref[pl.ds(lo, tk), :], v_ref[pl.ds(lo, tk), :], *carry)

    m, l = lax.fori_loop(0, nk // tk, body, (m0, l0))
    if ctx:
        m, l = step(k2_ref[...], v2_ref[...], m, l)
    o = acc_ref[...] / l
    for r in range(rep):
        o_ref[:, r * dv:(r + 1) * dv] = o[r * tq:(r + 1) * tq].astype(o_ref.dtype)


def _attention(q, k, kblk0, v, vblk0, ctx_kv, batch, n, *, groups, rep, dqk, dv, rows, name):
    tq = min(rows // rep, n)
    tk = min(1024, n)
    nq = n // tq
    in_specs = [
        pl.BlockSpec((tq, rep * dqk), lambda b, g, i: (b * nq + i, g)),
        pl.BlockSpec((n, dqk), lambda b, g, i: (b, kblk0 + g)),
        pl.BlockSpec((n, dv), lambda b, g, i: (b, vblk0 + g)),
    ]
    args = [q, k, v]
    if ctx_kv is not None:
        k2, v2 = ctx_kv
        nc = k2.shape[0] // batch
        in_specs += [pl.BlockSpec((nc, dqk), lambda b, g, i: (b, g)),
                     pl.BlockSpec((nc, dv), lambda b, g, i: (b, g))]
        args += [k2, v2]
    return pl.pallas_call(
        functools.partial(_attn_kernel, rep=rep, dqk=dqk, dv=dv, tq=tq, tk=tk, ctx=ctx_kv is not None),
        grid=(batch, groups, nq),
        in_specs=in_specs,
        out_specs=pl.BlockSpec((tq, rep * dv), lambda b, g, i: (b * nq + i, g)),
        out_shape=jax.ShapeDtypeStruct((batch * n, groups * rep * dv), BF16),
        scratch_shapes=[pltpu.VMEM((rep * tq, dv), F32)],
        compiler_params=_params("arbitrary", "arbitrary", "arbitrary"),
        name=name,
    )(*args)


def _ret_kernel(lg_ref, q_ref, k_ref, v_ref, gate_ref, gn_ref, s0f_ref, s0b_ref, *rest, n, rope, emit_state):
    rest = list(rest)
    tabs = [rest.pop(0) for _ in range(3)] if rope else None
    o_ref = rest.pop(0)
    if emit_state:
        sfo_ref, sbo_ref = rest.pop(0), rest.pop(0)
    acc_ref, sf_ref, sb_ref = rest
    c_len = B_CHUNK
    nc = n // c_len
    h = pl.program_id(1)
    lgf = lg_ref[0, h]
    lgb = lg_ref[1, h]

    ii = lax.broadcasted_iota(jnp.int32, (c_len, c_len), 0)
    jj = lax.broadcasted_iota(jnp.int32, (c_len, c_len), 1)
    diff = (ii - jj).astype(F32)
    dmat = (jnp.where(diff >= 0, jnp.exp(jnp.maximum(diff, 0.0) * lgf), 0.0)
            + jnp.where(diff <= 0, jnp.exp(jnp.maximum(-diff, 0.0) * lgb), 0.0))
    ri = lax.broadcasted_iota(jnp.int32, (c_len, 1), 0).astype(F32)
    qdec_f = jnp.exp((ri + 1.0) * lgf)
    qdec_b = jnp.exp((c_len - ri) * lgb)
    kdec_f = jnp.exp((c_len - 1.0 - ri) * lgf)
    kdec_b = jnp.exp(ri * lgb)
    one = jnp.ones((1, 1), F32)
    cdec_f = jnp.exp(one * (c_len * lgf))
    cdec_b = jnp.exp(one * (c_len * lgb))
    kscale = B_QK_DIM ** -0.5

    def load(c):
        lo = pl.multiple_of(c * c_len, c_len)
        qc = q_ref[pl.ds(lo, c_len), :].astype(F32)
        kc = k_ref[pl.ds(lo, c_len), :].astype(F32)
        if rope:
            t = [tab[pl.ds(lo, c_len), :] for tab in tabs]
            qc = _rope(qc, t[0], t[1], t[2], B_QK_DIM // 4)
            kc = _rope(kc, t[0], t[1], t[2], B_QK_DIM // 4)
        return lo, qc.astype(BF16), kc * kscale, v_ref[pl.ds(lo, c_len), :]

    def state_update(s_ref, cdec, kc, vc):
        s_ref[...] = cdec * s_ref[...] + jnp.dot(kc.T.astype(BF16), vc, preferred_element_type=F32)

    sf_ref[...] = s0f_ref[...]
    sb_ref[...] = s0b_ref[...]

    def bwd(t, carry):
        lo, qc, kc, vc = load(nc - 1 - t)
        acc_ref[pl.ds(lo, c_len), :] = qdec_b * jnp.dot(qc, sb_ref[...].astype(BF16), preferred_element_type=F32)
        state_update(sb_ref, cdec_b, kc * kdec_b, vc)
        return carry

    lax.fori_loop(0, nc, bwd, 0)

    def fwd(c, carry):
        lo, qc, kc, vc = load(c)
        s = lax.dot_general(qc, kc.astype(BF16), (((1,), (1,)), ((), ())), preferred_element_type=F32) * dmat
        inner = jnp.dot(s.astype(BF16), vc, preferred_element_type=F32)
        cross = qdec_f * jnp.dot(qc, sf_ref[...].astype(BF16), preferred_element_type=F32)
        acc_ref[pl.ds(lo, c_len), :] += inner + cross
        state_update(sf_ref, cdec_f, kc * kdec_f, vc)
        return carry

    lax.fori_loop(0, nc, fwd, 0)

    if emit_state:
        sfo_ref[...] = sf_ref[...]
        sbo_ref[...] = sb_ref[...]
    gate = gate_ref[...].astype(F32)
    o_ref[...] = (gate * jax.nn.sigmoid(gate) * _rms(acc_ref[...], gn_ref[...])).astype(BF16)


def _retention(proj, lg, gn, s0f, s0b, layer, tabs, batch, n, *, emit_state):
    rope = tabs is not None
    qb, kb = OFF_BQ // B_QK_DIM, OFF_BK // B_QK_DIM
    vb, gb = OFF_BV // B_V_DIM, OFF_BG // B_V_DIM
    st_spec = pl.BlockSpec((None, None, B_QK_DIM, B_V_DIM), lambda b, h, lg_: (b, h, 0, 0))
    in_specs = [
        pl.BlockSpec((n, B_QK_DIM), lambda b, h, lg_: (b, qb + h)),
        pl.BlockSpec((n, B_QK_DIM), lambda b, h, lg_: (b, kb + h)),
        pl.BlockSpec((n, B_V_DIM), lambda b, h, lg_: (b, vb + h)),
        pl.BlockSpec((n, B_V_DIM), lambda b, h, lg_: (b, gb + h)),
        pl.BlockSpec((None, 1, B_V_DIM), lambda b, h, lg_: (layer, 0, h)),
        st_spec,
        st_spec,
    ]
    args = [proj, proj, proj, proj, gn, s0f, s0b]
    if rope:
        in_specs += [pl.BlockSpec((n, LANE), lambda b, h, lg_: (0, 0))] * 3
        args += list(tabs)
    out_specs = [pl.BlockSpec((n, B_V_DIM), lambda b, h, lg_: (b, h))]
    out_shape = [jax.ShapeDtypeStruct((batch * n, B_HEADS * B_V_DIM), BF16)]
    if emit_state:
        out_specs += [st_spec, st_spec]
        out_shape += [jax.ShapeDtypeStruct((batch, B_HEADS, B_QK_DIM, B_V_DIM), F32)] * 2
    return pl.pallas_call(
        functools.partial(_ret_kernel, n=n, rope=rope, emit_state=emit_state),
        grid_spec=pltpu.PrefetchScalarGridSpec(
            num_scalar_prefetch=1,
            grid=(batch, B_HEADS),
            in_specs=in_specs,
            out_specs=out_specs,
            scratch_shapes=[pltpu.VMEM((n, B_V_DIM), F32),
                            pltpu.VMEM((B_QK_DIM, B_V_DIM), F32),
                            pltpu.VMEM((B_QK_DIM, B_V_DIM), F32)],
        ),
        out_shape=out_shape,
        compiler_params=_params("arbitrary", "arbitrary"),
        name="retention",
    )(lg, *args)


def _merge_kernel(oa_ref, ob_ref, oc_ref, wa_ref, wb_ref, wc_ref, ga_ref, gb_ref, gc_ref, o_ref):
    def branch(o, w, g):
        return jax.nn.sigmoid(g[...].astype(F32)) * jnp.dot(o[...], w[...], preferred_element_type=F32)

    o_ref[...] = (branch(oa_ref, wa_ref, ga_ref) + branch(ob_ref, wb_ref, gb_ref)
                  + branch(oc_ref, wc_ref, gc_ref)).astype(BF16)


def _merge(oa, ob, oc, wa, wb, wc, proj, layer, d):
    m = oa.shape[0]
    tm = min(1024, m)
    tn = min(512, d)
    g0 = OFF_GATE // tn
    nb = d // tn
    o_spec = lambda o: pl.BlockSpec((tm, o.shape[1]), lambda i, j: (i, 0))
    w_spec = lambda w: pl.BlockSpec((None, w.shape[1], tn), lambda i, j: (layer, 0, j))
    g_spec = lambda br: pl.BlockSpec((tm, tn), lambda i, j: (i, g0 + br * nb + j))
    return pl.pallas_call(
        _merge_kernel,
        grid=(m // tm, nb),
        in_specs=[o_spec(oa), o_spec(ob), o_spec(oc), w_spec(wa), w_spec(wb), w_spec(wc),
                  g_spec(0), g_spec(1), g_spec(2)],
        out_specs=pl.BlockSpec((tm, tn), lambda i, j: (i, j)),
        out_shape=jax.ShapeDtypeStruct((m, d), BF16),
        compiler_params=_params("arbitrary", "arbitrary"),
        name="merge",
    )(oa, ob, oc, wa, wb, wc, proj, proj, proj)


def _rope_tables(n_tokens, dim, pad):
    pos = jnp.arange(n_tokens, dtype=jnp.int32)
    row = (pos // GRID_W).astype(F32)[:, None]
    col = (pos % GRID_W).astype(F32)[:, None]
    quarter = dim // 4
    inv = ROPE_THETA ** (-jnp.arange(quarter, dtype=F32) / quarter)[None, :]
    ang = jnp.concatenate([row * inv, row * inv, col * inv, col * inv], axis=1)
    cos, sin = jnp.cos(ang), jnp.sin(ang)
    first = (jnp.arange(dim) // quarter) % 2 == 0
    sin_lo = jnp.where(first[None, :], -sin, 0.0)
    sin_hi = jnp.where(first[None, :], 0.0, sin)
    if pad:
        z = jnp.zeros_like(cos)
        tabs = [jnp.concatenate([t, z], axis=1) for t in (cos, sin_lo, sin_hi)]
    else:
        tabs = [jnp.tile(t, (1, LANE // dim)) for t in (cos, sin_lo, sin_hi)]
    return tuple(tabs)


def kernel(x_prompt, x_sample, c, cache_attn_k, cache_attn_v, state_ret_fwd, state_ret_bwd, cache_mla_ckv, cache_mla_krope, c_ctx, w_mod, b_mod, g_pre_mix, g_post_mix, g_pre_mlp, g_post_mlp, w_in, attn_q_norm, attn_k_norm, ret_decay_fwd, ret_decay_bwd, ret_gn, mla_q_norm, mla_kv_norm, w_mla_uq, w_mla_ukv, w_branch_a, w_branch_b, w_branch_c, w_out, w_mlp_up, w_mlp_down):
    bp, n_p, d = x_prompt.shape
    bs, n_s, _ = x_sample.shape
    depth = w_in.shape[0]
    past = cache_attn_k.shape[2]
    assert OFF_GATE % min(512, d) == 0 and n_p % B_CHUNK == 0 and n_s % B_CHUNK == 0

    w_in_p = jnp.concatenate(
        [w_in[..., :IN_MAIN], jnp.zeros((depth, d, OFF_GATE - IN_MAIN), w_in.dtype), w_in[..., IN_MAIN:]],
        axis=-1).astype(BF16)
    w_uq = jnp.pad(w_mla_uq.reshape(depth, C_Q_RANK, C_HEADS, C_NOPE_DIM + C_ROPE_DIM),
                   ((0, 0), (0, 0), (0, 0), (0, C_QK_PAD - C_NOPE_DIM - C_ROPE_DIM)))
    w_uq = w_uq.reshape(depth, C_Q_RANK, C_HEADS * C_QK_PAD).astype(BF16)
    w_ukv = w_mla_ukv.astype(BF16)
    wa, wb, wc = w_branch_a.astype(BF16), w_branch_b.astype(BF16), w_branch_c.astype(BF16)
    w_o, w_up, w_dn = w_out.astype(BF16), w_mlp_up.astype(BF16), w_mlp_down.astype(BF16)
    vec = lambda g: g.reshape(depth, 1, g.shape[-1])
    g_pre_mix, g_post_mix, g_pre_mlp, g_post_mlp = map(vec, (g_pre_mix, g_post_mix, g_pre_mlp, g_post_mlp))
    gqn, gkn, gcq, gckv, gret = map(vec, (attn_q_norm, attn_k_norm, mla_q_norm, mla_kv_norm, ret_gn))
    lg = jnp.stack([jax.nn.log_sigmoid(ret_decay_fwd.astype(F32)),
                    jax.nn.log_sigmoid(ret_decay_bwd.astype(F32))], axis=1)

    rows = -(-(1 + bs) // 8) * 8
    cvec = jnp.zeros((rows, d), F32).at[0].set(c_ctx).at[1:1 + bs].set(c)
    mod = _modulation(cvec, w_mod, b_mod).reshape(depth, rows, 6, d)
    mod = jnp.pad(mod, ((0, 0), (0, 0), (0, MOD_ROWS - 6), (0, 0)))

    tabs128 = _rope_tables(n_s, HEAD_DIM, pad=False)
    tabs64 = _rope_tables(n_s, C_ROPE_DIM, pad=False)
    tabs64p = _rope_tables(n_s, C_ROPE_DIM, pad=True)

    ctx_k = cache_attn_k.astype(BF16).reshape(bs, depth, past, A_KV_HEADS * HEAD_DIM)
    ctx_v = cache_attn_v.astype(BF16).reshape(bs, depth, past, A_KV_HEADS * HEAD_DIM)
    ctx_kr = jnp.pad(cache_mla_krope, ((0, 0), (0, 0), (0, 0), (0, LANE - C_ROPE_DIM)))
    zero_state = jnp.zeros((bp, B_HEADS, B_QK_DIM, B_V_DIM), F32)

    streams = {
        "p": dict(x=x_prompt.reshape(bp * n_p, d), batch=bp, n=n_p, grp=lambda r: 0, latent=False),
        "s": dict(x=x_sample.reshape(bs * n_s, d), batch=bs, n=n_s, grp=lambda r: 1 + r // n_s, latent=True),
    }
    for st in streams.values():
        st["h"] = _prenorm(st["x"], g_pre_mix, mod, 0, st["grp"])
    caches = [[] for _ in range(6)]

    for l in range(depth):
        for st in streams.values():
            batch, n, grp, latent = st["batch"], st["n"], st["grp"], st["latent"]
            proj = _matmul(st["h"], w_in_p, l, tm=1024, tn=512, name="w_in")
            t128, t64, t64p = (tabs128, tabs64, tabs64p) if latent else (None, None, None)

            pa = _prep_a(proj, gqn, gkn, l, t128, n, cache=not latent)
            qa, ka = pa[0], pa[1]
            ctx_a = (ctx_k[:, l].reshape(bs * past, -1), ctx_v[:, l].reshape(bs * past, -1)) if latent else None
            o_a = _attention(qa, ka, 0, proj, OFF_AV // HEAD_DIM, ctx_a, batch, n,
                             groups=A_KV_HEADS, rep=A_HEADS // A_KV_HEADS, dqk=HEAD_DIM, dv=HEAD_DIM,
                             rows=1024, name="attn_a")

            if latent:
                s0f, s0b = state_ret_fwd[:, l], state_ret_bwd[:, l]
            else:
                s0f = s0b = zero_state
            rb = _retention(proj, lg[l], gret, s0f, s0b, l, t128, batch, n, emit_state=not latent)
            o_b = rb[0]

            qc = _prep_cq(proj, gcq, w_uq, l, t64p, n)
            pc = _prep_ckv(proj, OFF_CKV // C_KV_RANK, proj, OFF_CKR // LANE, gckv, w_ukv, l, t64, n,
                           norm=True, cache=not latent)
            kc, vc = pc[0], pc[1]
            ctx_c = None
            if latent:
                ctx_c = _prep_ckv(cache_mla_ckv[:, l].reshape(bs * past, C_KV_RANK), 0,
                                  ctx_kr[:, l].reshape(bs * past, LANE), 0, gckv, w_ukv, l, None, past,
                                  norm=False, cache=False)
            o_c = _attention(qc, kc, 0, vc, 0, ctx_c, batch, n,
                             groups=C_HEADS, rep=1, dqk=C_QK_PAD, dv=C_V_DIM, rows=1024, name="attn_c")

            if not latent:
                for dst, val in zip(caches, (pa[2], pa[3], rb[1], rb[2], pc[2], pc[3])):
                    dst.append(val)

            merged = _merge(o_a, o_b, o_c, wa, wb, wc, proj, l, d)
            x, h2 = _matmul_row(merged, w_o, st["x"], g_post_mix, g_pre_mlp, mod, l, l, grp,
                                gate_row=2, next_rows=(3, 4), tm=512, tk=2048, name="w_out")
            hid = _matmul(h2, w_up, l, tm=1024, tn=512, relu2=True, name="mlp_up")
            last = l == depth - 1
            res = _matmul_row(hid, w_dn, x, g_post_mlp, g_pre_mix, mod, l, min(l + 1, depth - 1), grp,
                              gate_row=5, next_rows=None if last else (0, 1), tm=512, tk=1024, name="mlp_down")
            st["x"] = res[0]
            st["h"] = None if last else res[1]

    y_p = streams["p"]["x"].reshape(bp, n_p, d)
    y_s = streams["s"]["x"].reshape(bs, n_s, d)
    nk, nv, sf, sb, ckv, kr = caches
    new_attn_k = jnp.stack(nk, axis=1).reshape(bp, n_p, depth, A_KV_HEADS, HEAD_DIM).transpose(0, 2, 1, 3, 4)
    new_attn_v = jnp.stack(nv, axis=1).reshape(bp, n_p, depth, A_KV_HEADS, HEAD_DIM).transpose(0, 2, 1, 3, 4)
    new_ret_fwd = jnp.stack(sf, axis=1)
    new_ret_bwd = jnp.stack(sb, axis=1)
    new_mla_ckv = jnp.stack(ckv, axis=1).reshape(bp, n_p, depth, C_KV_RANK).transpose(0, 2, 1, 3)
    new_mla_krope = jnp.stack(kr, axis=1).reshape(bp, n_p, depth, C_ROPE_DIM).transpose(0, 2, 1, 3)
    return (y_p, y_s, new_attn_k, new_attn_v, new_ret_fwd, new_ret_bwd, new_mla_ckv, new_mla_krope)
```

```python
import functools

import jax
import jax.numpy as jnp
from jax import lax
from jax.experimental import pallas as pl
from jax.experimental.pallas import tpu as pltpu

F32 = jnp.float32
BF16 = jnp.bfloat16

EPS = 1e-6
LOG2E = 1.4426950408889634
ROPE_THETA = 10000.0
GRID_W = 64
LANE = 128
HEAD_DIM = 128
A_HEADS = 8
A_KV_HEADS = 2
B_HEADS = 4
B_QK_DIM = 128
B_V_DIM = 256
B_CHUNK = 128
C_HEADS = 8
C_Q_RANK = 512
C_KV_RANK = 256
C_NOPE_DIM = 128
C_ROPE_DIM = 64
C_V_DIM = 128
C_QK_PAD = 256

OFF_AQ = 0
OFF_AK = OFF_AQ + A_HEADS * HEAD_DIM
OFF_AV = OFF_AK + A_KV_HEADS * HEAD_DIM
OFF_BQ = OFF_AV + A_KV_HEADS * HEAD_DIM
OFF_BK = OFF_BQ + B_HEADS * B_QK_DIM
OFF_BV = OFF_BK + B_HEADS * B_QK_DIM
OFF_BG = OFF_BV + B_HEADS * B_V_DIM
OFF_CQ = OFF_BG + B_HEADS * B_V_DIM
OFF_CKV = OFF_CQ + C_Q_RANK
OFF_CKR = OFF_CKV + C_KV_RANK
IN_MAIN = OFF_CKR + C_ROPE_DIM
MAIN_W = 5632
MOD_ROWS = 8

VMEM_LIMIT = 56 * 1024 * 1024


def _params(*sem):
    return pltpu.CompilerParams(dimension_semantics=sem, vmem_limit_bytes=VMEM_LIMIT)


def _rms(x, g):
    return x * lax.rsqrt(jnp.mean(x * x, axis=-1, keepdims=True) + EPS) * g


def _unroll(trips, want):
    while trips % want:
        want //= 2
    return want


def _rope(x, cos, sin_lo, sin_hi, shift):
    return (x * cos + pltpu.roll(x, LANE - shift, 1) * sin_lo + pltpu.roll(x, shift, 1) * sin_hi)


def _mod_kernel(c_ref, w_ref, b_ref, o_ref):
    c = c_ref[...]
    a = (c * jax.nn.sigmoid(c)).astype(BF16)
    o_ref[...] = jnp.dot(a, w_ref[...].astype(BF16), preferred_element_type=F32) + b_ref[...]


def _modulation(cvec, w_mod, b_mod):
    depth, d, n = w_mod.shape
    rows = cvec.shape[0]
    tn = min(1024, n)
    return pl.pallas_call(
        _mod_kernel,
        grid=(depth, n // tn),
        in_specs=[
            pl.BlockSpec((rows, d), lambda l, j: (0, 0)),
            pl.BlockSpec((None, d, tn), lambda l, j: (l, 0, j)),
            pl.BlockSpec((None, 1, tn), lambda l, j: (l, 0, j)),
        ],
        out_specs=pl.BlockSpec((None, rows, tn), lambda l, j: (l, 0, j)),
        out_shape=jax.ShapeDtypeStruct((depth, rows, n), F32),
        compiler_params=_params("arbitrary", "arbitrary"),
        name="modulation",
    )(cvec, w_mod, b_mod.reshape(depth, 1, n))


def _prenorm_kernel(x_ref, g_ref, mod_ref, h_ref):
    y = _rms(x_ref[...], g_ref[...])
    h_ref[...] = (y * (1.0 + mod_ref[1:2, :]) + mod_ref[0:1, :]).astype(BF16)


def _prenorm(x, g, mod, layer, grp):
    m, d = x.shape
    tm = min(512, m)
    return pl.pallas_call(
        _prenorm_kernel,
        grid=(m // tm,),
        in_specs=[
            pl.BlockSpec((tm, d), lambda i: (i, 0)),
            pl.BlockSpec((None, 1, d), lambda i: (layer, 0, 0)),
            pl.BlockSpec((None, None, MOD_ROWS, d), lambda i: (layer, grp(i * tm), 0, 0)),
        ],
        out_specs=pl.BlockSpec((tm, d), lambda i: (i, 0)),
        out_shape=jax.ShapeDtypeStruct((m, d), BF16),
        compiler_params=_params("arbitrary"),
        name="prenorm",
    )(x, g, mod)


def _mm_kernel(a_ref, w_ref, o_ref, *, relu2):
    y = jnp.dot(a_ref[...], w_ref[...], preferred_element_type=F32)
    if relu2:
        y = jnp.square(jnp.maximum(y, 0.0))
    o_ref[...] = y.astype(o_ref.dtype)


def _matmul(a, w, layer, *, tm, tn, relu2=False, name):
    m, k = a.shape
    n = w.shape[-1]
    tm, tn = min(tm, m), min(tn, n)
    return pl.pallas_call(
        functools.partial(_mm_kernel, relu2=relu2),
        grid=(m // tm, n // tn),
        in_specs=[
            pl.BlockSpec((tm, k), lambda i, j: (i, 0)),
            pl.BlockSpec((None, k, tn), lambda i, j: (layer, 0, j)),
        ],
        out_specs=pl.BlockSpec((tm, tn), lambda i, j: (i, j)),
        out_shape=jax.ShapeDtypeStruct((m, n), BF16),
        compiler_params=_params("arbitrary", "arbitrary"),
        name=name,
    )(a, w)


def _mm_row_kernel(a_ref, w_ref, x_ref, gpost_ref, mod_ref, gnext_ref, modn_ref, *rest,
                   gate_row, next_rows, nk):
    if next_rows is None:
        xo_ref, acc_ref = rest
    else:
        xo_ref, ho_ref, acc_ref = rest
    k = pl.program_id(1)

    @pl.when(k == 0)
    def _():
        acc_ref[...] = jnp.zeros_like(acc_ref)

    acc_ref[...] += jnp.dot(a_ref[...], w_ref[...], preferred_element_type=F32)

    @pl.when(k == nk - 1)
    def _():
        xn = x_ref[...] + mod_ref[gate_row:gate_row + 1, :] * _rms(acc_ref[...], gpost_ref[...])
        xo_ref[...] = xn
        if next_rows is not None:
            shift_row, scale_row = next_rows
            y = _rms(xn, gnext_ref[...])
            ho_ref[...] = (y * (1.0 + modn_ref[scale_row:scale_row + 1, :])
                           + modn_ref[shift_row:shift_row + 1, :]).astype(BF16)


def _matmul_row(a, w, x, gpost, gnext, mod, layer, next_layer, grp, *, gate_row, next_rows, tm, tk, name):
    m, kdim = a.shape
    d = w.shape[-1]
    tm, tk = min(tm, m), min(tk, kdim)
    nk = kdim // tk
    vec = pl.BlockSpec((None, 1, d), lambda i, k: (layer, 0, 0))
    vecn = pl.BlockSpec((None, 1, d), lambda i, k: (next_layer, 0, 0))
    out_specs = [pl.BlockSpec((tm, d), lambda i, k: (i, 0))]
    out_shape = [jax.ShapeDtypeStruct((m, d), F32)]
    if next_rows is not None:
        out_specs.append(pl.BlockSpec((tm, d), lambda i, k: (i, 0)))
        out_shape.append(jax.ShapeDtypeStruct((m, d), BF16))
    return pl.pallas_call(
        functools.partial(_mm_row_kernel, gate_row=gate_row, next_rows=next_rows, nk=nk),
        grid=(m // tm, nk),
        in_specs=[
            pl.BlockSpec((tm, tk), lambda i, k: (i, k)),
            pl.BlockSpec((None, tk, d), lambda i, k: (layer, k, 0)),
            pl.BlockSpec((tm, d), lambda i, k: (i, 0)),
            vec,
            pl.BlockSpec((None, None, MOD_ROWS, d), lambda i, k: (layer, grp(i * tm), 0, 0)),
            vecn,
            pl.BlockSpec((None, None, MOD_ROWS, d), lambda i, k: (next_layer, grp(i * tm), 0, 0)),
        ],
        out_specs=out_specs,
        out_shape=out_shape,
        scratch_shapes=[pltpu.VMEM((tm, d), F32)],
        compiler_params=_params("arbitrary", "arbitrary"),
        name=name,
    )(a, w, x, gpost, mod, gnext, mod)


def _prep_a_kernel(p_ref, gq_ref, gk_ref, *rest, rope, cache):
    rest = list(rest)
    tabs = [rest.pop(0) for _ in range(3)] if rope else None
    q_ref, k_ref = rest[:2]
    scale = HEAD_DIM ** -0.5 * LOG2E
    for h in range(A_HEADS):
        y = _rms(p_ref[:, h * HEAD_DIM:(h + 1) * HEAD_DIM].astype(F32), gq_ref[...])
        if rope:
            y = _rope(y, tabs[0][...], tabs[1][...], tabs[2][...], HEAD_DIM // 4)
        q_ref[:, h * HEAD_DIM:(h + 1) * HEAD_DIM] = (y * scale).astype(BF16)
    for g in range(A_KV_HEADS):
        lo = OFF_AK + g * HEAD_DIM
        y = _rms(p_ref[:, lo:lo + HEAD_DIM].astype(F32), gk_ref[...])
        if cache:
            rest[2][:, g * HEAD_DIM:(g + 1) * HEAD_DIM] = y
        if rope:
            y = _rope(y, tabs[0][...], tabs[1][...], tabs[2][...], HEAD_DIM // 4)
        k_ref[:, g * HEAD_DIM:(g + 1) * HEAD_DIM] = y.astype(BF16)
    if cache:
        rest[3][...] = p_ref[:, OFF_AV:OFF_BQ].astype(F32)


def _prep_a(proj, gq, gk, layer, tabs, n_per, *, cache):
    m = proj.shape[0]
    tr = min(512, n_per)
    rope = tabs is not None
    width = OFF_BQ
    kvw = A_KV_HEADS * HEAD_DIM
    in_specs = [
        pl.BlockSpec((tr, width), lambda i: (i, 0)),
        pl.BlockSpec((None, 1, HEAD_DIM), lambda i: (layer, 0, 0)),
        pl.BlockSpec((None, 1, HEAD_DIM), lambda i: (layer, 0, 0)),
    ]
    args = [proj, gq, gk]
    if rope:
        nt = n_per // tr
        in_specs += [pl.BlockSpec((tr, LANE), lambda i: (i % nt, 0))] * 3
        args += list(tabs)
    out_specs = [pl.BlockSpec((tr, A_HEADS * HEAD_DIM), lambda i: (i, 0)),
                 pl.BlockSpec((tr, kvw), lambda i: (i, 0))]
    out_shape = [jax.ShapeDtypeStruct((m, A_HEADS * HEAD_DIM), BF16),
                 jax.ShapeDtypeStruct((m, kvw), BF16)]
    if cache:
        out_specs += [pl.BlockSpec((tr, kvw), lambda i: (i, 0))] * 2
        out_shape += [jax.ShapeDtypeStruct((m, kvw), F32)] * 2
    return pl.pallas_call(
        functools.partial(_prep_a_kernel, rope=rope, cache=cache),
        grid=(m // tr,),
        in_specs=in_specs,
        out_specs=out_specs,
        out_shape=out_shape,
        compiler_params=_params("arbitrary"),
        name="prep_a",
    )(*args)


def _prep_cq_kernel(p_ref, g_ref, w_ref, *rest, rope):
    rest = list(rest)
    tabs = [rest.pop(0) for _ in range(3)] if rope else None
    q_ref = rest[0]
    scale = (C_NOPE_DIM + C_ROPE_DIM) ** -0.5 * LOG2E
    y = _rms(p_ref[...].astype(F32), g_ref[...]).astype(BF16)
    z = jnp.dot(y, w_ref[...], preferred_element_type=F32)
    for h in range(C_HEADS):
        lo = h * C_QK_PAD
        q_ref[:, lo:lo + C_NOPE_DIM] = (z[:, lo:lo + C_NOPE_DIM] * scale).astype(BF16)
        r = z[:, lo + C_NOPE_DIM:lo + C_QK_PAD]
        if rope:
            r = _rope(r, tabs[0][...], tabs[1][...], tabs[2][...], C_ROPE_DIM // 4)
        q_ref[:, lo + C_NOPE_DIM:lo + C_QK_PAD] = (r * scale).astype(BF16)


def _prep_cq(proj, g, w_uq, layer, tabs, n_per):
    m = proj.shape[0]
    tr = min(512, n_per)
    rope = tabs is not None
    width = C_HEADS * C_QK_PAD
    in_specs = [
        pl.BlockSpec((tr, C_Q_RANK), lambda i: (i, OFF_CQ // C_Q_RANK)),
        pl.BlockSpec((None, 1, C_Q_RANK), lambda i: (layer, 0, 0)),
        pl.BlockSpec((None, C_Q_RANK, width), lambda i: (layer, 0, 0)),
    ]
    args = [proj, g, w_uq]
    if rope:
        nt = n_per // tr
        in_specs += [pl.BlockSpec((tr, LANE), lambda i: (i % nt, 0))] * 3
        args += list(tabs)
    return pl.pallas_call(
        functools.partial(_prep_cq_kernel, rope=rope),
        grid=(m // tr,),
        in_specs=in_specs,
        out_specs=pl.BlockSpec((tr, width), lambda i: (i, 0)),
        out_shape=jax.ShapeDtypeStruct((m, width), BF16),
        compiler_params=_params("arbitrary"),
        name="prep_cq",
    )(*args)


def _prep_ckv_kernel(ckv_ref, kr_ref, g_ref, w_ref, *rest, norm, rope, cache):
    rest = list(rest)
    tabs = [rest.pop(0) for _ in range(3)] if rope else None
    kc_ref, vc_ref = rest[:2]
    x = ckv_ref[...].astype(F32)
    if norm:
        x = _rms(x, g_ref[...])
    kr = kr_ref[...].astype(F32)
    if cache:
        rest[2][...] = x
        rest[3][...] = kr[:, :C_ROPE_DIM]
    if rope:
        kr = _rope(kr, tabs[0][...], tabs[1][...], tabs[2][...], C_ROPE_DIM // 4)
    krb = kr.astype(BF16)
    z = jnp.dot(x.astype(BF16), w_ref[...], preferred_element_type=F32)
    up = C_NOPE_DIM + C_V_DIM
    for h in range(C_HEADS):
        kc_ref[:, h * C_QK_PAD:h * C_QK_PAD + C_NOPE_DIM] = z[:, h * up:h * up + C_NOPE_DIM].astype(BF16)
        kc_ref[:, h * C_QK_PAD + C_NOPE_DIM:(h + 1) * C_QK_PAD] = krb
        vc_ref[:, h * C_V_DIM:(h + 1) * C_V_DIM] = z[:, h * up + C_NOPE_DIM:(h + 1) * up].astype(BF16)


def _prep_ckv(ckv_src, ckv_blk, kr_src, kr_blk, g, w_ukv, layer, tabs, n_per, *, norm, cache):
    m = ckv_src.shape[0]
    tr = min(512, n_per)
    rope = tabs is not None
    in_specs = [
        pl.BlockSpec((tr, C_KV_RANK), lambda i: (i, ckv_blk)),
        pl.BlockSpec((tr, LANE), lambda i: (i, kr_blk)),
        pl.BlockSpec((None, 1, C_KV_RANK), lambda i: (layer, 0, 0)),
        pl.BlockSpec((None, C_KV_RANK, C_HEADS * (C_NOPE_DIM + C_V_DIM)), lambda i: (layer, 0, 0)),
    ]
    args = [ckv_src, kr_src, g, w_ukv]
    if rope:
        nt = n_per // tr
        in_specs += [pl.BlockSpec((tr, LANE), lambda i: (i % nt, 0))] * 3
        args += list(tabs)
    out_specs = [pl.BlockSpec((tr, C_HEADS * C_QK_PAD), lambda i: (i, 0)),
                 pl.BlockSpec((tr, C_HEADS * C_V_DIM), lambda i: (i, 0))]
    out_shape = [jax.ShapeDtypeStruct((m, C_HEADS * C_QK_PAD), BF16),
                 jax.ShapeDtypeStruct((m, C_HEADS * C_V_DIM), BF16)]
    if cache:
        out_specs += [pl.BlockSpec((tr, C_KV_RANK), lambda i: (i, 0)),
                      pl.BlockSpec((tr, C_ROPE_DIM), lambda i: (i, 0))]
        out_shape += [jax.ShapeDtypeStruct((m, C_KV_RANK), F32),
                      jax.ShapeDtypeStruct((m, C_ROPE_DIM), F32)]
    return pl.pallas_call(
        functools.partial(_prep_ckv_kernel, norm=norm, rope=rope, cache=cache),
        grid=(m // tr,),
        in_specs=in_specs,
        out_specs=out_specs,
        out_shape=out_shape,
        compiler_params=_params("arbitrary"),
        name="prep_ckv",
    )(*args)


def _attn_kernel(q_ref, k_ref, v_ref, *rest, rep, dqk, dv, tq, tk, ctx):
    if ctx:
        k2_ref, v2_ref, o_ref, acc_ref, s0_ref, s1_ref = rest
    else:
        o_ref, acc_ref, s0_ref, s1_ref = rest
    rows = rep * tq
    if rep == 1:
        q = q_ref[...]
    else:
        q = jnp.concatenate([q_ref[:, r * dqk:(r + 1) * dqk] for r in range(rep)], axis=0)

    chunks = [(k_ref, v_ref, c * tk, tk) for c in range(k_ref.shape[0] // tk)]
    if ctx:
        nctx = k2_ref.shape[0]
        chunks += [(k2_ref, v2_ref, lo, min(tk, nctx - lo)) for lo in range(0, nctx, tk)]
    s_refs = (s0_ref, s1_ref)

    def scores(idx):
        kr, _, lo, size = chunks[idx]
        s_refs[idx % 2][:, :size] = lax.dot_general(
            q, kr[lo:lo + size, :], (((1,), (1,)), ((), ())), preferred_element_type=F32)

    scores(0)
    m = None
    for idx, (_, vr, lo, size) in enumerate(chunks):
        if idx + 1 < len(chunks):
            scores(idx + 1)
        s = s_refs[idx % 2][:, :size]
        m_blk = jnp.max(s, axis=-1, keepdims=True)
        m_new = m_blk if m is None else jnp.maximum(m, m_blk)
        p = jnp.exp2(s - m_new).astype(BF16)
        v_ext = jnp.concatenate([vr[lo:lo + size, :], jnp.ones((size, dv), BF16)], axis=1)
        pv = jnp.dot(p, v_ext, preferred_element_type=F32)
        if m is None:
            acc_ref[...] = pv
        else:
            acc_ref[...] = jnp.exp2(m - m_new) * acc_ref[...] + pv
        m = m_new
    o = acc_ref[:, :dv] / acc_ref[:, dv:]
    for r in range(rep):
        o_ref[:, r * dv:(r + 1) * dv] = o[r * tq:(r + 1) * tq].astype(o_ref.dtype)


def _attention(q, k, kblk0, v, vblk0, ctx_kv, batch, n, *, groups, rep, dqk, dv, rows, name):
    tq = min(rows // rep, n)
    tk = min(512, n)
    nq = n // tq
    in_specs = [
        pl.BlockSpec((tq, rep * dqk), lambda b, g, i: (b * nq + i, g)),
        pl.BlockSpec((n, dqk), lambda b, g, i: (b, kblk0 + g)),
        pl.BlockSpec((n, dv), lambda b, g, i: (b, vblk0 + g)),
    ]
    args = [q, k, v]
    if ctx_kv is not None:
        k2, v2 = ctx_kv
        nc = k2.shape[0] // batch
        in_specs += [pl.BlockSpec((nc, dqk), lambda b, g, i: (b, g)),
                     pl.BlockSpec((nc, dv), lambda b, g, i: (b, g))]
        args += [k2, v2]
    return pl.pallas_call(
        functools.partial(_attn_kernel, rep=rep, dqk=dqk, dv=dv, tq=tq, tk=tk, ctx=ctx_kv is not None),
        grid=(batch, groups, nq),
        in_specs=in_specs,
        out_specs=pl.BlockSpec((tq, rep * dv), lambda b, g, i: (b * nq + i, g)),
        out_shape=jax.ShapeDtypeStruct((batch * n, groups * rep * dv), BF16),
        scratch_shapes=[pltpu.VMEM((rep * tq, 2 * dv), F32),
                        pltpu.VMEM((rep * tq, tk), F32),
                        pltpu.VMEM((rep * tq, tk), F32)],
        compiler_params=_params("arbitrary", "arbitrary", "arbitrary"),
        name=name,
    )(*args)


def _ret_kernel(lg_ref, q_ref, k_ref, v_ref, gate_ref, gn_ref, s0f_ref, s0b_ref, *rest, n, rope, emit_state):
    rest = list(rest)
    tabs = [rest.pop(0) for _ in range(3)] if rope else None
    o_ref = rest.pop(0)
    if emit_state:
        sfo_ref, sbo_ref = rest.pop(0), rest.pop(0)
    kvf_ref, kvb_ref, state_ref = rest
    c_len = B_CHUNK
    nc = n // c_len
    h = pl.program_id(1)
    lgf = lg_ref[0, h]
    lgb = lg_ref[1, h]

    ii = lax.broadcasted_iota(jnp.int32, (c_len, c_len), 0)
    jj = lax.broadcasted_iota(jnp.int32, (c_len, c_len), 1)
    diff = (ii - jj).astype(F32)
    dmat = (jnp.where(diff >= 0, jnp.exp(jnp.maximum(diff, 0.0) * lgf), 0.0)
            + jnp.where(diff <= 0, jnp.exp(jnp.maximum(-diff, 0.0) * lgb), 0.0))
    ri = lax.broadcasted_iota(jnp.int32, (c_len, 1), 0).astype(F32)
    qdec_f = jnp.exp((ri + 1.0) * lgf)
    qdec_b = jnp.exp((c_len - ri) * lgb)
    kdec_f = jnp.exp((c_len - 1.0 - ri) * lgf)
    kdec_b = jnp.exp(ri * lgb)
    one = jnp.ones((1, 1), F32)
    cdec_f = jnp.exp(one * (c_len * lgf))
    cdec_b = jnp.exp(one * (c_len * lgb))
    kscale = B_QK_DIM ** -0.5

    def roped(ref, lo):
        x = ref[pl.ds(lo, c_len), :].astype(F32)
        if rope:
            t = [tab[pl.ds(lo, c_len), :] for tab in tabs]
            x = _rope(x, t[0], t[1], t[2], B_QK_DIM // 4)
        return x

    def increments(c, carry):
        lo = pl.multiple_of(c * c_len, c_len)
        kc = roped(k_ref, lo) * kscale
        kk = jnp.concatenate([kc * kdec_f, kc * kdec_b], axis=1).T.astype(BF16)
        kv = jnp.dot(kk, v_ref[pl.ds(lo, c_len), :], preferred_element_type=F32)
        kvf_ref[c] = kv[:B_QK_DIM]
        kvb_ref[c] = kv[B_QK_DIM:]
        return carry

    lax.fori_loop(0, nc, increments, 0, unroll=_unroll(nc, 8))

    def scan_f(c, s):
        state_ref[c, :B_QK_DIM, :] = s.astype(BF16)
        return cdec_f * s + kvf_ref[c]

    def scan_b(t, s):
        c = nc - 1 - t
        state_ref[c, B_QK_DIM:, :] = s.astype(BF16)
        return cdec_b * s + kvb_ref[c]

    s_f = lax.fori_loop(0, nc, scan_f, s0f_ref[...])
    s_b = lax.fori_loop(0, nc, scan_b, s0b_ref[...])
    if emit_state:
        sfo_ref[...] = s_f
        sbo_ref[...] = s_b

    def outputs(c, carry):
        lo = pl.multiple_of(c * c_len, c_len)
        qc = roped(q_ref, lo)
        kc = (roped(k_ref, lo) * kscale).astype(BF16)
        s = lax.dot_general(qc.astype(BF16), kc, (((1,), (1,)), ((), ())), preferred_element_type=F32) * dmat
        o = jnp.dot(s.astype(BF16), v_ref[pl.ds(lo, c_len), :], preferred_element_type=F32)
        qq = jnp.concatenate([qc * qdec_f, qc * qdec_b], axis=1).astype(BF16)
        o = o + jnp.dot(qq, state_ref[c], preferred_element_type=F32)
        gate = gate_ref[pl.ds(lo, c_len), :].astype(F32)
        o_ref[pl.ds(lo, c_len), :] = (gate * jax.nn.sigmoid(gate) * _rms(o, gn_ref[...])).astype(BF16)
        return carry

    lax.fori_loop(0, nc, outputs, 0, unroll=_unroll(nc, 4))


def _retention(proj, lg, gn, s0f, s0b, layer, tabs, batch, n, *, emit_state):
    rope = tabs is not None
    qb, kb = OFF_BQ // B_QK_DIM, OFF_BK // B_QK_DIM
    vb, gb = OFF_BV // B_V_DIM, OFF_BG // B_V_DIM
    st_spec = pl.BlockSpec((None, None, B_QK_DIM, B_V_DIM), lambda b, h, lg_: (b, h, 0, 0))
    in_specs = [
        pl.BlockSpec((n, B_QK_DIM), lambda b, h, lg_: (b, qb + h)),
        pl.BlockSpec((n, B_QK_DIM), lambda b, h, lg_: (b, kb + h)),
        pl.BlockSpec((n, B_V_DIM), lambda b, h, lg_: (b, vb + h)),
        pl.BlockSpec((n, B_V_DIM), lambda b, h, lg_: (b, gb + h)),
        pl.BlockSpec((None, 1, B_V_DIM), lambda b, h, lg_: (layer, 0, h)),
        st_spec,
        st_spec,
    ]
    args = [proj, proj, proj, proj, gn, s0f, s0b]
    if rope:
        in_specs += [pl.BlockSpec((n, LANE), lambda b, h, lg_: (0, 0))] * 3
        args += list(tabs)
    out_specs = [pl.BlockSpec((n, B_V_DIM), lambda b, h, lg_: (b, h))]
    out_shape = [jax.ShapeDtypeStruct((batch * n, B_HEADS * B_V_DIM), BF16)]
    if emit_state:
        out_specs += [st_spec, st_spec]
        out_shape += [jax.ShapeDtypeStruct((batch, B_HEADS, B_QK_DIM, B_V_DIM), F32)] * 2
    return pl.pallas_call(
        functools.partial(_ret_kernel, n=n, rope=rope, emit_state=emit_state),
        grid_spec=pltpu.PrefetchScalarGridSpec(
            num_scalar_prefetch=1,
            grid=(batch, B_HEADS),
            in_specs=in_specs,
            out_specs=out_specs,
            scratch_shapes=[pltpu.VMEM((n // B_CHUNK, B_QK_DIM, B_V_DIM), F32),
                            pltpu.VMEM((n // B_CHUNK, B_QK_DIM, B_V_DIM), F32),
                            pltpu.VMEM((n // B_CHUNK, 2 * B_QK_DIM, B_V_DIM), BF16)],
        ),
        out_shape=out_shape,
        compiler_params=_params("arbitrary", "arbitrary"),
        name="retention",
    )(lg, *args)


def _merge_kernel(oa_ref, ob_ref, oc_ref, wa_ref, wb_ref, wc_ref, ga_ref, gb_ref, gc_ref, o_ref):
    def branch(o, w, g):
        return jax.nn.sigmoid(g[...].astype(F32)) * jnp.dot(o[...], w[...], preferred_element_type=F32)

    o_ref[...] = (branch(oa_ref, wa_ref, ga_ref) + branch(ob_ref, wb_ref, gb_ref)
                  + branch(oc_ref, wc_ref, gc_ref)).astype(BF16)


def _merge(oa, ob, oc, wa, wb, wc, gates, layer, d):
    m = oa.shape[0]
    tm = min(1024, m)
    tn = min(512, d)
    nb = d // tn
    o_spec = lambda o: pl.BlockSpec((tm, o.shape[1]), lambda i, j: (i, 0))
    w_spec = lambda w: pl.BlockSpec((None, w.shape[1], tn), lambda i, j: (layer, 0, j))
    g_spec = lambda br: pl.BlockSpec((tm, tn), lambda i, j: (i, br * nb + j))
    return pl.pallas_call(
        _merge_kernel,
        grid=(m // tm, nb),
        in_specs=[o_spec(oa), o_spec(ob), o_spec(oc), w_spec(wa), w_spec(wb), w_spec(wc),
                  g_spec(0), g_spec(1), g_spec(2)],
        out_specs=pl.BlockSpec((tm, tn), lambda i, j: (i, j)),
        out_shape=jax.ShapeDtypeStruct((m, d), BF16),
        compiler_params=_params("arbitrary", "arbitrary"),
        name="merge",
    )(oa, ob, oc, wa, wb, wc, gates, gates, gates)


def _rope_tables(n_tokens, dim, pad):
    pos = jnp.arange(n_tokens, dtype=jnp.int32)
    row = (pos // GRID_W).astype(F32)[:, None]
    col = (pos % GRID_W).astype(F32)[:, None]
    quarter = dim // 4
    inv = ROPE_THETA ** (-jnp.arange(quarter, dtype=F32) / quarter)[None, :]
    ang = jnp.concatenate([row * inv, row * inv, col * inv, col * inv], axis=1)
    cos, sin = jnp.cos(ang), jnp.sin(ang)
    first = (jnp.arange(dim) // quarter) % 2 == 0
    sin_lo = jnp.where(first[None, :], -sin, 0.0)
    sin_hi = jnp.where(first[None, :], 0.0, sin)
    if pad:
        z = jnp.zeros_like(cos)
        tabs = [jnp.concatenate([t, z], axis=1) for t in (cos, sin_lo, sin_hi)]
    else:
        tabs = [jnp.tile(t, (1, LANE // dim)) for t in (cos, sin_lo, sin_hi)]
    return tuple(tabs)


def kernel(x_prompt, x_sample, c, cache_attn_k, cache_attn_v, state_ret_fwd, state_ret_bwd, cache_mla_ckv, cache_mla_krope, c_ctx, w_mod, b_mod, g_pre_mix, g_post_mix, g_pre_mlp, g_post_mlp, w_in, attn_q_norm, attn_k_norm, ret_decay_fwd, ret_decay_bwd, ret_gn, mla_q_norm, mla_kv_norm, w_mla_uq, w_mla_ukv, w_branch_a, w_branch_b, w_branch_c, w_out, w_mlp_up, w_mlp_down):
    bp, n_p, d = x_prompt.shape
    bs, n_s, _ = x_sample.shape
    depth = w_in.shape[0]
    past = cache_attn_k.shape[2]
    assert n_p % B_CHUNK == 0 and n_s % B_CHUNK == 0

    w_main = w_in[..., :MAIN_W].astype(BF16)
    w_gate = w_in[..., IN_MAIN:].astype(BF16)
    w_uq = jnp.pad(w_mla_uq.reshape(depth, C_Q_RANK, C_HEADS, C_NOPE_DIM + C_ROPE_DIM),
                   ((0, 0), (0, 0), (0, 0), (0, C_QK_PAD - C_NOPE_DIM - C_ROPE_DIM)))
    w_uq = w_uq.reshape(depth, C_Q_RANK, C_HEADS * C_QK_PAD).astype(BF16)
    w_ukv = w_mla_ukv.astype(BF16)
    wa, wb, wc = w_branch_a.astype(BF16), w_branch_b.astype(BF16), w_branch_c.astype(BF16)
    w_o, w_up, w_dn = w_out.astype(BF16), w_mlp_up.astype(BF16), w_mlp_down.astype(BF16)
    vec = lambda g: g.reshape(depth, 1, g.shape[-1])
    g_pre_mix, g_post_mix, g_pre_mlp, g_post_mlp = map(vec, (g_pre_mix, g_post_mix, g_pre_mlp, g_post_mlp))
    gqn, gkn, gcq, gckv, gret = map(vec, (attn_q_norm, attn_k_norm, mla_q_norm, mla_kv_norm, ret_gn))
    lg = jnp.stack([jax.nn.log_sigmoid(ret_decay_fwd.astype(F32)),
                    jax.nn.log_sigmoid(ret_decay_bwd.astype(F32))], axis=1)

    rows = -(-(1 + bs) // 8) * 8
    cvec = jnp.zeros((rows, d), F32).at[0].set(c_ctx).at[1:1 + bs].set(c)
    mod = _modulation(cvec, w_mod, b_mod).reshape(depth, rows, 6, d)
    mod = jnp.pad(mod, ((0, 0), (0, 0), (0, MOD_ROWS - 6), (0, 0)))

    tabs128 = _rope_tables(n_s, HEAD_DIM, pad=False)
    tabs64 = _rope_tables(n_s, C_ROPE_DIM, pad=False)
    tabs64p = _rope_tables(n_s, C_ROPE_DIM, pad=True)

    ctx_k = cache_attn_k.astype(BF16).reshape(bs, depth, past, A_KV_HEADS * HEAD_DIM)
    ctx_v = cache_attn_v.astype(BF16).reshape(bs, depth, past, A_KV_HEADS * HEAD_DIM)
    ctx_kr = jnp.pad(cache_mla_krope, ((0, 0), (0, 0), (0, 0), (0, LANE - C_ROPE_DIM)))
    zero_state = jnp.zeros((bp, B_HEADS, B_QK_DIM, B_V_DIM), F32)

    streams = {
        "p": dict(x=x_prompt.reshape(bp * n_p, d), batch=bp, n=n_p, grp=lambda r: 0, latent=False),
        "s": dict(x=x_sample.reshape(bs * n_s, d), batch=bs, n=n_s, grp=lambda r: 1 + r // n_s, latent=True),
    }
    for st in streams.values():
        st["h"] = _prenorm(st["x"], g_pre_mix, mod, 0, st["grp"])
    caches = [[] for _ in range(6)]

    for l in range(depth):
        for st in streams.values():
            batch, n, grp, latent = st["batch"], st["n"], st["grp"], st["latent"]
            proj = _matmul(st["h"], w_main, l, tm=1024, tn=512, name="w_in")
            gates = _matmul(st["h"], w_gate, l, tm=1024, tn=512, name="w_gates")
            t128, t64, t64p = (tabs128, tabs64, tabs64p) if latent else (None, None, None)

            pa = _prep_a(proj, gqn, gkn, l, t128, n, cache=not latent)
            qa, ka = pa[0], pa[1]
            ctx_a = (ctx_k[:, l].reshape(bs * past, -1), ctx_v[:, l].reshape(bs * past, -1)) if latent else None
            o_a = _attention(qa, ka, 0, proj, OFF_AV // HEAD_DIM, ctx_a, batch, n,
                             groups=A_KV_HEADS, rep=A_HEADS // A_KV_HEADS, dqk=HEAD_DIM, dv=HEAD_DIM,
                             rows=1024, name="attn_a")

            if latent:
                s0f, s0b = state_ret_fwd[:, l], state_ret_bwd[:, l]
            else:
                s0f = s0b = zero_state
            rb = _retention(proj, lg[l], gret, s0f, s0b, l, t128, batch, n, emit_state=not latent)
            o_b = rb[0]

            qc = _prep_cq(proj, gcq, w_uq, l, t64p, n)
            pc = _prep_ckv(proj, OFF_CKV // C_KV_RANK, proj, OFF_CKR // LANE, gckv, w_ukv, l, t64, n,
                           norm=True, cache=not latent)
            kc, vc = pc[0], pc[1]
            ctx_c = None
            if latent:
                ctx_c = _prep_ckv(cache_mla_ckv[:, l].reshape(bs * past, C_KV_RANK), 0,
                                  ctx_kr[:, l].reshape(bs * past, LANE), 0, gckv, w_ukv, l, None, past,
                                  norm=False, cache=False)
            o_c = _attention(qc, kc, 0, vc, 0, ctx_c, batch, n,
                             groups=C_HEADS, rep=1, dqk=C_QK_PAD, dv=C_V_DIM, rows=1024, name="attn_c")

            if not latent:
                for dst, val in zip(caches, (pa[2], pa[3], rb[1], rb[2], pc[2], pc[3])):
                    dst.append(val)

            merged = _merge(o_a, o_b, o_c, wa, wb, wc, gates, l, d)
            x, h2 = _matmul_row(merged, w_o, st["x"], g_post_mix, g_pre_mlp, mod, l, l, grp,
                                gate_row=2, next_rows=(3, 4), tm=512, tk=2048, name="w_out")
            hid = _matmul(h2, w_up, l, tm=1024, tn=512, relu2=True, name="mlp_up")
            last = l == depth - 1
            res = _matmul_row(hid, w_dn, x, g_post_mlp, g_pre_mix, mod, l, min(l + 1, depth - 1), grp,
                              gate_row=5, next_rows=None if last else (0, 1), tm=512, tk=1024, name="mlp_down")
            st["x"] = res[0]
            st["h"] = None if last else res[1]

    y_p = streams["p"]["x"].reshape(bp, n_p, d)
    y_s = streams["s"]["x"].reshape(bs, n_s, d)
    nk, nv, sf, sb, ckv, kr = caches
    new_attn_k = jnp.stack(nk, axis=1).reshape(bp, n_p, depth, A_KV_HEADS, HEAD_DIM).transpose(0, 2, 1, 3, 4)
    new_attn_v = jnp.stack(nv, axis=1).reshape(bp, n_p, depth, A_KV_HEADS, HEAD_DIM).transpose(0, 2, 1, 3, 4)
    new_ret_fwd = jnp.stack(sf, axis=1)
    new_ret_bwd = jnp.stack(sb, axis=1)
    new_mla_ckv = jnp.stack(ckv, axis=1).reshape(bp, n_p, depth, C_KV_RANK).transpose(0, 2, 1, 3)
    new_mla_krope = jnp.stack(kr, axis=1).reshape(bp, n_p, depth, C_ROPE_DIM).transpose(0, 2, 1, 3)
    return (y_p, y_s, new_attn_k, new_attn_v, new_ret_fwd, new_ret_bwd, new_mla_ckv, new_mla_krope)
```

```python
import functools

import jax
import jax.numpy as jnp
from jax import lax
from jax.experimental import pallas as pl
from jax.experimental.pallas import tpu as pltpu

F32 = jnp.float32
BF16 = jnp.bfloat16

EPS = 1e-6
LOG2E = 1.4426950408889634
ROPE_THETA = 10000.0
GRID_W = 64
LANE = 128
HEAD_DIM = 128
A_HEADS = 8
A_KV_HEADS = 2
B_HEADS = 4
B_QK_DIM = 128
B_V_DIM = 256
B_CHUNK = 128
C_HEADS = 8
C_Q_RANK = 512
C_KV_RANK = 256
C_NOPE_DIM = 128
C_ROPE_DIM = 64
C_V_DIM = 128
C_QK_PAD = 256

OFF_AQ = 0
OFF_AK = OFF_AQ + A_HEADS * HEAD_DIM
OFF_AV = OFF_AK + A_KV_HEADS * HEAD_DIM
OFF_BQ = OFF_AV + A_KV_HEADS * HEAD_DIM
OFF_BK = OFF_BQ + B_HEADS * B_QK_DIM
OFF_BV = OFF_BK + B_HEADS * B_QK_DIM
OFF_BG = OFF_BV + B_HEADS * B_V_DIM
OFF_CQ = OFF_BG + B_HEADS * B_V_DIM
OFF_CKV = OFF_CQ + C_Q_RANK
OFF_CKR = OFF_CKV + C_KV_RANK
IN_MAIN = OFF_CKR + C_ROPE_DIM
MAIN_W = 5632
MOD_ROWS = 8

VMEM_LIMIT = 56 * 1024 * 1024


def _params(*sem):
    return pltpu.CompilerParams(dimension_semantics=sem, vmem_limit_bytes=VMEM_LIMIT)


def _rms(x, g):
    return x * lax.rsqrt(jnp.mean(x * x, axis=-1, keepdims=True) + EPS) * g


def _unroll(trips, want):
    while trips % want:
        want //= 2
    return want


def _rope(x, cos, sin_lo, sin_hi, shift):
    return (x * cos + pltpu.roll(x, LANE - shift, 1) * sin_lo + pltpu.roll(x, shift, 1) * sin_hi)


def _mod_kernel(c_ref, w_ref, b_ref, o_ref):
    c = c_ref[...]
    a = (c * jax.nn.sigmoid(c)).astype(BF16)
    o_ref[...] = jnp.dot(a, w_ref[...].astype(BF16), preferred_element_type=F32) + b_ref[...]


def _modulation(cvec, w_mod, b_mod):
    depth, d, n = w_mod.shape
    rows = cvec.shape[0]
    tn = min(1024, n)
    return pl.pallas_call(
        _mod_kernel,
        grid=(depth, n // tn),
        in_specs=[
            pl.BlockSpec((rows, d), lambda l, j: (0, 0)),
            pl.BlockSpec((None, d, tn), lambda l, j: (l, 0, j)),
            pl.BlockSpec((None, 1, tn), lambda l, j: (l, 0, j)),
        ],
        out_specs=pl.BlockSpec((None, rows, tn), lambda l, j: (l, 0, j)),
        out_shape=jax.ShapeDtypeStruct((depth, rows, n), F32),
        compiler_params=_params("arbitrary", "arbitrary"),
        name="modulation",
    )(cvec, w_mod, b_mod.reshape(depth, 1, n))


def _prenorm_kernel(x_ref, g_ref, mod_ref, h_ref):
    y = _rms(x_ref[...], g_ref[...])
    h_ref[...] = (y * (1.0 + mod_ref[1:2, :]) + mod_ref[0:1, :]).astype(BF16)


def _prenorm(x, g, mod, layer, grp):
    m, d = x.shape
    tm = min(512, m)
    return pl.pallas_call(
        _prenorm_kernel,
        grid=(m // tm,),
        in_specs=[
            pl.BlockSpec((tm, d), lambda i: (i, 0)),
            pl.BlockSpec((None, 1, d), lambda i: (layer, 0, 0)),
            pl.BlockSpec((None, None, MOD_ROWS, d), lambda i: (layer, grp(i * tm), 0, 0)),
        ],
        out_specs=pl.BlockSpec((tm, d), lambda i: (i, 0)),
        out_shape=jax.ShapeDtypeStruct((m, d), BF16),
        compiler_params=_params("arbitrary"),
        name="prenorm",
    )(x, g, mod)


def _mm_kernel(a_ref, w_ref, o_ref, *, relu2):
    y = jnp.dot(a_ref[...], w_ref[...], preferred_element_type=F32)
    if relu2:
        y = jnp.square(jnp.maximum(y, 0.0))
    o_ref[...] = y.astype(o_ref.dtype)


def _matmul(a, w, layer, *, tm, tn, relu2=False, name):
    m, k = a.shape
    n = w.shape[-1]
    tm, tn = min(tm, m), min(tn, n)
    return pl.pallas_call(
        functools.partial(_mm_kernel, relu2=relu2),
        grid=(m // tm, n // tn),
        in_specs=[
            pl.BlockSpec((tm, k), lambda i, j: (i, 0)),
            pl.BlockSpec((None, k, tn), lambda i, j: (layer, 0, j)),
        ],
        out_specs=pl.BlockSpec((tm, tn), lambda i, j: (i, j)),
        out_shape=jax.ShapeDtypeStruct((m, n), BF16),
        compiler_params=_params("arbitrary", "arbitrary"),
        name=name,
    )(a, w)


def _mm_row_kernel(a_ref, w_ref, x_ref, gpost_ref, mod_ref, gnext_ref, modn_ref, *rest,
                   gate_row, next_rows, nk, nt):
    if next_rows is None:
        xo_ref, acc0_ref, acc1_ref = rest
    else:
        xo_ref, ho_ref, acc0_ref, acc1_ref = rest
    i = pl.program_id(0)
    k = pl.program_id(1)
    tm, d = acc0_ref.shape
    sub = tm // nk

    @pl.when((i == 0) & (k == 0))
    def _():
        acc0_ref[...] = jnp.zeros_like(acc0_ref)
        acc1_ref[...] = jnp.zeros_like(acc1_ref)

    def epilogue(acc_ref):
        rows = pl.ds(pl.multiple_of(k * sub, sub), sub)
        f = acc_ref[rows, :]
        acc_ref[rows, :] = jnp.zeros((sub, d), F32)
        xn = x_ref[rows, :] + mod_ref[gate_row:gate_row + 1, :] * _rms(f, gpost_ref[...])
        xo_ref[rows, :] = xn
        if next_rows is not None:
            shift_row, scale_row = next_rows
            y = _rms(xn, gnext_ref[...])
            ho_ref[rows, :] = (y * (1.0 + modn_ref[scale_row:scale_row + 1, :])
                               + modn_ref[shift_row:shift_row + 1, :]).astype(BF16)

    for parity, (acc_mm, acc_ep) in enumerate(((acc0_ref, acc1_ref), (acc1_ref, acc0_ref))):
        @pl.when((i < nt) & (i % 2 == parity))
        def _():
            epilogue(acc_ep)
            acc_mm[...] += jnp.dot(a_ref[...], w_ref[...], preferred_element_type=F32)

    @pl.when(i == nt)
    def _():
        epilogue(acc0_ref if (nt - 1) % 2 == 0 else acc1_ref)


def _matmul_row(a, w, x, gpost, gnext, mod, layer, next_layer, grp, *, gate_row, next_rows, tm, tk, name):
    m, kdim = a.shape
    d = w.shape[-1]
    tm, tk = min(tm, m), min(tk, kdim)
    nk, nt = kdim // tk, m // tm
    assert tm % nk == 0 and (tm // nk) % 16 == 0
    cur = lambda i: jnp.minimum(i, nt - 1)
    kcur = lambda i, k: jnp.where(i < nt, k, nk - 1)
    prev = lambda i: jnp.maximum(i - 1, 0)
    vec = pl.BlockSpec((None, 1, d), lambda i, k: (layer, 0, 0))
    vecn = pl.BlockSpec((None, 1, d), lambda i, k: (next_layer, 0, 0))
    out_specs = [pl.BlockSpec((tm, d), lambda i, k: (prev(i), 0))]
    out_shape = [jax.ShapeDtypeStruct((m, d), F32)]
    if next_rows is not None:
        out_specs.append(pl.BlockSpec((tm, d), lambda i, k: (prev(i), 0)))
        out_shape.append(jax.ShapeDtypeStruct((m, d), BF16))
    return pl.pallas_call(
        functools.partial(_mm_row_kernel, gate_row=gate_row, next_rows=next_rows, nk=nk, nt=nt),
        grid=(nt + 1, nk),
        in_specs=[
            pl.BlockSpec((tm, tk), lambda i, k: (cur(i), kcur(i, k))),
            pl.BlockSpec((None, tk, d), lambda i, k: (layer, kcur(i, k), 0)),
            pl.BlockSpec((tm, d), lambda i, k: (prev(i), 0)),
            vec,
            pl.BlockSpec((None, None, MOD_ROWS, d), lambda i, k: (layer, grp(prev(i) * tm), 0, 0)),
            vecn,
            pl.BlockSpec((None, None, MOD_ROWS, d), lambda i, k: (next_layer, grp(prev(i) * tm), 0, 0)),
        ],
        out_specs=out_specs,
        out_shape=out_shape,
        scratch_shapes=[pltpu.VMEM((tm, d), F32), pltpu.VMEM((tm, d), F32)],
        compiler_params=_params("arbitrary", "arbitrary"),
        name=name,
    )(a, w, x, gpost, mod, gnext, mod)


def _prep_a_kernel(p_ref, gq_ref, gk_ref, *rest, rope, cache):
    rest = list(rest)
    tabs = [rest.pop(0) for _ in range(3)] if rope else None
    q_ref, k_ref = rest[:2]
    scale = HEAD_DIM ** -0.5 * LOG2E
    for h in range(A_HEADS):
        y = _rms(p_ref[:, h * HEAD_DIM:(h + 1) * HEAD_DIM].astype(F32), gq_ref[...])
        if rope:
            y = _rope(y, tabs[0][...], tabs[1][...], tabs[2][...], HEAD_DIM // 4)
        q_ref[:, h * HEAD_DIM:(h + 1) * HEAD_DIM] = (y * scale).astype(BF16)
    for g in range(A_KV_HEADS):
        lo = OFF_AK + g * HEAD_DIM
        y = _rms(p_ref[:, lo:lo + HEAD_DIM].astype(F32), gk_ref[...])
        if cache:
            rest[2][:, g * HEAD_DIM:(g + 1) * HEAD_DIM] = y
        if rope:
            y = _rope(y, tabs[0][...], tabs[1][...], tabs[2][...], HEAD_DIM // 4)
        k_ref[:, g * HEAD_DIM:(g + 1) * HEAD_DIM] = y.astype(BF16)
    if cache:
        rest[3][...] = p_ref[:, OFF_AV:OFF_BQ].astype(F32)


def _prep_a(proj, gq, gk, layer, tabs, n_per, *, cache):
    m = proj.shape[0]
    tr = min(512, n_per)
    rope = tabs is not None
    width = OFF_BQ
    kvw = A_KV_HEADS * HEAD_DIM
    in_specs = [
        pl.BlockSpec((tr, width), lambda i: (i, 0)),
        pl.BlockSpec((None, 1, HEAD_DIM), lambda i: (layer, 0, 0)),
        pl.BlockSpec((None, 1, HEAD_DIM), lambda i: (layer, 0, 0)),
    ]
    args = [proj, gq, gk]
    if rope:
        nt = n_per // tr
        in_specs += [pl.BlockSpec((tr, LANE), lambda i: (i % nt, 0))] * 3
        args += list(tabs)
    out_specs = [pl.BlockSpec((tr, A_HEADS * HEAD_DIM), lambda i: (i, 0)),
                 pl.BlockSpec((tr, kvw), lambda i: (i, 0))]
    out_shape = [jax.ShapeDtypeStruct((m, A_HEADS * HEAD_DIM), BF16),
                 jax.ShapeDtypeStruct((m, kvw), BF16)]
    if cache:
        out_specs += [pl.BlockSpec((tr, kvw), lambda i: (i, 0))] * 2
        out_shape += [jax.ShapeDtypeStruct((m, kvw), F32)] * 2
    return pl.pallas_call(
        functools.partial(_prep_a_kernel, rope=rope, cache=cache),
        grid=(m // tr,),
        in_specs=in_specs,
        out_specs=out_specs,
        out_shape=out_shape,
        compiler_params=_params("arbitrary"),
        name="prep_a",
    )(*args)


def _prep_cq_kernel(p_ref, g_ref, w_ref, *rest, rope):
    rest = list(rest)
    tabs = [rest.pop(0) for _ in range(3)] if rope else None
    q_ref = rest[0]
    scale = (C_NOPE_DIM + C_ROPE_DIM) ** -0.5 * LOG2E
    y = _rms(p_ref[...].astype(F32), g_ref[...]).astype(BF16)
    z = jnp.dot(y, w_ref[...], preferred_element_type=F32)
    for h in range(C_HEADS):
        lo = h * C_QK_PAD
        q_ref[:, lo:lo + C_NOPE_DIM] = (z[:, lo:lo + C_NOPE_DIM] * scale).astype(BF16)
        r = z[:, lo + C_NOPE_DIM:lo + C_QK_PAD]
        if rope:
            r = _rope(r, tabs[0][...], tabs[1][...], tabs[2][...], C_ROPE_DIM // 4)
        q_ref[:, lo + C_NOPE_DIM:lo + C_QK_PAD] = (r * scale).astype(BF16)


def _prep_cq(proj, g, w_uq, layer, tabs, n_per):
    m = proj.shape[0]
    tr = min(512, n_per)
    rope = tabs is not None
    width = C_HEADS * C_QK_PAD
    in_specs = [
        pl.BlockSpec((tr, C_Q_RANK), lambda i: (i, OFF_CQ // C_Q_RANK)),
        pl.BlockSpec((None, 1, C_Q_RANK), lambda i: (layer, 0, 0)),
        pl.BlockSpec((None, C_Q_RANK, width), lambda i: (layer, 0, 0)),
    ]
    args = [proj, g, w_uq]
    if rope:
        nt = n_per // tr
        in_specs += [pl.BlockSpec((tr, LANE), lambda i: (i % nt, 0))] * 3
        args += list(tabs)
    return pl.pallas_call(
        functools.partial(_prep_cq_kernel, rope=rope),
        grid=(m // tr,),
        in_specs=in_specs,
        out_specs=pl.BlockSpec((tr, width), lambda i: (i, 0)),
        out_shape=jax.ShapeDtypeStruct((m, width), BF16),
        compiler_params=_params("arbitrary"),
        name="prep_cq",
    )(*args)


def _prep_ckv_kernel(ckv_ref, kr_ref, g_ref, w_ref, *rest, norm, rope, cache):
    rest = list(rest)
    tabs = [rest.pop(0) for _ in range(3)] if rope else None
    kc_ref, vc_ref = rest[:2]
    x = ckv_ref[...].astype(F32)
    if norm:
        x = _rms(x, g_ref[...])
    kr = kr_ref[...].astype(F32)
    if cache:
        rest[2][...] = x
        rest[3][...] = kr[:, :C_ROPE_DIM]
    if rope:
        kr = _rope(kr, tabs[0][...], tabs[1][...], tabs[2][...], C_ROPE_DIM // 4)
    krb = kr.astype(BF16)
    z = jnp.dot(x.astype(BF16), w_ref[...], preferred_element_type=F32)
    up = C_NOPE_DIM + C_V_DIM
    for h in range(C_HEADS):
        kc_ref[:, h * C_QK_PAD:h * C_QK_PAD + C_NOPE_DIM] = z[:, h * up:h * up + C_NOPE_DIM].astype(BF16)
        kc_ref[:, h * C_QK_PAD + C_NOPE_DIM:(h + 1) * C_QK_PAD] = krb
        vc_ref[:, h * C_V_DIM:(h + 1) * C_V_DIM] = z[:, h * up + C_NOPE_DIM:(h + 1) * up].astype(BF16)


def _prep_ckv(ckv_src, ckv_blk, kr_src, kr_blk, g, w_ukv, layer, tabs, n_per, *, norm, cache):
    m = ckv_src.shape[0]
    tr = min(512, n_per)
    rope = tabs is not None
    in_specs = [
        pl.BlockSpec((tr, C_KV_RANK), lambda i: (i, ckv_blk)),
        pl.BlockSpec((tr, LANE), lambda i: (i, kr_blk)),
        pl.BlockSpec((None, 1, C_KV_RANK), lambda i: (layer, 0, 0)),
        pl.BlockSpec((None, C_KV_RANK, C_HEADS * (C_NOPE_DIM + C_V_DIM)), lambda i: (layer, 0, 0)),
    ]
    args = [ckv_src, kr_src, g, w_ukv]
    if rope:
        nt = n_per // tr
        in_specs += [pl.BlockSpec((tr, LANE), lambda i: (i % nt, 0))] * 3
        args += list(tabs)
    out_specs = [pl.BlockSpec((tr, C_HEADS * C_QK_PAD), lambda i: (i, 0)),
                 pl.BlockSpec((tr, C_HEADS * C_V_DIM), lambda i: (i, 0))]
    out_shape = [jax.ShapeDtypeStruct((m, C_HEADS * C_QK_PAD), BF16),
                 jax.ShapeDtypeStruct((m, C_HEADS * C_V_DIM), BF16)]
    if cache:
        out_specs += [pl.BlockSpec((tr, C_KV_RANK), lambda i: (i, 0)),
                      pl.BlockSpec((tr, C_ROPE_DIM), lambda i: (i, 0))]
        out_shape += [jax.ShapeDtypeStruct((m, C_KV_RANK), F32),
                      jax.ShapeDtypeStruct((m, C_ROPE_DIM), F32)]
    return pl.pallas_call(
        functools.partial(_prep_ckv_kernel, norm=norm, rope=rope, cache=cache),
        grid=(m // tr,),
        in_specs=in_specs,
        out_specs=out_specs,
        out_shape=out_shape,
        compiler_params=_params("arbitrary"),
        name="prep_ckv",
    )(*args)


def _attn_kernel(q_ref, k_ref, v_ref, *rest, gps, rep, dqk, dv, tq, tk, ctx):
    if ctx:
        k2_ref, v2_ref, o_ref, acc_ref, s0_ref, s1_ref = rest
    else:
        o_ref, acc_ref, s0_ref, s1_ref = rest
    s_refs = (s0_ref, s1_ref)

    items = []
    for g in range(gps):
        chunks = [(k_ref, v_ref, c * tk, tk) for c in range(k_ref.shape[0] // tk)]
        if ctx:
            nctx = k2_ref.shape[0]
            chunks += [(k2_ref, v2_ref, lo, min(tk, nctx - lo)) for lo in range(0, nctx, tk)]
        items += [(g, c == 0, c == len(chunks) - 1) + ch for c, ch in enumerate(chunks)]

    def queries(g):
        cols = [q_ref[:, (g * rep + r) * dqk:(g * rep + r + 1) * dqk] for r in range(rep)]
        return cols[0] if rep == 1 else jnp.concatenate(cols, axis=0)

    def scores(t):
        g, _, _, kr, _, lo, size = items[t]
        s_refs[t % 2][:, :size] = lax.dot_general(
            queries(g), kr[lo:lo + size, g * dqk:(g + 1) * dqk], (((1,), (1,)), ((), ())),
            preferred_element_type=F32)

    scores(0)
    m = None
    for t, (g, first, last, _, vr, lo, size) in enumerate(items):
        if t + 1 < len(items):
            scores(t + 1)
        s = s_refs[t % 2][:, :size]
        m_blk = jnp.max(s, axis=-1, keepdims=True)
        m_new = m_blk if first else jnp.maximum(m, m_blk)
        p = jnp.exp2(s - m_new).astype(BF16)
        v_ext = jnp.concatenate([vr[lo:lo + size, g * dv:(g + 1) * dv], jnp.ones((size, dv), BF16)], axis=1)
        pv = jnp.dot(p, v_ext, preferred_element_type=F32)
        if first:
            acc_ref[g] = pv
        else:
            acc_ref[g] = jnp.exp2(m - m_new) * acc_ref[g] + pv
        m = m_new
        if last:
            o = acc_ref[g, :, :dv] / acc_ref[g, :, dv:]
            for r in range(rep):
                o_ref[:, (g * rep + r) * dv:(g * rep + r + 1) * dv] = o[r * tq:(r + 1) * tq].astype(o_ref.dtype)


def _attention(q, k, kblk0, v, vblk0, ctx_kv, batch, n, *, groups, gps, rep, dqk, dv, rows, name):
    tq = min(rows // rep, n)
    tk = min(512, n)
    nq = n // tq
    assert groups % gps == 0 and kblk0 % gps == 0 and vblk0 % gps == 0 and (ctx_kv is None or gps == 1)
    in_specs = [
        pl.BlockSpec((tq, gps * rep * dqk), lambda b, g, i: (b * nq + i, g)),
        pl.BlockSpec((n, gps * dqk), lambda b, g, i: (b, kblk0 // gps + g)),
        pl.BlockSpec((n, gps * dv), lambda b, g, i: (b, vblk0 // gps + g)),
    ]
    args = [q, k, v]
    if ctx_kv is not None:
        k2, v2 = ctx_kv
        nc = k2.shape[0] // batch
        in_specs += [pl.BlockSpec((nc, dqk), lambda b, g, i: (b, g)),
                     pl.BlockSpec((nc, dv), lambda b, g, i: (b, g))]
        args += [k2, v2]
    return pl.pallas_call(
        functools.partial(_attn_kernel, gps=gps, rep=rep, dqk=dqk, dv=dv, tq=tq, tk=tk, ctx=ctx_kv is not None),
        grid=(batch, groups // gps, nq),
        in_specs=in_specs,
        out_specs=pl.BlockSpec((tq, gps * rep * dv), lambda b, g, i: (b * nq + i, g)),
        out_shape=jax.ShapeDtypeStruct((batch * n, groups * rep * dv), BF16),
        scratch_shapes=[pltpu.VMEM((gps, rep * tq, 2 * dv), F32),
                        pltpu.VMEM((rep * tq, tk), F32),
                        pltpu.VMEM((rep * tq, tk), F32)],
        compiler_params=_params("arbitrary", "arbitrary", "arbitrary"),
        name=name,
    )(*args)


def _ret_head(lgf, lgb, io, tabs, nc, kvf_ref, kvb_ref, state_ref):
    c_len = B_CHUNK
    rope = tabs is not None
    ii = lax.broadcasted_iota(jnp.int32, (c_len, c_len), 0)
    jj = lax.broadcasted_iota(jnp.int32, (c_len, c_len), 1)
    diff = (ii - jj).astype(F32)
    dmat = (jnp.where(diff >= 0, jnp.exp(jnp.maximum(diff, 0.0) * lgf), 0.0)
            + jnp.where(diff <= 0, jnp.exp(jnp.maximum(-diff, 0.0) * lgb), 0.0))
    ri = lax.broadcasted_iota(jnp.int32, (c_len, 1), 0).astype(F32)
    qdec_f = jnp.exp((ri + 1.0) * lgf)
    qdec_b = jnp.exp((c_len - ri) * lgb)
    kdec_f = jnp.exp((c_len - 1.0 - ri) * lgf)
    kdec_b = jnp.exp(ri * lgb)
    one = jnp.ones((1, 1), F32)
    cdec_f = jnp.exp(one * (c_len * lgf))
    cdec_b = jnp.exp(one * (c_len * lgb))
    kscale = B_QK_DIM ** -0.5

    def roped(load, lo):
        x = load(lo).astype(F32)
        if rope:
            t = [tab[pl.ds(lo, c_len), :] for tab in tabs]
            x = _rope(x, t[0], t[1], t[2], B_QK_DIM // 4)
        return x

    def increments(c, carry):
        lo = pl.multiple_of(c * c_len, c_len)
        kc = roped(io["k"], lo) * kscale
        kk = jnp.concatenate([kc * kdec_f, kc * kdec_b], axis=1).T.astype(BF16)
        kv = jnp.dot(kk, io["v"](lo), preferred_element_type=F32)
        kvf_ref[c] = kv[:B_QK_DIM]
        kvb_ref[c] = kv[B_QK_DIM:]
        return carry

    lax.fori_loop(0, nc, increments, 0, unroll=_unroll(nc, 8))

    def scan_f(c, s):
        state_ref[c, :B_QK_DIM, :] = s.astype(BF16)
        return cdec_f * s + kvf_ref[c]

    def scan_b(t, s):
        c = nc - 1 - t
        state_ref[c, B_QK_DIM:, :] = s.astype(BF16)
        return cdec_b * s + kvb_ref[c]

    s_f = lax.fori_loop(0, nc, scan_f, io["s0f"], unroll=_unroll(nc, 2))
    s_b = lax.fori_loop(0, nc, scan_b, io["s0b"], unroll=_unroll(nc, 2))
    io["store_state"](s_f, s_b)

    def outputs(c, carry):
        lo = pl.multiple_of(c * c_len, c_len)
        qc = roped(io["q"], lo)
        kc = (roped(io["k"], lo) * kscale).astype(BF16)
        s = lax.dot_general(qc.astype(BF16), kc, (((1,), (1,)), ((), ())), preferred_element_type=F32) * dmat
        o = jnp.dot(s.astype(BF16), io["v"](lo), preferred_element_type=F32)
        qq = jnp.concatenate([qc * qdec_f, qc * qdec_b], axis=1).astype(BF16)
        o = o + jnp.dot(qq, state_ref[c], preferred_element_type=F32)
        gate = io["gate"](lo).astype(F32)
        io["store_o"](lo, (gate * jax.nn.sigmoid(gate) * _rms(o, io["gn"])).astype(BF16))
        return carry

    lax.fori_loop(0, nc, outputs, 0, unroll=_unroll(nc, 4))


def _ret_kernel(lg_ref, *refs, n, rope, emit_state, all_heads):
    refs = list(refs)
    data = [refs.pop(0) for _ in range(2 if all_heads else 4)]
    gn_ref, s0f_ref, s0b_ref = refs.pop(0), refs.pop(0), refs.pop(0)
    tabs = [refs.pop(0) for _ in range(3)] if rope else None
    o_ref = refs.pop(0)
    sfo_ref, sbo_ref = (refs.pop(0), refs.pop(0)) if emit_state else (None, None)
    kvf_ref, kvb_ref, state_ref = refs
    nc = n // B_CHUNK
    rows = lambda lo: pl.ds(lo, B_CHUNK)
    qw, vw = B_HEADS * B_QK_DIM, B_V_DIM

    for hh in range(B_HEADS if all_heads else 1):
        if all_heads:
            h = hh
            blk1, blk2 = data
            v_src = (blk1, 2 * qw + hh * vw) if hh < 2 else (blk2, (hh - 2) * vw)
            col = lambda ref, c0, w: (lambda lo: ref[rows(lo), c0:c0 + w])
            io = dict(
                q=col(blk1, hh * B_QK_DIM, B_QK_DIM), k=col(blk1, qw + hh * B_QK_DIM, B_QK_DIM),
                v=col(v_src[0], v_src[1], vw), gate=col(blk2, 2 * vw + hh * vw, vw),
                gn=gn_ref[:, hh * vw:(hh + 1) * vw], s0f=s0f_ref[hh], s0b=s0b_ref[hh])

            def store_o(lo, val, hh=hh):
                o_ref[rows(lo), hh * vw:(hh + 1) * vw] = val

            def store_state(sf, sb, hh=hh):
                if emit_state:
                    sfo_ref[hh] = sf
                    sbo_ref[hh] = sb
        else:
            h = pl.program_id(1)
            whole = lambda ref: (lambda lo: ref[rows(lo), :])
            io = dict(q=whole(data[0]), k=whole(data[1]), v=whole(data[2]), gate=whole(data[3]),
                      gn=gn_ref[...], s0f=s0f_ref[...], s0b=s0b_ref[...])

            def store_o(lo, val):
                o_ref[rows(lo), :] = val

            def store_state(sf, sb):
                if emit_state:
                    sfo_ref[...] = sf
                    sbo_ref[...] = sb
        io.update(store_o=store_o, store_state=store_state)
        _ret_head(lg_ref[0, h], lg_ref[1, h], io, tabs, nc, kvf_ref.at[hh], kvb_ref.at[hh], state_ref.at[hh])


def _retention(proj, lg, gn, s0f, s0b, layer, tabs, batch, n, *, emit_state, all_heads):
    rope = tabs is not None
    if all_heads:
        wide = (OFF_CQ - OFF_BQ) // 2
        assert OFF_BQ % wide == 0 and wide == 2 * B_HEADS * B_QK_DIM + 2 * B_V_DIM
        st_spec = pl.BlockSpec((None, B_HEADS, B_QK_DIM, B_V_DIM), lambda b, h, lg_: (b, 0, 0, 0))
        in_specs = [
            pl.BlockSpec((n, wide), lambda b, h, lg_: (b, OFF_BQ // wide)),
            pl.BlockSpec((n, wide), lambda b, h, lg_: (b, OFF_BQ // wide + 1)),
            pl.BlockSpec((None, 1, B_HEADS * B_V_DIM), lambda b, h, lg_: (layer, 0, 0)),
        ]
        args = [proj, proj, gn]
        out_specs = [pl.BlockSpec((n, B_HEADS * B_V_DIM), lambda b, h, lg_: (b, 0))]
        heads_per_step = B_HEADS
    else:
        qb, kb = OFF_BQ // B_QK_DIM, OFF_BK // B_QK_DIM
        vb, gb = OFF_BV // B_V_DIM, OFF_BG // B_V_DIM
        st_spec = pl.BlockSpec((None, None, B_QK_DIM, B_V_DIM), lambda b, h, lg_: (b, h, 0, 0))
        in_specs = [
            pl.BlockSpec((n, B_QK_DIM), lambda b, h, lg_: (b, qb + h)),
            pl.BlockSpec((n, B_QK_DIM), lambda b, h, lg_: (b, kb + h)),
            pl.BlockSpec((n, B_V_DIM), lambda b, h, lg_: (b, vb + h)),
            pl.BlockSpec((n, B_V_DIM), lambda b, h, lg_: (b, gb + h)),
            pl.BlockSpec((None, 1, B_V_DIM), lambda b, h, lg_: (layer, 0, h)),
        ]
        args = [proj, proj, proj, proj, gn]
        out_specs = [pl.BlockSpec((n, B_V_DIM), lambda b, h, lg_: (b, h))]
        heads_per_step = 1
    in_specs += [st_spec, st_spec]
    args += [s0f, s0b]
    if rope:
        in_specs += [pl.BlockSpec((n, LANE), lambda b, h, lg_: (0, 0))] * 3
        args += list(tabs)
    out_shape = [jax.ShapeDtypeStruct((batch * n, B_HEADS * B_V_DIM), BF16)]
    if emit_state:
        out_specs += [st_spec, st_spec]
        out_shape += [jax.ShapeDtypeStruct((batch, B_HEADS, B_QK_DIM, B_V_DIM), F32)] * 2
    nc = n // B_CHUNK
    return pl.pallas_call(
        functools.partial(_ret_kernel, n=n, rope=rope, emit_state=emit_state, all_heads=all_heads),
        grid_spec=pltpu.PrefetchScalarGridSpec(
            num_scalar_prefetch=1,
            grid=(batch, B_HEADS // heads_per_step),
            in_specs=in_specs,
            out_specs=out_specs,
            scratch_shapes=[pltpu.VMEM((heads_per_step, nc, B_QK_DIM, B_V_DIM), F32),
                            pltpu.VMEM((heads_per_step, nc, B_QK_DIM, B_V_DIM), F32),
                            pltpu.VMEM((heads_per_step, nc, 2 * B_QK_DIM, B_V_DIM), BF16)],
        ),
        out_shape=out_shape,
        compiler_params=_params("arbitrary", "arbitrary"),
        name="retention",
    )(lg, *args)


def _merge_kernel(oa_ref, ob_ref, oc_ref, wa_ref, wb_ref, wc_ref, ga_ref, gb_ref, gc_ref, o_ref):
    def branch(o, w, g):
        return jax.nn.sigmoid(g[...].astype(F32)) * jnp.dot(o[...], w[...], preferred_element_type=F32)

    o_ref[...] = (branch(oa_ref, wa_ref, ga_ref) + branch(ob_ref, wb_ref, gb_ref)
                  + branch(oc_ref, wc_ref, gc_ref)).astype(BF16)


def _merge(oa, ob, oc, wa, wb, wc, gates, layer, d):
    m = oa.shape[0]
    tm = min(1024, m)
    tn = min(512, d)
    nb = d // tn
    o_spec = lambda o: pl.BlockSpec((tm, o.shape[1]), lambda i, j: (i, 0))
    w_spec = lambda w: pl.BlockSpec((None, w.shape[1], tn), lambda i, j: (layer, 0, j))
    g_spec = lambda br: pl.BlockSpec((tm, tn), lambda i, j: (i, br * nb + j))
    return pl.pallas_call(
        _merge_kernel,
        grid=(m // tm, nb),
        in_specs=[o_spec(oa), o_spec(ob), o_spec(oc), w_spec(wa), w_spec(wb), w_spec(wc),
                  g_spec(0), g_spec(1), g_spec(2)],
        out_specs=pl.BlockSpec((tm, tn), lambda i, j: (i, j)),
        out_shape=jax.ShapeDtypeStruct((m, d), BF16),
        compiler_params=_params("arbitrary", "arbitrary"),
        name="merge",
    )(oa, ob, oc, wa, wb, wc, gates, gates, gates)


def _rope_tables(n_tokens, dim, pad):
    pos = jnp.arange(n_tokens, dtype=jnp.int32)
    row = (pos // GRID_W).astype(F32)[:, None]
    col = (pos % GRID_W).astype(F32)[:, None]
    quarter = dim // 4
    inv = ROPE_THETA ** (-jnp.arange(quarter, dtype=F32) / quarter)[None, :]
    ang = jnp.concatenate([row * inv, row * inv, col * inv, col * inv], axis=1)
    cos, sin = jnp.cos(ang), jnp.sin(ang)
    first = (jnp.arange(dim) // quarter) % 2 == 0
    sin_lo = jnp.where(first[None, :], -sin, 0.0)
    sin_hi = jnp.where(first[None, :], 0.0, sin)
    if pad:
        z = jnp.zeros_like(cos)
        tabs = [jnp.concatenate([t, z], axis=1) for t in (cos, sin_lo, sin_hi)]
    else:
        tabs = [jnp.tile(t, (1, LANE // dim)) for t in (cos, sin_lo, sin_hi)]
    return tuple(tabs)


def kernel(x_prompt, x_sample, c, cache_attn_k, cache_attn_v, state_ret_fwd, state_ret_bwd, cache_mla_ckv, cache_mla_krope, c_ctx, w_mod, b_mod, g_pre_mix, g_post_mix, g_pre_mlp, g_post_mlp, w_in, attn_q_norm, attn_k_norm, ret_decay_fwd, ret_decay_bwd, ret_gn, mla_q_norm, mla_kv_norm, w_mla_uq, w_mla_ukv, w_branch_a, w_branch_b, w_branch_c, w_out, w_mlp_up, w_mlp_down):
    bp, n_p, d = x_prompt.shape
    bs, n_s, _ = x_sample.shape
    depth = w_in.shape[0]
    past = cache_attn_k.shape[2]
    assert n_p % B_CHUNK == 0 and n_s % B_CHUNK == 0

    w_main = w_in[..., :MAIN_W].astype(BF16)
    w_gate = w_in[..., IN_MAIN:].astype(BF16)
    w_uq = jnp.pad(w_mla_uq.reshape(depth, C_Q_RANK, C_HEADS, C_NOPE_DIM + C_ROPE_DIM),
                   ((0, 0), (0, 0), (0, 0), (0, C_QK_PAD - C_NOPE_DIM - C_ROPE_DIM)))
    w_uq = w_uq.reshape(depth, C_Q_RANK, C_HEADS * C_QK_PAD).astype(BF16)
    w_ukv = w_mla_ukv.astype(BF16)
    wa, wb, wc = w_branch_a.astype(BF16), w_branch_b.astype(BF16), w_branch_c.astype(BF16)
    w_o, w_up, w_dn = w_out.astype(BF16), w_mlp_up.astype(BF16), w_mlp_down.astype(BF16)
    vec = lambda g: g.reshape(depth, 1, g.shape[-1])
    g_pre_mix, g_post_mix, g_pre_mlp, g_post_mlp = map(vec, (g_pre_mix, g_post_mix, g_pre_mlp, g_post_mlp))
    gqn, gkn, gcq, gckv, gret = map(vec, (attn_q_norm, attn_k_norm, mla_q_norm, mla_kv_norm, ret_gn))
    lg = jnp.stack([jax.nn.log_sigmoid(ret_decay_fwd.astype(F32)),
                    jax.nn.log_sigmoid(ret_decay_bwd.astype(F32))], axis=1)

    rows = -(-(1 + bs) // 8) * 8
    cvec = jnp.zeros((rows, d), F32).at[0].set(c_ctx).at[1:1 + bs].set(c)
    mod = _modulation(cvec, w_mod, b_mod).reshape(depth, rows, 6, d)
    mod = jnp.pad(mod, ((0, 0), (0, 0), (0, MOD_ROWS - 6), (0, 0)))

    tabs128 = _rope_tables(n_s, HEAD_DIM, pad=False)
    tabs64 = _rope_tables(n_s, C_ROPE_DIM, pad=False)
    tabs64p = _rope_tables(n_s, C_ROPE_DIM, pad=True)

    ctx_k = cache_attn_k.astype(BF16).reshape(bs, depth, past, A_KV_HEADS * HEAD_DIM)
    ctx_v = cache_attn_v.astype(BF16).reshape(bs, depth, past, A_KV_HEADS * HEAD_DIM)
    ctx_kr = jnp.pad(cache_mla_krope, ((0, 0), (0, 0), (0, 0), (0, LANE - C_ROPE_DIM)))
    zero_state = jnp.zeros((bp, B_HEADS, B_QK_DIM, B_V_DIM), F32)

    streams = {
        "p": dict(x=x_prompt.reshape(bp * n_p, d), batch=bp, n=n_p, grp=lambda r: 0, latent=False),
        "s": dict(x=x_sample.reshape(bs * n_s, d), batch=bs, n=n_s, grp=lambda r: 1 + r // n_s, latent=True),
    }
    for st in streams.values():
        st["h"] = _prenorm(st["x"], g_pre_mix, mod, 0, st["grp"])
    caches = [[] for _ in range(6)]

    for l in range(depth):
        for st in streams.values():
            batch, n, grp, latent = st["batch"], st["n"], st["grp"], st["latent"]
            proj = _matmul(st["h"], w_main, l, tm=1024, tn=512, name="w_in")
            gates = _matmul(st["h"], w_gate, l, tm=1024, tn=512, name="w_gates")
            t128, t64, t64p = (tabs128, tabs64, tabs64p) if latent else (None, None, None)

            pa = _prep_a(proj, gqn, gkn, l, t128, n, cache=not latent)
            qa, ka = pa[0], pa[1]
            ctx_a = (ctx_k[:, l].reshape(bs * past, -1), ctx_v[:, l].reshape(bs * past, -1)) if latent else None
            o_a = _attention(qa, ka, 0, proj, OFF_AV // HEAD_DIM, ctx_a, batch, n,
                             groups=A_KV_HEADS, gps=1 if latent else A_KV_HEADS, rep=A_HEADS // A_KV_HEADS,
                             dqk=HEAD_DIM, dv=HEAD_DIM, rows=1024, name="attn_a")

            if latent:
                s0f, s0b = state_ret_fwd[:, l], state_ret_bwd[:, l]
            else:
                s0f = s0b = zero_state
            rb = _retention(proj, lg[l], gret, s0f, s0b, l, t128, batch, n,
                            emit_state=not latent, all_heads=not latent)
            o_b = rb[0]

            qc = _prep_cq(proj, gcq, w_uq, l, t64p, n)
            pc = _prep_ckv(proj, OFF_CKV // C_KV_RANK, proj, OFF_CKR // LANE, gckv, w_ukv, l, t64, n,
                           norm=True, cache=not latent)
            kc, vc = pc[0], pc[1]
            ctx_c = None
            if latent:
                ctx_c = _prep_ckv(cache_mla_ckv[:, l].reshape(bs * past, C_KV_RANK), 0,
                                  ctx_kr[:, l].reshape(bs * past, LANE), 0, gckv, w_ukv, l, None, past,
                                  norm=False, cache=False)
            o_c = _attention(qc, kc, 0, vc, 0, ctx_c, batch, n,
                             groups=C_HEADS, gps=1 if latent else C_HEADS, rep=1,
                             dqk=C_QK_PAD, dv=C_V_DIM, rows=1024, name="attn_c")

            if not latent:
                for dst, val in zip(caches, (pa[2], pa[3], rb[1], rb[2], pc[2], pc[3])):
                    dst.append(val)

            merged = _merge(o_a, o_b, o_c, wa, wb, wc, gates, l, d)
            x, h2 = _matmul_row(merged, w_o, st["x"], g_post_mix, g_pre_mlp, mod, l, l, grp,
                                gate_row=2, next_rows=(3, 4), tm=512, tk=512, name="w_out")
            hid = _matmul(h2, w_up, l, tm=1024, tn=512, relu2=True, name="mlp_up")
            last = l == depth - 1
            res = _matmul_row(hid, w_dn, x, g_post_mlp, g_pre_mix, mod, l, min(l + 1, depth - 1), grp,
                              gate_row=5, next_rows=None if last else (0, 1), tm=512, tk=1024, name="mlp_down")
            st["x"] = res[0]
            st["h"] = None if last else res[1]

    y_p = streams["p"]["x"].reshape(bp, n_p, d)
    y_s = streams["s"]["x"].reshape(bs, n_s, d)
    nk, nv, sf, sb, ckv, kr = caches
    new_attn_k = jnp.stack(nk, axis=1).reshape(bp, n_p, depth, A_KV_HEADS, HEAD_DIM).transpose(0, 2, 1, 3, 4)
    new_attn_v = jnp.stack(nv, axis=1).reshape(bp, n_p, depth, A_KV_HEADS, HEAD_DIM).transpose(0, 2, 1, 3, 4)
    new_ret_fwd = jnp.stack(sf, axis=1)
    new_ret_bwd = jnp.stack(sb, axis=1)
    new_mla_ckv = jnp.stack(ckv, axis=1).reshape(bp, n_p, depth, C_KV_RANK).transpose(0, 2, 1, 3)
    new_mla_krope = jnp.stack(kr, axis=1).reshape(bp, n_p, depth, C_ROPE_DIM).transpose(0, 2, 1, 3)
    return (y_p, y_s, new_attn_k, new_attn_v, new_ret_fwd, new_ret_bwd, new_mla_ckv, new_mla_krope)
```

```python
import functools

import jax
import jax.numpy as jnp
from jax import lax
from jax.experimental import pallas as pl
from jax.experimental.pallas import tpu as pltpu

F32 = jnp.float32
BF16 = jnp.bfloat16

EPS = 1e-6
LOG2E = 1.4426950408889634
ROPE_THETA = 10000.0
GRID_W = 64
LANE = 128
HEAD_DIM = 128
A_HEADS = 8
A_KV_HEADS = 2
B_HEADS = 4
B_QK_DIM = 128
B_V_DIM = 256
B_CHUNK = 128
C_HEADS = 8
C_Q_RANK = 512
C_KV_RANK = 256
C_NOPE_DIM = 128
C_ROPE_DIM = 64
C_V_DIM = 128
C_QK_PAD = 256

OFF_AQ = 0
OFF_AK = OFF_AQ + A_HEADS * HEAD_DIM
OFF_AV = OFF_AK + A_KV_HEADS * HEAD_DIM
OFF_BQ = OFF_AV + A_KV_HEADS * HEAD_DIM
OFF_BK = OFF_BQ + B_HEADS * B_QK_DIM
OFF_BV = OFF_BK + B_HEADS * B_QK_DIM
OFF_BG = OFF_BV + B_HEADS * B_V_DIM
OFF_CQ = OFF_BG + B_HEADS * B_V_DIM
OFF_CKV = OFF_CQ + C_Q_RANK
OFF_CKR = OFF_CKV + C_KV_RANK
IN_MAIN = OFF_CKR + C_ROPE_DIM
MAIN_W = 5632
MOD_ROWS = 8

VMEM_LIMIT = 56 * 1024 * 1024


def _params(*sem):
    return pltpu.CompilerParams(dimension_semantics=sem, vmem_limit_bytes=VMEM_LIMIT)


def _rms(x, g):
    return x * lax.rsqrt(jnp.mean(x * x, axis=-1, keepdims=True) + EPS) * g


def _unroll(trips, want):
    while trips % want:
        want //= 2
    return want


def _rope(x, cos, sin_lo, sin_hi, shift):
    return (x * cos + pltpu.roll(x, LANE - shift, 1) * sin_lo + pltpu.roll(x, shift, 1) * sin_hi)


def _mod_kernel(c_ref, w_ref, b_ref, o_ref):
    c = c_ref[...]
    a = (c * jax.nn.sigmoid(c)).astype(BF16)
    o_ref[...] = jnp.dot(a, w_ref[...].astype(BF16), preferred_element_type=F32) + b_ref[...]


def _modulation(cvec, w_mod, b_mod):
    depth, d, n = w_mod.shape
    rows = cvec.shape[0]
    tn = min(1024, n)
    return pl.pallas_call(
        _mod_kernel,
        grid=(depth, n // tn),
        in_specs=[
            pl.BlockSpec((rows, d), lambda l, j: (0, 0)),
            pl.BlockSpec((None, d, tn), lambda l, j: (l, 0, j)),
            pl.BlockSpec((None, 1, tn), lambda l, j: (l, 0, j)),
        ],
        out_specs=pl.BlockSpec((None, rows, tn), lambda l, j: (l, 0, j)),
        out_shape=jax.ShapeDtypeStruct((depth, rows, n), F32),
        compiler_params=_params("arbitrary", "arbitrary"),
        name="modulation",
    )(cvec, w_mod, b_mod.reshape(depth, 1, n))


def _split_w_in_kernel(w_ref, main_ref, gate_ref):
    main_ref[...] = w_ref[:, :MAIN_W].astype(BF16)
    gate_ref[...] = w_ref[:, IN_MAIN:].astype(BF16)


def _split_w_in(w_in):
    depth, d, width = w_in.shape
    tr = min(256, d)
    gate_w = width - IN_MAIN
    return pl.pallas_call(
        _split_w_in_kernel,
        grid=(depth, d // tr),
        in_specs=[pl.BlockSpec((None, tr, width), lambda l, i: (l, i, 0))],
        out_specs=[pl.BlockSpec((None, tr, MAIN_W), lambda l, i: (l, i, 0)),
                   pl.BlockSpec((None, tr, gate_w), lambda l, i: (l, i, 0))],
        out_shape=[jax.ShapeDtypeStruct((depth, d, MAIN_W), BF16),
                   jax.ShapeDtypeStruct((depth, d, gate_w), BF16)],
        compiler_params=_params("arbitrary", "arbitrary"),
        name="split_w_in",
    )(w_in)


def _prenorm_kernel(x_ref, g_ref, mod_ref, h_ref):
    y = _rms(x_ref[...], g_ref[...])
    h_ref[...] = (y * (1.0 + mod_ref[1:2, :]) + mod_ref[0:1, :]).astype(BF16)


def _prenorm(x, g, mod, layer, grp):
    m, d = x.shape
    tm = min(512, m)
    return pl.pallas_call(
        _prenorm_kernel,
        grid=(m // tm,),
        in_specs=[
            pl.BlockSpec((tm, d), lambda i: (i, 0)),
            pl.BlockSpec((None, 1, d), lambda i: (layer, 0, 0)),
            pl.BlockSpec((None, None, MOD_ROWS, d), lambda i: (layer, grp(i * tm), 0, 0)),
        ],
        out_specs=pl.BlockSpec((tm, d), lambda i: (i, 0)),
        out_shape=jax.ShapeDtypeStruct((m, d), BF16),
        compiler_params=_params("arbitrary"),
        name="prenorm",
    )(x, g, mod)


def _mm_kernel(a_ref, w_ref, o_ref, *, relu2):
    y = jnp.dot(a_ref[...], w_ref[...], preferred_element_type=F32)
    if relu2:
        y = jnp.square(jnp.maximum(y, 0.0))
    o_ref[...] = y.astype(o_ref.dtype)


def _matmul(a, w, layer, *, tm, tn, relu2=False, name):
    m, k = a.shape
    n = w.shape[-1]
    tm, tn = min(tm, m), min(tn, n)
    return pl.pallas_call(
        functools.partial(_mm_kernel, relu2=relu2),
        grid=(m // tm, n // tn),
        in_specs=[
            pl.BlockSpec((tm, k), lambda i, j: (i, 0)),
            pl.BlockSpec((None, k, tn), lambda i, j: (layer, 0, j)),
        ],
        out_specs=pl.BlockSpec((tm, tn), lambda i, j: (i, j)),
        out_shape=jax.ShapeDtypeStruct((m, n), BF16),
        compiler_params=_params("arbitrary", "arbitrary"),
        name=name,
    )(a, w)


def _mm_row_kernel(a_ref, w_ref, x_ref, gpost_ref, mod_ref, gnext_ref, modn_ref, *rest,
                   gate_row, next_rows, nk, nt):
    if next_rows is None:
        xo_ref, acc0_ref, acc1_ref = rest
    else:
        xo_ref, ho_ref, acc0_ref, acc1_ref = rest
    i = pl.program_id(0)
    k = pl.program_id(1)
    tm, d = acc0_ref.shape
    sub = tm // nk

    @pl.when((i == 0) & (k == 0))
    def _():
        acc0_ref[...] = jnp.zeros_like(acc0_ref)
        acc1_ref[...] = jnp.zeros_like(acc1_ref)

    def epilogue(acc_ref):
        rows = pl.ds(pl.multiple_of(k * sub, sub), sub)
        f = acc_ref[rows, :]
        acc_ref[rows, :] = jnp.zeros((sub, d), F32)
        xn = x_ref[rows, :] + _rms(f, gpost_ref[...] * mod_ref[gate_row:gate_row + 1, :])
        xo_ref[rows, :] = xn
        if next_rows is not None:
            shift_row, scale_row = next_rows
            y = _rms(xn, gnext_ref[...] * (1.0 + modn_ref[scale_row:scale_row + 1, :]))
            ho_ref[rows, :] = (y + modn_ref[shift_row:shift_row + 1, :]).astype(BF16)

    for parity, (acc_mm, acc_ep) in enumerate(((acc0_ref, acc1_ref), (acc1_ref, acc0_ref))):
        @pl.when((i < nt) & (i % 2 == parity))
        def _():
            epilogue(acc_ep)
            acc_mm[...] += jnp.dot(a_ref[...], w_ref[...], preferred_element_type=F32)

    @pl.when(i == nt)
    def _():
        epilogue(acc0_ref if (nt - 1) % 2 == 0 else acc1_ref)


def _matmul_row(a, w, x, gpost, gnext, mod, layer, next_layer, grp, *, gate_row, next_rows, tm, tk, name):
    m, kdim = a.shape
    d = w.shape[-1]
    tm, tk = min(tm, m), min(tk, kdim)
    nk, nt = kdim // tk, m // tm
    assert tm % nk == 0 and (tm // nk) % 16 == 0
    cur = lambda i: jnp.minimum(i, nt - 1)
    kcur = lambda i, k: jnp.where(i < nt, k, nk - 1)
    prev = lambda i: jnp.maximum(i - 1, 0)
    vec = pl.BlockSpec((None, 1, d), lambda i, k: (layer, 0, 0))
    vecn = pl.BlockSpec((None, 1, d), lambda i, k: (next_layer, 0, 0))
    out_specs = [pl.BlockSpec((tm, d), lambda i, k: (prev(i), 0))]
    out_shape = [jax.ShapeDtypeStruct((m, d), F32)]
    if next_rows is not None:
        out_specs.append(pl.BlockSpec((tm, d), lambda i, k: (prev(i), 0)))
        out_shape.append(jax.ShapeDtypeStruct((m, d), BF16))
    return pl.pallas_call(
        functools.partial(_mm_row_kernel, gate_row=gate_row, next_rows=next_rows, nk=nk, nt=nt),
        grid=(nt + 1, nk),
        in_specs=[
            pl.BlockSpec((tm, tk), lambda i, k: (cur(i), kcur(i, k))),
            pl.BlockSpec((None, tk, d), lambda i, k: (layer, kcur(i, k), 0)),
            pl.BlockSpec((tm, d), lambda i, k: (prev(i), 0)),
            vec,
            pl.BlockSpec((None, None, MOD_ROWS, d), lambda i, k: (layer, grp(prev(i) * tm), 0, 0)),
            vecn,
            pl.BlockSpec((None, None, MOD_ROWS, d), lambda i, k: (next_layer, grp(prev(i) * tm), 0, 0)),
        ],
        out_specs=out_specs,
        out_shape=out_shape,
        scratch_shapes=[pltpu.VMEM((tm, d), F32), pltpu.VMEM((tm, d), F32)],
        compiler_params=_params("arbitrary", "arbitrary"),
        name=name,
    )(a, w, x, gpost, mod, gnext, mod)


def _prep_a_kernel(p_ref, gq_ref, gk_ref, *rest, rope, cache):
    rest = list(rest)
    tabs = [rest.pop(0) for _ in range(3)] if rope else None
    q_ref, k_ref = rest[:2]
    scale = HEAD_DIM ** -0.5 * LOG2E
    for h in range(A_HEADS):
        y = _rms(p_ref[:, h * HEAD_DIM:(h + 1) * HEAD_DIM].astype(F32), gq_ref[...])
        if rope:
            y = _rope(y, tabs[0][...], tabs[1][...], tabs[2][...], HEAD_DIM // 4)
        q_ref[:, h * HEAD_DIM:(h + 1) * HEAD_DIM] = (y * scale).astype(BF16)
    for g in range(A_KV_HEADS):
        lo = OFF_AK + g * HEAD_DIM
        y = _rms(p_ref[:, lo:lo + HEAD_DIM].astype(F32), gk_ref[...])
        if cache:
            rest[2][:, g * HEAD_DIM:(g + 1) * HEAD_DIM] = y
        if rope:
            y = _rope(y, tabs[0][...], tabs[1][...], tabs[2][...], HEAD_DIM // 4)
        k_ref[:, g * HEAD_DIM:(g + 1) * HEAD_DIM] = y.astype(BF16)
    if cache:
        rest[3][...] = p_ref[:, OFF_AV:OFF_BQ].astype(F32)


def _prep_a(proj, gq, gk, layer, tabs, n_per, *, cache):
    m = proj.shape[0]
    tr = min(512, n_per)
    rope = tabs is not None
    width = OFF_BQ
    kvw = A_KV_HEADS * HEAD_DIM
    in_specs = [
        pl.BlockSpec((tr, width), lambda i: (i, 0)),
        pl.BlockSpec((None, 1, HEAD_DIM), lambda i: (layer, 0, 0)),
        pl.BlockSpec((None, 1, HEAD_DIM), lambda i: (layer, 0, 0)),
    ]
    args = [proj, gq, gk]
    if rope:
        nt = n_per // tr
        in_specs += [pl.BlockSpec((tr, LANE), lambda i: (i % nt, 0))] * 3
        args += list(tabs)
    out_specs = [pl.BlockSpec((tr, A_HEADS * HEAD_DIM), lambda i: (i, 0)),
                 pl.BlockSpec((tr, kvw), lambda i: (i, 0))]
    out_shape = [jax.ShapeDtypeStruct((m, A_HEADS * HEAD_DIM), BF16),
                 jax.ShapeDtypeStruct((m, kvw), BF16)]
    if cache:
        out_specs += [pl.BlockSpec((tr, kvw), lambda i: (i, 0))] * 2
        out_shape += [jax.ShapeDtypeStruct((m, kvw), F32)] * 2
    return pl.pallas_call(
        functools.partial(_prep_a_kernel, rope=rope, cache=cache),
        grid=(m // tr,),
        in_specs=in_specs,
        out_specs=out_specs,
        out_shape=out_shape,
        compiler_params=_params("arbitrary"),
        name="prep_a",
    )(*args)


def _prep_cq_kernel(p_ref, g_ref, w_ref, *rest, rope):
    rest = list(rest)
    tabs = [rest.pop(0) for _ in range(3)] if rope else None
    q_ref = rest[0]
    scale = (C_NOPE_DIM + C_ROPE_DIM) ** -0.5 * LOG2E
    y = _rms(p_ref[...].astype(F32), g_ref[...]).astype(BF16)
    z = jnp.dot(y, w_ref[...], preferred_element_type=F32)
    for h in range(C_HEADS):
        lo = h * C_QK_PAD
        q_ref[:, lo:lo + C_NOPE_DIM] = (z[:, lo:lo + C_NOPE_DIM] * scale).astype(BF16)
        r = z[:, lo + C_NOPE_DIM:lo + C_QK_PAD]
        if rope:
            r = _rope(r, tabs[0][...], tabs[1][...], tabs[2][...], C_ROPE_DIM // 4)
        q_ref[:, lo + C_NOPE_DIM:lo + C_QK_PAD] = (r * scale).astype(BF16)


def _prep_cq(proj, g, w_uq, layer, tabs, n_per):
    m = proj.shape[0]
    tr = min(512, n_per)
    rope = tabs is not None
    width = C_HEADS * C_QK_PAD
    in_specs = [
        pl.BlockSpec((tr, C_Q_RANK), lambda i: (i, OFF_CQ // C_Q_RANK)),
        pl.BlockSpec((None, 1, C_Q_RANK), lambda i: (layer, 0, 0)),
        pl.BlockSpec((None, C_Q_RANK, width), lambda i: (layer, 0, 0)),
    ]
    args = [proj, g, w_uq]
    if rope:
        nt = n_per // tr
        in_specs += [pl.BlockSpec((tr, LANE), lambda i: (i % nt, 0))] * 3
        args += list(tabs)
    return pl.pallas_call(
        functools.partial(_prep_cq_kernel, rope=rope),
        grid=(m // tr,),
        in_specs=in_specs,
        out_specs=pl.BlockSpec((tr, width), lambda i: (i, 0)),
        out_shape=jax.ShapeDtypeStruct((m, width), BF16),
        compiler_params=_params("arbitrary"),
        name="prep_cq",
    )(*args)


def _prep_ckv_kernel(ckv_ref, kr_ref, g_ref, w_ref, *rest, norm, rope, cache):
    rest = list(rest)
    tabs = [rest.pop(0) for _ in range(3)] if rope else None
    kc_ref, vc_ref = rest[:2]
    x = ckv_ref[...].astype(F32)
    if norm:
        x = _rms(x, g_ref[...])
    kr = kr_ref[...].astype(F32)
    if cache:
        rest[2][...] = x
        rest[3][...] = kr[:, :C_ROPE_DIM]
    if rope:
        kr = _rope(kr, tabs[0][...], tabs[1][...], tabs[2][...], C_ROPE_DIM // 4)
    krb = kr.astype(BF16)
    z = jnp.dot(x.astype(BF16), w_ref[...], preferred_element_type=F32)
    up = C_NOPE_DIM + C_V_DIM
    for h in range(C_HEADS):
        kc_ref[:, h * C_QK_PAD:h * C_QK_PAD + C_NOPE_DIM] = z[:, h * up:h * up + C_NOPE_DIM].astype(BF16)
        kc_ref[:, h * C_QK_PAD + C_NOPE_DIM:(h + 1) * C_QK_PAD] = krb
        vc_ref[:, h * C_V_DIM:(h + 1) * C_V_DIM] = z[:, h * up + C_NOPE_DIM:(h + 1) * up].astype(BF16)


def _prep_ckv(ckv_src, ckv_blk, kr_src, kr_blk, g, w_ukv, layer, tabs, n_per, *, norm, cache):
    m = ckv_src.shape[0]
    tr = min(512, n_per)
    rope = tabs is not None
    in_specs = [
        pl.BlockSpec((tr, C_KV_RANK), lambda i: (i, ckv_blk)),
        pl.BlockSpec((tr, LANE), lambda i: (i, kr_blk)),
        pl.BlockSpec((None, 1, C_KV_RANK), lambda i: (layer, 0, 0)),
        pl.BlockSpec((None, C_KV_RANK, C_HEADS * (C_NOPE_DIM + C_V_DIM)), lambda i: (layer, 0, 0)),
    ]
    args = [ckv_src, kr_src, g, w_ukv]
    if rope:
        nt = n_per // tr
        in_specs += [pl.BlockSpec((tr, LANE), lambda i: (i % nt, 0))] * 3
        args += list(tabs)
    out_specs = [pl.BlockSpec((tr, C_HEADS * C_QK_PAD), lambda i: (i, 0)),
                 pl.BlockSpec((tr, C_HEADS * C_V_DIM), lambda i: (i, 0))]
    out_shape = [jax.ShapeDtypeStruct((m, C_HEADS * C_QK_PAD), BF16),
                 jax.ShapeDtypeStruct((m, C_HEADS * C_V_DIM), BF16)]
    if cache:
        out_specs += [pl.BlockSpec((tr, C_KV_RANK), lambda i: (i, 0)),
                      pl.BlockSpec((tr, C_ROPE_DIM), lambda i: (i, 0))]
        out_shape += [jax.ShapeDtypeStruct((m, C_KV_RANK), F32),
                      jax.ShapeDtypeStruct((m, C_ROPE_DIM), F32)]
    return pl.pallas_call(
        functools.partial(_prep_ckv_kernel, norm=norm, rope=rope, cache=cache),
        grid=(m // tr,),
        in_specs=in_specs,
        out_specs=out_specs,
        out_shape=out_shape,
        compiler_params=_params("arbitrary"),
        name="prep_ckv",
    )(*args)


def _attn_kernel(q_ref, k_ref, v_ref, *rest, gps, rep, dqk, dv, tq, tk, ctx):
    if ctx:
        k2_ref, v2_ref, o_ref, acc_ref, s0_ref, s1_ref = rest
    else:
        o_ref, acc_ref, s0_ref, s1_ref = rest
    s_refs = (s0_ref, s1_ref)

    items = []
    for g in range(gps):
        chunks = [(k_ref, v_ref, c * tk, tk) for c in range(k_ref.shape[0] // tk)]
        if ctx:
            nctx = k2_ref.shape[0]
            chunks += [(k2_ref, v2_ref, lo, min(tk, nctx - lo)) for lo in range(0, nctx, tk)]
        items += [(g, c == 0, c == len(chunks) - 1) + ch for c, ch in enumerate(chunks)]

    def queries(g):
        cols = [q_ref[:, (g * rep + r) * dqk:(g * rep + r + 1) * dqk] for r in range(rep)]
        return cols[0] if rep == 1 else jnp.concatenate(cols, axis=0)

    def scores(t):
        g, _, _, kr, _, lo, size = items[t]
        s_refs[t % 2][:, :size] = lax.dot_general(
            queries(g), kr[lo:lo + size, g * dqk:(g + 1) * dqk], (((1,), (1,)), ((), ())),
            preferred_element_type=F32)

    scores(0)
    m = None
    for t, (g, first, last, _, vr, lo, size) in enumerate(items):
        if t + 1 < len(items):
            scores(t + 1)
        s = s_refs[t % 2][:, :size]
        m_blk = jnp.max(s, axis=-1, keepdims=True)
        m_new = m_blk if first else jnp.maximum(m, m_blk)
        p = jnp.exp2(s - m_new).astype(BF16)
        v_ext = jnp.concatenate([vr[lo:lo + size, g * dv:(g + 1) * dv], jnp.ones((size, dv), BF16)], axis=1)
        pv = jnp.dot(p, v_ext, preferred_element_type=F32)
        if first:
            acc_ref[g] = pv
        else:
            acc_ref[g] = jnp.exp2(m - m_new) * acc_ref[g] + pv
        m = m_new
        if last:
            o = acc_ref[g, :, :dv] / acc_ref[g, :, dv:]
            for r in range(rep):
                o_ref[:, (g * rep + r) * dv:(g * rep + r + 1) * dv] = o[r * tq:(r + 1) * tq].astype(o_ref.dtype)


def _attention(q, k, kblk0, v, vblk0, ctx_kv, batch, n, *, groups, gps, rep, dqk, dv, rows, name):
    tq = min(rows // rep, n)
    tk = min(512, n)
    nq = n // tq
    assert groups % gps == 0 and kblk0 % gps == 0 and vblk0 % gps == 0 and (ctx_kv is None or gps == 1)
    in_specs = [
        pl.BlockSpec((tq, gps * rep * dqk), lambda b, g, i: (b * nq + i, g)),
        pl.BlockSpec((n, gps * dqk), lambda b, g, i: (b, kblk0 // gps + g)),
        pl.BlockSpec((n, gps * dv), lambda b, g, i: (b, vblk0 // gps + g)),
    ]
    args = [q, k, v]
    if ctx_kv is not None:
        k2, v2 = ctx_kv
        nc = k2.shape[0] // batch
        in_specs += [pl.BlockSpec((nc, dqk), lambda b, g, i: (b, g)),
                     pl.BlockSpec((nc, dv), lambda b, g, i: (b, g))]
        args += [k2, v2]
    return pl.pallas_call(
        functools.partial(_attn_kernel, gps=gps, rep=rep, dqk=dqk, dv=dv, tq=tq, tk=tk, ctx=ctx_kv is not None),
        grid=(batch, groups // gps, nq),
        in_specs=in_specs,
        out_specs=pl.BlockSpec((tq, gps * rep * dv), lambda b, g, i: (b * nq + i, g)),
        out_shape=jax.ShapeDtypeStruct((batch * n, groups * rep * dv), BF16),
        scratch_shapes=[pltpu.VMEM((gps, rep * tq, 2 * dv), F32),
                        pltpu.VMEM((rep * tq, tk), F32),
                        pltpu.VMEM((rep * tq, tk), F32)],
        compiler_params=_params("arbitrary", "arbitrary", "arbitrary"),
        name=name,
    )(*args)


def _ret_head(lgf, lgb, io, tabs, nc, kvf_ref, kvb_ref, state_ref):
    c_len = B_CHUNK
    rope = tabs is not None
    ii = lax.broadcasted_iota(jnp.int32, (c_len, c_len), 0)
    jj = lax.broadcasted_iota(jnp.int32, (c_len, c_len), 1)
    diff = (ii - jj).astype(F32)
    dmat = (jnp.where(diff >= 0, jnp.exp(jnp.maximum(diff, 0.0) * lgf), 0.0)
            + jnp.where(diff <= 0, jnp.exp(jnp.maximum(-diff, 0.0) * lgb), 0.0))
    ri = lax.broadcasted_iota(jnp.int32, (c_len, 1), 0).astype(F32)
    qdec_f = jnp.exp((ri + 1.0) * lgf)
    qdec_b = jnp.exp((c_len - ri) * lgb)
    kdec_f = jnp.exp((c_len - 1.0 - ri) * lgf)
    kdec_b = jnp.exp(ri * lgb)
    one = jnp.ones((1, 1), F32)
    cdec_f = jnp.exp(one * (c_len * lgf))
    cdec_b = jnp.exp(one * (c_len * lgb))
    kscale = B_QK_DIM ** -0.5

    def roped(load, lo):
        x = load(lo).astype(F32)
        if rope:
            t = [tab[pl.ds(lo, c_len), :] for tab in tabs]
            x = _rope(x, t[0], t[1], t[2], B_QK_DIM // 4)
        return x

    def increments(c, carry):
        lo = pl.multiple_of(c * c_len, c_len)
        kc = roped(io["k"], lo) * kscale
        kk = jnp.concatenate([kc * kdec_f, kc * kdec_b], axis=1).T.astype(BF16)
        kv = jnp.dot(kk, io["v"](lo), preferred_element_type=F32)
        kvf_ref[c] = kv[:B_QK_DIM]
        kvb_ref[c] = kv[B_QK_DIM:]
        return carry

    lax.fori_loop(0, nc, increments, 0, unroll=_unroll(nc, 8))

    def scan_f(c, s):
        state_ref[c, :B_QK_DIM, :] = s.astype(BF16)
        return cdec_f * s + kvf_ref[c]

    def scan_b(t, s):
        c = nc - 1 - t
        state_ref[c, B_QK_DIM:, :] = s.astype(BF16)
        return cdec_b * s + kvb_ref[c]

    s_f = lax.fori_loop(0, nc, scan_f, io["s0f"], unroll=_unroll(nc, 2))
    s_b = lax.fori_loop(0, nc, scan_b, io["s0b"], unroll=_unroll(nc, 2))
    io["store_state"](s_f, s_b)

    def outputs(c, carry):
        lo = pl.multiple_of(c * c_len, c_len)
        qc = roped(io["q"], lo)
        kc = (roped(io["k"], lo) * kscale).astype(BF16)
        s = lax.dot_general(qc.astype(BF16), kc, (((1,), (1,)), ((), ())), preferred_element_type=F32) * dmat
        o = jnp.dot(s.astype(BF16), io["v"](lo), preferred_element_type=F32)
        qq = jnp.concatenate([qc * qdec_f, qc * qdec_b], axis=1).astype(BF16)
        o = o + jnp.dot(qq, state_ref[c], preferred_element_type=F32)
        gate = io["gate"](lo).astype(F32)
        io["store_o"](lo, (gate * jax.nn.sigmoid(gate) * _rms(o, io["gn"])).astype(BF16))
        return carry

    lax.fori_loop(0, nc, outputs, 0, unroll=_unroll(nc, 4))


def _ret_kernel(lg_ref, *refs, n, rope, emit_state, all_heads):
    refs = list(refs)
    data = [refs.pop(0) for _ in range(2 if all_heads else 4)]
    gn_ref, s0f_ref, s0b_ref = refs.pop(0), refs.pop(0), refs.pop(0)
    tabs = [refs.pop(0) for _ in range(3)] if rope else None
    o_ref = refs.pop(0)
    sfo_ref, sbo_ref = (refs.pop(0), refs.pop(0)) if emit_state else (None, None)
    kvf_ref, kvb_ref, state_ref = refs
    nc = n // B_CHUNK
    rows = lambda lo: pl.ds(lo, B_CHUNK)
    qw, vw = B_HEADS * B_QK_DIM, B_V_DIM

    for hh in range(B_HEADS if all_heads else 1):
        if all_heads:
            h = hh
            blk1, blk2 = data
            v_src = (blk1, 2 * qw + hh * vw) if hh < 2 else (blk2, (hh - 2) * vw)
            col = lambda ref, c0, w: (lambda lo: ref[rows(lo), c0:c0 + w])
            io = dict(
                q=col(blk1, hh * B_QK_DIM, B_QK_DIM), k=col(blk1, qw + hh * B_QK_DIM, B_QK_DIM),
                v=col(v_src[0], v_src[1], vw), gate=col(blk2, 2 * vw + hh * vw, vw),
                gn=gn_ref[:, hh * vw:(hh + 1) * vw], s0f=s0f_ref[hh], s0b=s0b_ref[hh])

            def store_o(lo, val, hh=hh):
                o_ref[rows(lo), hh * vw:(hh + 1) * vw] = val

            def store_state(sf, sb, hh=hh):
                if emit_state:
                    sfo_ref[hh] = sf
                    sbo_ref[hh] = sb
        else:
            h = pl.program_id(1)
            whole = lambda ref: (lambda lo: ref[rows(lo), :])
            io = dict(q=whole(data[0]), k=whole(data[1]), v=whole(data[2]), gate=whole(data[3]),
                      gn=gn_ref[...], s0f=s0f_ref[...], s0b=s0b_ref[...])

            def store_o(lo, val):
                o_ref[rows(lo), :] = val

            def store_state(sf, sb):
                if emit_state:
                    sfo_ref[...] = sf
                    sbo_ref[...] = sb
        io.update(store_o=store_o, store_state=store_state)
        _ret_head(lg_ref[0, h], lg_ref[1, h], io, tabs, nc, kvf_ref.at[hh], kvb_ref.at[hh], state_ref.at[hh])


def _retention(proj, lg, gn, s0f, s0b, layer, tabs, batch, n, *, emit_state, all_heads):
    rope = tabs is not None
    if all_heads:
        wide = (OFF_CQ - OFF_BQ) // 2
        assert OFF_BQ % wide == 0 and wide == 2 * B_HEADS * B_QK_DIM + 2 * B_V_DIM
        st_spec = pl.BlockSpec((None, B_HEADS, B_QK_DIM, B_V_DIM), lambda b, h, lg_: (b, 0, 0, 0))
        in_specs = [
            pl.BlockSpec((n, wide), lambda b, h, lg_: (b, OFF_BQ // wide)),
            pl.BlockSpec((n, wide), lambda b, h, lg_: (b, OFF_BQ // wide + 1)),
            pl.BlockSpec((None, 1, B_HEADS * B_V_DIM), lambda b, h, lg_: (layer, 0, 0)),
        ]
        args = [proj, proj, gn]
        out_specs = [pl.BlockSpec((n, B_HEADS * B_V_DIM), lambda b, h, lg_: (b, 0))]
        heads_per_step = B_HEADS
    else:
        qb, kb = OFF_BQ // B_QK_DIM, OFF_BK // B_QK_DIM
        vb, gb = OFF_BV // B_V_DIM, OFF_BG // B_V_DIM
        st_spec = pl.BlockSpec((None, None, B_QK_DIM, B_V_DIM), lambda b, h, lg_: (b, h, 0, 0))
        in_specs = [
            pl.BlockSpec((n, B_QK_DIM), lambda b, h, lg_: (b, qb + h)),
            pl.BlockSpec((n, B_QK_DIM), lambda b, h, lg_: (b, kb + h)),
            pl.BlockSpec((n, B_V_DIM), lambda b, h, lg_: (b, vb + h)),
            pl.BlockSpec((n, B_V_DIM), lambda b, h, lg_: (b, gb + h)),
            pl.BlockSpec((None, 1, B_V_DIM), lambda b, h, lg_: (layer, 0, h)),
        ]
        args = [proj, proj, proj, proj, gn]
        out_specs = [pl.BlockSpec((n, B_V_DIM), lambda b, h, lg_: (b, h))]
        heads_per_step = 1
    in_specs += [st_spec, st_spec]
    args += [s0f, s0b]
    if rope:
        in_specs += [pl.BlockSpec((n, LANE), lambda b, h, lg_: (0, 0))] * 3
        args += list(tabs)
    out_shape = [jax.ShapeDtypeStruct((batch * n, B_HEADS * B_V_DIM), BF16)]
    if emit_state:
        out_specs += [st_spec, st_spec]
        out_shape += [jax.ShapeDtypeStruct((batch, B_HEADS, B_QK_DIM, B_V_DIM), F32)] * 2
    nc = n // B_CHUNK
    return pl.pallas_call(
        functools.partial(_ret_kernel, n=n, rope=rope, emit_state=emit_state, all_heads=all_heads),
        grid_spec=pltpu.PrefetchScalarGridSpec(
            num_scalar_prefetch=1,
            grid=(batch, B_HEADS // heads_per_step),
            in_specs=in_specs,
            out_specs=out_specs,
            scratch_shapes=[pltpu.VMEM((heads_per_step, nc, B_QK_DIM, B_V_DIM), F32),
                            pltpu.VMEM((heads_per_step, nc, B_QK_DIM, B_V_DIM), F32),
                            pltpu.VMEM((heads_per_step, nc, 2 * B_QK_DIM, B_V_DIM), BF16)],
        ),
        out_shape=out_shape,
        compiler_params=_params("arbitrary", "arbitrary"),
        name="retention",
    )(lg, *args)


def _merge_kernel(oa_ref, ob_ref, oc_ref, wa_ref, wb_ref, wc_ref, ga_ref, gb_ref, gc_ref, o_ref):
    def branch(o, w, g):
        return jax.nn.sigmoid(g[...].astype(F32)) * jnp.dot(o[...], w[...], preferred_element_type=F32)

    o_ref[...] = (branch(oa_ref, wa_ref, ga_ref) + branch(ob_ref, wb_ref, gb_ref)
                  + branch(oc_ref, wc_ref, gc_ref)).astype(BF16)


def _merge(oa, ob, oc, wa, wb, wc, gates, layer, d):
    m = oa.shape[0]
    tm = min(1024, m)
    tn = min(1024, d)
    nb = d // tn
    o_spec = lambda o: pl.BlockSpec((tm, o.shape[1]), lambda i, j: (i, 0))
    w_spec = lambda w: pl.BlockSpec((None, w.shape[1], tn), lambda i, j: (layer, 0, j))
    g_spec = lambda br: pl.BlockSpec((tm, tn), lambda i, j: (i, br * nb + j))
    return pl.pallas_call(
        _merge_kernel,
        grid=(m // tm, nb),
        in_specs=[o_spec(oa), o_spec(ob), o_spec(oc), w_spec(wa), w_spec(wb), w_spec(wc),
                  g_spec(0), g_spec(1), g_spec(2)],
        out_specs=pl.BlockSpec((tm, tn), lambda i, j: (i, j)),
        out_shape=jax.ShapeDtypeStruct((m, d), BF16),
        compiler_params=_params("arbitrary", "arbitrary"),
        name="merge",
    )(oa, ob, oc, wa, wb, wc, gates, gates, gates)


def _rope_tables(n_tokens, dim, pad):
    pos = jnp.arange(n_tokens, dtype=jnp.int32)
    row = (pos // GRID_W).astype(F32)[:, None]
    col = (pos % GRID_W).astype(F32)[:, None]
    quarter = dim // 4
    inv = ROPE_THETA ** (-jnp.arange(quarter, dtype=F32) / quarter)[None, :]
    ang = jnp.concatenate([row * inv, row * inv, col * inv, col * inv], axis=1)
    cos, sin = jnp.cos(ang), jnp.sin(ang)
    first = (jnp.arange(dim) // quarter) % 2 == 0
    sin_lo = jnp.where(first[None, :], -sin, 0.0)
    sin_hi = jnp.where(first[None, :], 0.0, sin)
    if pad:
        z = jnp.zeros_like(cos)
        tabs = [jnp.concatenate([t, z], axis=1) for t in (cos, sin_lo, sin_hi)]
    else:
        tabs = [jnp.tile(t, (1, LANE // dim)) for t in (cos, sin_lo, sin_hi)]
    return tuple(tabs)


def kernel(x_prompt, x_sample, c, cache_attn_k, cache_attn_v, state_ret_fwd, state_ret_bwd, cache_mla_ckv, cache_mla_krope, c_ctx, w_mod, b_mod, g_pre_mix, g_post_mix, g_pre_mlp, g_post_mlp, w_in, attn_q_norm, attn_k_norm, ret_decay_fwd, ret_decay_bwd, ret_gn, mla_q_norm, mla_kv_norm, w_mla_uq, w_mla_ukv, w_branch_a, w_branch_b, w_branch_c, w_out, w_mlp_up, w_mlp_down):
    bp, n_p, d = x_prompt.shape
    bs, n_s, _ = x_sample.shape
    depth = w_in.shape[0]
    past = cache_attn_k.shape[2]
    assert n_p % B_CHUNK == 0 and n_s % B_CHUNK == 0

    w_main, w_gate = _split_w_in(w_in)
    w_uq = jnp.pad(w_mla_uq.reshape(depth, C_Q_RANK, C_HEADS, C_NOPE_DIM + C_ROPE_DIM),
                   ((0, 0), (0, 0), (0, 0), (0, C_QK_PAD - C_NOPE_DIM - C_ROPE_DIM)))
    w_uq = w_uq.reshape(depth, C_Q_RANK, C_HEADS * C_QK_PAD).astype(BF16)
    w_ukv = w_mla_ukv.astype(BF16)
    wa, wb, wc = w_branch_a.astype(BF16), w_branch_b.astype(BF16), w_branch_c.astype(BF16)
    w_o, w_up, w_dn = w_out.astype(BF16), w_mlp_up.astype(BF16), w_mlp_down.astype(BF16)
    vec = lambda g: g.reshape(depth, 1, g.shape[-1])
    g_pre_mix, g_post_mix, g_pre_mlp, g_post_mlp = map(vec, (g_pre_mix, g_post_mix, g_pre_mlp, g_post_mlp))
    gqn, gkn, gcq, gckv, gret = map(vec, (attn_q_norm, attn_k_norm, mla_q_norm, mla_kv_norm, ret_gn))
    lg = jnp.stack([jax.nn.log_sigmoid(ret_decay_fwd.astype(F32)),
                    jax.nn.log_sigmoid(ret_decay_bwd.astype(F32))], axis=1)

    rows = -(-(1 + bs) // 8) * 8
    cvec = jnp.zeros((rows, d), F32).at[0].set(c_ctx).at[1:1 + bs].set(c)
    mod = _modulation(cvec, w_mod, b_mod).reshape(depth, rows, 6, d)
    mod = jnp.pad(mod, ((0, 0), (0, 0), (0, MOD_ROWS - 6), (0, 0)))

    tabs128 = _rope_tables(n_s, HEAD_DIM, pad=False)
    tabs64 = _rope_tables(n_s, C_ROPE_DIM, pad=False)
    tabs64p = _rope_tables(n_s, C_ROPE_DIM, pad=True)

    ctx_k = cache_attn_k.astype(BF16).reshape(bs, depth, past, A_KV_HEADS * HEAD_DIM)
    ctx_v = cache_attn_v.astype(BF16).reshape(bs, depth, past, A_KV_HEADS * HEAD_DIM)
    ctx_kr = jnp.pad(cache_mla_krope, ((0, 0), (0, 0), (0, 0), (0, LANE - C_ROPE_DIM)))
    zero_state = jnp.zeros((bp, B_HEADS, B_QK_DIM, B_V_DIM), F32)

    streams = {
        "p": dict(x=x_prompt.reshape(bp * n_p, d), batch=bp, n=n_p, grp=lambda r: 0, latent=False),
        "s": dict(x=x_sample.reshape(bs * n_s, d), batch=bs, n=n_s, grp=lambda r: 1 + r // n_s, latent=True),
    }
    for st in streams.values():
        st["h"] = _prenorm(st["x"], g_pre_mix, mod, 0, st["grp"])
    caches = [[] for _ in range(6)]

    for l in range(depth):
        for st in streams.values():
            batch, n, grp, latent = st["batch"], st["n"], st["grp"], st["latent"]
            proj = _matmul(st["h"], w_main, l, tm=2048, tn=512, name="w_in")
            gates = _matmul(st["h"], w_gate, l, tm=2048, tn=512, name="w_gates")
            t128, t64, t64p = (tabs128, tabs64, tabs64p) if latent else (None, None, None)

            pa = _prep_a(proj, gqn, gkn, l, t128, n, cache=not latent)
            qa, ka = pa[0], pa[1]
            ctx_a = (ctx_k[:, l].reshape(bs * past, -1), ctx_v[:, l].reshape(bs * past, -1)) if latent else None
            o_a = _attention(qa, ka, 0, proj, OFF_AV // HEAD_DIM, ctx_a, batch, n,
                             groups=A_KV_HEADS, gps=1 if latent else A_KV_HEADS, rep=A_HEADS // A_KV_HEADS,
                             dqk=HEAD_DIM, dv=HEAD_DIM, rows=1024, name="attn_a")

            if latent:
                s0f, s0b = state_ret_fwd[:, l], state_ret_bwd[:, l]
            else:
                s0f = s0b = zero_state
            rb = _retention(proj, lg[l], gret, s0f, s0b, l, t128, batch, n,
                            emit_state=not latent, all_heads=not latent)
            o_b = rb[0]

            qc = _prep_cq(proj, gcq, w_uq, l, t64p, n)
            pc = _prep_ckv(proj, OFF_CKV // C_KV_RANK, proj, OFF_CKR // LANE, gckv, w_ukv, l, t64, n,
                           norm=True, cache=not latent)
            kc, vc = pc[0], pc[1]
            ctx_c = None
            if latent:
                ctx_c = _prep_ckv(cache_mla_ckv[:, l].reshape(bs * past, C_KV_RANK), 0,
                                  ctx_kr[:, l].reshape(bs * past, LANE), 0, gckv, w_ukv, l, None, past,
                                  norm=False, cache=False)
            o_c = _attention(qc, kc, 0, vc, 0, ctx_c, batch, n,
                             groups=C_HEADS, gps=1 if latent else C_HEADS, rep=1,
                             dqk=C_QK_PAD, dv=C_V_DIM, rows=1024, name="attn_c")

            if not latent:
                for dst, val in zip(caches, (pa[2], pa[3], rb[1], rb[2], pc[2], pc[3])):
                    dst.append(val)

            merged = _merge(o_a, o_b, o_c, wa, wb, wc, gates, l, d)
            x, h2 = _matmul_row(merged, w_o, st["x"], g_post_mix, g_pre_mlp, mod, l, l, grp,
                                gate_row=2, next_rows=(3, 4), tm=512, tk=2048, name="w_out")
            hid = _matmul(h2, w_up, l, tm=2048, tn=512, relu2=True, name="mlp_up")
            last = l == depth - 1
            res = _matmul_row(hid, w_dn, x, g_post_mlp, g_pre_mix, mod, l, min(l + 1, depth - 1), grp,
                              gate_row=5, next_rows=None if last else (0, 1), tm=512, tk=2048, name="mlp_down")
            st["x"] = res[0]
            st["h"] = None if last else res[1]

    y_p = streams["p"]["x"].reshape(bp, n_p, d)
    y_s = streams["s"]["x"].reshape(bs, n_s, d)
    nk, nv, sf, sb, ckv, kr = caches
    new_attn_k = jnp.stack(nk, axis=1).reshape(bp, n_p, depth, A_KV_HEADS, HEAD_DIM).transpose(0, 2, 1, 3, 4)
    new_attn_v = jnp.stack(nv, axis=1).reshape(bp, n_p, depth, A_KV_HEADS, HEAD_DIM).transpose(0, 2, 1, 3, 4)
    new_ret_fwd = jnp.stack(sf, axis=1)
    new_ret_bwd = jnp.stack(sb, axis=1)
    new_mla_ckv = jnp.stack(ckv, axis=1).reshape(bp, n_p, depth, C_KV_RANK).transpose(0, 2, 1, 3)
    new_mla_krope = jnp.stack(kr, axis=1).reshape(bp, n_p, depth, C_ROPE_DIM).transpose(0, 2, 1, 3)
    return (y_p, y_s, new_attn_k, new_attn_v, new_ret_fwd, new_ret_bwd, new_mla_ckv, new_mla_krope)
```

```python
import functools

import jax
import jax.numpy as jnp
from jax import lax
from jax.experimental import pallas as pl
from jax.experimental.pallas import tpu as pltpu

F32 = jnp.float32
BF16 = jnp.bfloat16

EPS = 1e-6
LOG2E = 1.4426950408889634
ROPE_THETA = 10000.0
GRID_W = 64
LANE = 128
HEAD_DIM = 128
A_HEADS = 8
A_KV_HEADS = 2
B_HEADS = 4
B_QK_DIM = 128
B_V_DIM = 256
B_CHUNK = 128
C_HEADS = 8
C_Q_RANK = 512
C_KV_RANK = 256
C_NOPE_DIM = 128
C_ROPE_DIM = 64
C_V_DIM = 128
C_QK_PAD = 256

OFF_AQ = 0
OFF_AK = OFF_AQ + A_HEADS * HEAD_DIM
OFF_AV = OFF_AK + A_KV_HEADS * HEAD_DIM
OFF_BQ = OFF_AV + A_KV_HEADS * HEAD_DIM
OFF_BK = OFF_BQ + B_HEADS * B_QK_DIM
OFF_BV = OFF_BK + B_HEADS * B_QK_DIM
OFF_BG = OFF_BV + B_HEADS * B_V_DIM
OFF_CQ = OFF_BG + B_HEADS * B_V_DIM
OFF_CKV = OFF_CQ + C_Q_RANK
OFF_CKR = OFF_CKV + C_KV_RANK
IN_MAIN = OFF_CKR + C_ROPE_DIM
MAIN_W = 5632
MOD_ROWS = 8

VMEM_LIMIT = 56 * 1024 * 1024


def _params(*sem):
    return pltpu.CompilerParams(dimension_semantics=sem, vmem_limit_bytes=VMEM_LIMIT)


def _rms(x, g):
    return x * lax.rsqrt(jnp.mean(x * x, axis=-1, keepdims=True) + EPS) * g


def _unroll(trips, want):
    while trips % want:
        want //= 2
    return want


def _rope(x, cos, sin_lo, sin_hi, shift):
    return (x * cos + pltpu.roll(x, LANE - shift, 1) * sin_lo + pltpu.roll(x, shift, 1) * sin_hi)


def _mod_kernel(c_ref, w_ref, b_ref, o_ref):
    c = c_ref[...]
    a = (c * jax.nn.sigmoid(c)).astype(BF16)
    o_ref[...] = jnp.dot(a, w_ref[...].astype(BF16), preferred_element_type=F32) + b_ref[...]


def _modulation(cvec, w_mod, b_mod):
    depth, d, n = w_mod.shape
    rows = cvec.shape[0]
    tn = min(1024, n)
    return pl.pallas_call(
        _mod_kernel,
        grid=(depth, n // tn),
        in_specs=[
            pl.BlockSpec((rows, d), lambda l, j: (0, 0)),
            pl.BlockSpec((None, d, tn), lambda l, j: (l, 0, j)),
            pl.BlockSpec((None, 1, tn), lambda l, j: (l, 0, j)),
        ],
        out_specs=pl.BlockSpec((None, rows, tn), lambda l, j: (l, 0, j)),
        out_shape=jax.ShapeDtypeStruct((depth, rows, n), F32),
        compiler_params=_params("arbitrary", "arbitrary"),
        name="modulation",
    )(cvec, w_mod, b_mod.reshape(depth, 1, n))


def _cast_kernel(w_ref, o_ref):
    o_ref[...] = w_ref[0].astype(BF16)


def _cast_rows(w_t, row0, n_rows, name):
    depth, _, d = w_t.shape
    tr = 512
    assert n_rows % tr == 0
    return pl.pallas_call(
        _cast_kernel,
        grid=(depth, n_rows // tr),
        in_specs=[pl.BlockSpec((pl.Element(1), pl.Element(tr), pl.Element(d)),
                               lambda l, i: (l, pl.multiple_of(row0 + i * tr, 64), 0))],
        out_specs=pl.BlockSpec((None, tr, d), lambda l, i: (l, i, 0)),
        out_shape=jax.ShapeDtypeStruct((depth, n_rows, d), BF16),
        compiler_params=_params("arbitrary", "arbitrary"),
        name=name,
    )(w_t)


def _split_w_in(w_in):
    width = w_in.shape[-1]
    w_t = jnp.transpose(w_in, (0, 2, 1))
    return (_cast_rows(w_t, 0, MAIN_W, "cast_w_main"),
            _cast_rows(w_t, IN_MAIN, width - IN_MAIN, "cast_w_gate"))


def _prenorm_kernel(x_ref, g_ref, mod_ref, h_ref):
    y = _rms(x_ref[...], g_ref[...])
    h_ref[...] = (y * (1.0 + mod_ref[1:2, :]) + mod_ref[0:1, :]).astype(BF16)


def _prenorm(x, g, mod, layer, grp):
    m, d = x.shape
    tm = min(512, m)
    return pl.pallas_call(
        _prenorm_kernel,
        grid=(m // tm,),
        in_specs=[
            pl.BlockSpec((tm, d), lambda i: (i, 0)),
            pl.BlockSpec((None, 1, d), lambda i: (layer, 0, 0)),
            pl.BlockSpec((None, None, MOD_ROWS, d), lambda i: (layer, grp(i * tm), 0, 0)),
        ],
        out_specs=pl.BlockSpec((tm, d), lambda i: (i, 0)),
        out_shape=jax.ShapeDtypeStruct((m, d), BF16),
        compiler_params=_params("arbitrary"),
        name="prenorm",
    )(x, g, mod)


def _mm_kernel(a_ref, w_ref, o_ref, *, relu2, w_transposed):
    contract = (((1,), (1 if w_transposed else 0,)), ((), ()))
    y = lax.dot_general(a_ref[...], w_ref[...], contract, preferred_element_type=F32)
    if relu2:
        y = jnp.square(jnp.maximum(y, 0.0))
    o_ref[...] = y.astype(o_ref.dtype)


def _matmul(a, w, layer, *, tm, tn, relu2=False, w_transposed=False, name):
    m, k = a.shape
    n = w.shape[1] if w_transposed else w.shape[2]
    tm, tn = min(tm, m), min(tn, n)
    if w_transposed:
        w_spec = pl.BlockSpec((None, tn, k), lambda i, j: (layer, j, 0))
    else:
        w_spec = pl.BlockSpec((None, k, tn), lambda i, j: (layer, 0, j))
    return pl.pallas_call(
        functools.partial(_mm_kernel, relu2=relu2, w_transposed=w_transposed),
        grid=(m // tm, n // tn),
        in_specs=[
            pl.BlockSpec((tm, k), lambda i, j: (i, 0)),
            w_spec,
        ],
        out_specs=pl.BlockSpec((tm, tn), lambda i, j: (i, j)),
        out_shape=jax.ShapeDtypeStruct((m, n), BF16),
        compiler_params=_params("arbitrary", "arbitrary"),
        name=name,
    )(a, w)


def _mm_row_kernel(a_ref, w_ref, x_ref, gpost_ref, mod_ref, gnext_ref, modn_ref, *rest,
                   gate_row, next_rows, nk, nt):
    if next_rows is None:
        xo_ref, acc0_ref, acc1_ref = rest
    else:
        xo_ref, ho_ref, acc0_ref, acc1_ref = rest
    i = pl.program_id(0)
    k = pl.program_id(1)
    tm, d = acc0_ref.shape
    sub = tm // nk

    @pl.when((i == 0) & (k == 0))
    def _():
        acc0_ref[...] = jnp.zeros_like(acc0_ref)
        acc1_ref[...] = jnp.zeros_like(acc1_ref)

    def epilogue(acc_ref):
        rows = pl.ds(pl.multiple_of(k * sub, sub), sub)
        f = acc_ref[rows, :]
        acc_ref[rows, :] = jnp.zeros((sub, d), F32)
        xn = x_ref[rows, :] + _rms(f, gpost_ref[...] * mod_ref[gate_row:gate_row + 1, :])
        xo_ref[rows, :] = xn
        if next_rows is not None:
            shift_row, scale_row = next_rows
            y = _rms(xn, gnext_ref[...] * (1.0 + modn_ref[scale_row:scale_row + 1, :]))
            ho_ref[rows, :] = (y + modn_ref[shift_row:shift_row + 1, :]).astype(BF16)

    for parity, (acc_mm, acc_ep) in enumerate(((acc0_ref, acc1_ref), (acc1_ref, acc0_ref))):
        @pl.when((i < nt) & (i % 2 == parity))
        def _():
            epilogue(acc_ep)
            acc_mm[...] += jnp.dot(a_ref[...], w_ref[...], preferred_element_type=F32)

    @pl.when(i == nt)
    def _():
        epilogue(acc0_ref if (nt - 1) % 2 == 0 else acc1_ref)


def _matmul_row(a, w, x, gpost, gnext, mod, layer, next_layer, grp, *, gate_row, next_rows, tm, tk, name):
    m, kdim = a.shape
    d = w.shape[-1]
    tm, tk = min(tm, m), min(tk, kdim)
    nk, nt = kdim // tk, m // tm
    assert tm % nk == 0 and (tm // nk) % 16 == 0
    cur = lambda i: jnp.minimum(i, nt - 1)
    kcur = lambda i, k: jnp.where(i < nt, k, nk - 1)
    prev = lambda i: jnp.maximum(i - 1, 0)
    vec = pl.BlockSpec((None, 1, d), lambda i, k: (layer, 0, 0))
    vecn = pl.BlockSpec((None, 1, d), lambda i, k: (next_layer, 0, 0))
    out_specs = [pl.BlockSpec((tm, d), lambda i, k: (prev(i), 0))]
    out_shape = [jax.ShapeDtypeStruct((m, d), F32)]
    if next_rows is not None:
        out_specs.append(pl.BlockSpec((tm, d), lambda i, k: (prev(i), 0)))
        out_shape.append(jax.ShapeDtypeStruct((m, d), BF16))
    return pl.pallas_call(
        functools.partial(_mm_row_kernel, gate_row=gate_row, next_rows=next_rows, nk=nk, nt=nt),
        grid=(nt + 1, nk),
        in_specs=[
            pl.BlockSpec((tm, tk), lambda i, k: (cur(i), kcur(i, k))),
            pl.BlockSpec((None, tk, d), lambda i, k: (layer, kcur(i, k), 0)),
            pl.BlockSpec((tm, d), lambda i, k: (prev(i), 0)),
            vec,
            pl.BlockSpec((None, None, MOD_ROWS, d), lambda i, k: (layer, grp(prev(i) * tm), 0, 0)),
            vecn,
            pl.BlockSpec((None, None, MOD_ROWS, d), lambda i, k: (next_layer, grp(prev(i) * tm), 0, 0)),
        ],
        out_specs=out_specs,
        out_shape=out_shape,
        scratch_shapes=[pltpu.VMEM((tm, d), F32), pltpu.VMEM((tm, d), F32)],
        compiler_params=_params("arbitrary", "arbitrary"),
        name=name,
    )(a, w, x, gpost, mod, gnext, mod)


def _prep_a_kernel(p_ref, gq_ref, gk_ref, *rest, rope, cache):
    rest = list(rest)
    tabs = [rest.pop(0) for _ in range(3)] if rope else None
    q_ref, k_ref = rest[:2]
    scale = HEAD_DIM ** -0.5 * LOG2E
    for h in range(A_HEADS):
        y = _rms(p_ref[:, h * HEAD_DIM:(h + 1) * HEAD_DIM].astype(F32), gq_ref[...])
        if rope:
            y = _rope(y, tabs[0][...], tabs[1][...], tabs[2][...], HEAD_DIM // 4)
        q_ref[:, h * HEAD_DIM:(h + 1) * HEAD_DIM] = (y * scale).astype(BF16)
    for g in range(A_KV_HEADS):
        lo = OFF_AK + g * HEAD_DIM
        y = _rms(p_ref[:, lo:lo + HEAD_DIM].astype(F32), gk_ref[...])
        if cache:
            rest[2][:, g * HEAD_DIM:(g + 1) * HEAD_DIM] = y
        if rope:
            y = _rope(y, tabs[0][...], tabs[1][...], tabs[2][...], HEAD_DIM // 4)
        k_ref[:, g * HEAD_DIM:(g + 1) * HEAD_DIM] = y.astype(BF16)
    if cache:
        rest[3][...] = p_ref[:, OFF_AV:OFF_BQ].astype(F32)


def _prep_a(proj, gq, gk, layer, tabs, n_per, *, cache):
    m = proj.shape[0]
    tr = min(512, n_per)
    rope = tabs is not None
    width = OFF_BQ
    kvw = A_KV_HEADS * HEAD_DIM
    in_specs = [
        pl.BlockSpec((tr, width), lambda i: (i, 0)),
        pl.BlockSpec((None, 1, HEAD_DIM), lambda i: (layer, 0, 0)),
        pl.BlockSpec((None, 1, HEAD_DIM), lambda i: (layer, 0, 0)),
    ]
    args = [proj, gq, gk]
    if rope:
        nt = n_per // tr
        in_specs += [pl.BlockSpec((tr, LANE), lambda i: (i % nt, 0))] * 3
        args += list(tabs)
    out_specs = [pl.BlockSpec((tr, A_HEADS * HEAD_DIM), lambda i: (i, 0)),
                 pl.BlockSpec((tr, kvw), lambda i: (i, 0))]
    out_shape = [jax.ShapeDtypeStruct((m, A_HEADS * HEAD_DIM), BF16),
                 jax.ShapeDtypeStruct((m, kvw), BF16)]
    if cache:
        out_specs += [pl.BlockSpec((tr, kvw), lambda i: (i, 0))] * 2
        out_shape += [jax.ShapeDtypeStruct((m, kvw), F32)] * 2
    return pl.pallas_call(
        functools.partial(_prep_a_kernel, rope=rope, cache=cache),
        grid=(m // tr,),
        in_specs=in_specs,
        out_specs=out_specs,
        out_shape=out_shape,
        compiler_params=_params("arbitrary"),
        name="prep_a",
    )(*args)


def _prep_cq_kernel(p_ref, g_ref, w_ref, *rest, rope):
    rest = list(rest)
    tabs = [rest.pop(0) for _ in range(3)] if rope else None
    q_ref = rest[0]
    scale = (C_NOPE_DIM + C_ROPE_DIM) ** -0.5 * LOG2E
    y = _rms(p_ref[...].astype(F32), g_ref[...]).astype(BF16)
    z = jnp.dot(y, w_ref[...], preferred_element_type=F32)
    for h in range(C_HEADS):
        lo = h * C_QK_PAD
        q_ref[:, lo:lo + C_NOPE_DIM] = (z[:, lo:lo + C_NOPE_DIM] * scale).astype(BF16)
        r = z[:, lo + C_NOPE_DIM:lo + C_QK_PAD]
        if rope:
            r = _rope(r, tabs[0][...], tabs[1][...], tabs[2][...], C_ROPE_DIM // 4)
        q_ref[:, lo + C_NOPE_DIM:lo + C_QK_PAD] = (r * scale).astype(BF16)


def _prep_cq(proj, g, w_uq, layer, tabs, n_per):
    m = proj.shape[0]
    tr = min(512, n_per)
    rope = tabs is not None
    width = C_HEADS * C_QK_PAD
    in_specs = [
        pl.BlockSpec((tr, C_Q_RANK), lambda i: (i, OFF_CQ // C_Q_RANK)),
        pl.BlockSpec((None, 1, C_Q_RANK), lambda i: (layer, 0, 0)),
        pl.BlockSpec((None, C_Q_RANK, width), lambda i: (layer, 0, 0)),
    ]
    args = [proj, g, w_uq]
    if rope:
        nt = n_per // tr
        in_specs += [pl.BlockSpec((tr, LANE), lambda i: (i % nt, 0))] * 3
        args += list(tabs)
    return pl.pallas_call(
        functools.partial(_prep_cq_kernel, rope=rope),
        grid=(m // tr,),
        in_specs=in_specs,
        out_specs=pl.BlockSpec((tr, width), lambda i: (i, 0)),
        out_shape=jax.ShapeDtypeStruct((m, width), BF16),
        compiler_params=_params("arbitrary"),
        name="prep_cq",
    )(*args)


def _prep_ckv_kernel(ckv_ref, kr_ref, g_ref, w_ref, *rest, norm, rope, cache):
    rest = list(rest)
    tabs = [rest.pop(0) for _ in range(3)] if rope else None
    kc_ref, vc_ref = rest[:2]
    x = ckv_ref[...].astype(F32)
    if norm:
        x = _rms(x, g_ref[...])
    kr = kr_ref[...].astype(F32)
    if cache:
        rest[2][...] = x
        rest[3][...] = kr[:, :C_ROPE_DIM]
    if rope:
        kr = _rope(kr, tabs[0][...], tabs[1][...], tabs[2][...], C_ROPE_DIM // 4)
    krb = kr.astype(BF16)
    z = jnp.dot(x.astype(BF16), w_ref[...], preferred_element_type=F32)
    up = C_NOPE_DIM + C_V_DIM
    for h in range(C_HEADS):
        kc_ref[:, h * C_QK_PAD:h * C_QK_PAD + C_NOPE_DIM] = z[:, h * up:h * up + C_NOPE_DIM].astype(BF16)
        kc_ref[:, h * C_QK_PAD + C_NOPE_DIM:(h + 1) * C_QK_PAD] = krb
        vc_ref[:, h * C_V_DIM:(h + 1) * C_V_DIM] = z[:, h * up + C_NOPE_DIM:(h + 1) * up].astype(BF16)


def _prep_ckv(ckv_src, ckv_blk, kr_src, kr_blk, g, w_ukv, layer, tabs, n_per, *, norm, cache):
    m = ckv_src.shape[0]
    tr = min(512, n_per)
    rope = tabs is not None
    in_specs = [
        pl.BlockSpec((tr, C_KV_RANK), lambda i: (i, ckv_blk)),
        pl.BlockSpec((tr, LANE), lambda i: (i, kr_blk)),
        pl.BlockSpec((None, 1, C_KV_RANK), lambda i: (layer, 0, 0)),
        pl.BlockSpec((None, C_KV_RANK, C_HEADS * (C_NOPE_DIM + C_V_DIM)), lambda i: (layer, 0, 0)),
    ]
    args = [ckv_src, kr_src, g, w_ukv]
    if rope:
        nt = n_per // tr
        in_specs += [pl.BlockSpec((tr, LANE), lambda i: (i % nt, 0))] * 3
        args += list(tabs)
    out_specs = [pl.BlockSpec((tr, C_HEADS * C_QK_PAD), lambda i: (i, 0)),
                 pl.BlockSpec((tr, C_HEADS * C_V_DIM), lambda i: (i, 0))]
    out_shape = [jax.ShapeDtypeStruct((m, C_HEADS * C_QK_PAD), BF16),
                 jax.ShapeDtypeStruct((m, C_HEADS * C_V_DIM), BF16)]
    if cache:
        out_specs += [pl.BlockSpec((tr, C_KV_RANK), lambda i: (i, 0)),
                      pl.BlockSpec((tr, C_ROPE_DIM), lambda i: (i, 0))]
        out_shape += [jax.ShapeDtypeStruct((m, C_KV_RANK), F32),
                      jax.ShapeDtypeStruct((m, C_ROPE_DIM), F32)]
    return pl.pallas_call(
        functools.partial(_prep_ckv_kernel, norm=norm, rope=rope, cache=cache),
        grid=(m // tr,),
        in_specs=in_specs,
        out_specs=out_specs,
        out_shape=out_shape,
        compiler_params=_params("arbitrary"),
        name="prep_ckv",
    )(*args)


def _attn_kernel(q_ref, k_ref, v_ref, *rest, gps, rep, dqk, dv, tq, tk, ctx):
    if ctx:
        k2_ref, v2_ref, o_ref, acc_ref, s0_ref, s1_ref = rest
    else:
        o_ref, acc_ref, s0_ref, s1_ref = rest
    s_refs = (s0_ref, s1_ref)

    items = []
    for g in range(gps):
        chunks = [(k_ref, v_ref, c * tk, tk) for c in range(k_ref.shape[0] // tk)]
        if ctx:
            nctx = k2_ref.shape[0]
            chunks += [(k2_ref, v2_ref, lo, min(tk, nctx - lo)) for lo in range(0, nctx, tk)]
        items += [(g, c == 0, c == len(chunks) - 1) + ch for c, ch in enumerate(chunks)]

    def queries(g):
        cols = [q_ref[:, (g * rep + r) * dqk:(g * rep + r + 1) * dqk] for r in range(rep)]
        return cols[0] if rep == 1 else jnp.concatenate(cols, axis=0)

    def scores(t):
        g, _, _, kr, _, lo, size = items[t]
        s_refs[t % 2][:, :size] = lax.dot_general(
            queries(g), kr[lo:lo + size, g * dqk:(g + 1) * dqk], (((1,), (1,)), ((), ())),
            preferred_element_type=F32)

    scores(0)
    m = None
    for t, (g, first, last, _, vr, lo, size) in enumerate(items):
        if t + 1 < len(items):
            scores(t + 1)
        s = s_refs[t % 2][:, :size]
        m_blk = jnp.max(s, axis=-1, keepdims=True)
        m_new = m_blk if first else jnp.maximum(m, m_blk)
        p = jnp.exp2(s - m_new).astype(BF16)
        v_ext = jnp.concatenate([vr[lo:lo + size, g * dv:(g + 1) * dv], jnp.ones((size, dv), BF16)], axis=1)
        pv = jnp.dot(p, v_ext, preferred_element_type=F32)
        if first:
            acc_ref[g] = pv
        else:
            acc_ref[g] = jnp.exp2(m - m_new) * acc_ref[g] + pv
        m = m_new
        if last:
            o = acc_ref[g, :, :dv] / acc_ref[g, :, dv:]
            for r in range(rep):
                o_ref[:, (g * rep + r) * dv:(g * rep + r + 1) * dv] = o[r * tq:(r + 1) * tq].astype(o_ref.dtype)


def _attention(q, k, kblk0, v, vblk0, ctx_kv, batch, n, *, groups, gps, rep, dqk, dv, rows, name):
    tq = min(rows // rep, n)
    tk = min(512, n)
    nq = n // tq
    assert groups % gps == 0 and kblk0 % gps == 0 and vblk0 % gps == 0 and (ctx_kv is None or gps == 1)
    in_specs = [
        pl.BlockSpec((tq, gps * rep * dqk), lambda b, g, i: (b * nq + i, g)),
        pl.BlockSpec((n, gps * dqk), lambda b, g, i: (b, kblk0 // gps + g)),
        pl.BlockSpec((n, gps * dv), lambda b, g, i: (b, vblk0 // gps + g)),
    ]
    args = [q, k, v]
    if ctx_kv is not None:
        k2, v2 = ctx_kv
        nc = k2.shape[0] // batch
        in_specs += [pl.BlockSpec((nc, dqk), lambda b, g, i: (b, g)),
                     pl.BlockSpec((nc, dv), lambda b, g, i: (b, g))]
        args += [k2, v2]
    return pl.pallas_call(
        functools.partial(_attn_kernel, gps=gps, rep=rep, dqk=dqk, dv=dv, tq=tq, tk=tk, ctx=ctx_kv is not None),
        grid=(batch, groups // gps, nq),
        in_specs=in_specs,
        out_specs=pl.BlockSpec((tq, gps * rep * dv), lambda b, g, i: (b * nq + i, g)),
        out_shape=jax.ShapeDtypeStruct((batch * n, groups * rep * dv), BF16),
        scratch_shapes=[pltpu.VMEM((gps, rep * tq, 2 * dv), F32),
                        pltpu.VMEM((rep * tq, tk), F32),
                        pltpu.VMEM((rep * tq, tk), F32)],
        compiler_params=_params("arbitrary", "arbitrary", "arbitrary"),
        name=name,
    )(*args)


def _ret_head(lgf, lgb, io, tabs, nc, kvf_ref, kvb_ref, state_ref):
    c_len = B_CHUNK
    rope = tabs is not None
    ii = lax.broadcasted_iota(jnp.int32, (c_len, c_len), 0)
    jj = lax.broadcasted_iota(jnp.int32, (c_len, c_len), 1)
    diff = (ii - jj).astype(F32)
    dmat = (jnp.where(diff >= 0, jnp.exp(jnp.maximum(diff, 0.0) * lgf), 0.0)
            + jnp.where(diff <= 0, jnp.exp(jnp.maximum(-diff, 0.0) * lgb), 0.0))
    ri = lax.broadcasted_iota(jnp.int32, (c_len, 1), 0).astype(F32)
    qdec_f = jnp.exp((ri + 1.0) * lgf)
    qdec_b = jnp.exp((c_len - ri) * lgb)
    kdec_f = jnp.exp((c_len - 1.0 - ri) * lgf)
    kdec_b = jnp.exp(ri * lgb)
    one = jnp.ones((1, 1), F32)
    cdec_f = jnp.exp(one * (c_len * lgf))
    cdec_b = jnp.exp(one * (c_len * lgb))
    kscale = B_QK_DIM ** -0.5

    def roped(load, lo):
        x = load(lo).astype(F32)
        if rope:
            t = [tab[pl.ds(lo, c_len), :] for tab in tabs]
            x = _rope(x, t[0], t[1], t[2], B_QK_DIM // 4)
        return x

    def increments(c, carry):
        lo = pl.multiple_of(c * c_len, c_len)
        kc = roped(io["k"], lo) * kscale
        kk = jnp.concatenate([kc * kdec_f, kc * kdec_b], axis=1).T.astype(BF16)
        kv = jnp.dot(kk, io["v"](lo), preferred_element_type=F32)
        kvf_ref[c] = kv[:B_QK_DIM]
        kvb_ref[c] = kv[B_QK_DIM:]
        return carry

    lax.fori_loop(0, nc, increments, 0, unroll=_unroll(nc, 8))

    def scan_f(c, s):
        state_ref[c, :B_QK_DIM, :] = s.astype(BF16)
        return cdec_f * s + kvf_ref[c]

    def scan_b(t, s):
        c = nc - 1 - t
        state_ref[c, B_QK_DIM:, :] = s.astype(BF16)
        return cdec_b * s + kvb_ref[c]

    s_f = lax.fori_loop(0, nc, scan_f, io["s0f"], unroll=_unroll(nc, 2))
    s_b = lax.fori_loop(0, nc, scan_b, io["s0b"], unroll=_unroll(nc, 2))
    io["store_state"](s_f, s_b)

    def outputs(c, carry):
        lo = pl.multiple_of(c * c_len, c_len)
        qc = roped(io["q"], lo)
        kc = (roped(io["k"], lo) * kscale).astype(BF16)
        s = lax.dot_general(qc.astype(BF16), kc, (((1,), (1,)), ((), ())), preferred_element_type=F32) * dmat
        o = jnp.dot(s.astype(BF16), io["v"](lo), preferred_element_type=F32)
        qq = jnp.concatenate([qc * qdec_f, qc * qdec_b], axis=1).astype(BF16)
        o = o + jnp.dot(qq, state_ref[c], preferred_element_type=F32)
        gate = io["gate"](lo).astype(F32)
        io["store_o"](lo, (gate * jax.nn.sigmoid(gate) * _rms(o, io["gn"])).astype(BF16))
        return carry

    lax.fori_loop(0, nc, outputs, 0, unroll=_unroll(nc, 4))


def _ret_kernel(lg_ref, *refs, n, rope, emit_state, all_heads):
    refs = list(refs)
    data = [refs.pop(0) for _ in range(2 if all_heads else 4)]
    gn_ref, s0f_ref, s0b_ref = refs.pop(0), refs.pop(0), refs.pop(0)
    tabs = [refs.pop(0) for _ in range(3)] if rope else None
    o_ref = refs.pop(0)
    sfo_ref, sbo_ref = (refs.pop(0), refs.pop(0)) if emit_state else (None, None)
    kvf_ref, kvb_ref, state_ref = refs
    nc = n // B_CHUNK
    rows = lambda lo: pl.ds(lo, B_CHUNK)
    qw, vw = B_HEADS * B_QK_DIM, B_V_DIM

    for hh in range(B_HEADS if all_heads else 1):
        if all_heads:
            h = hh
            blk1, blk2 = data
            v_src = (blk1, 2 * qw + hh * vw) if hh < 2 else (blk2, (hh - 2) * vw)
            col = lambda ref, c0, w: (lambda lo: ref[rows(lo), c0:c0 + w])
            io = dict(
                q=col(blk1, hh * B_QK_DIM, B_QK_DIM), k=col(blk1, qw + hh * B_QK_DIM, B_QK_DIM),
                v=col(v_src[0], v_src[1], vw), gate=col(blk2, 2 * vw + hh * vw, vw),
                gn=gn_ref[:, hh * vw:(hh + 1) * vw], s0f=s0f_ref[hh], s0b=s0b_ref[hh])

            def store_o(lo, val, hh=hh):
                o_ref[rows(lo), hh * vw:(hh + 1) * vw] = val

            def store_state(sf, sb, hh=hh):
                if emit_state:
                    sfo_ref[hh] = sf
                    sbo_ref[hh] = sb
        else:
            h = pl.program_id(1)
            whole = lambda ref: (lambda lo: ref[rows(lo), :])
            io = dict(q=whole(data[0]), k=whole(data[1]), v=whole(data[2]), gate=whole(data[3]),
                      gn=gn_ref[...], s0f=s0f_ref[...], s0b=s0b_ref[...])

            def store_o(lo, val):
                o_ref[rows(lo), :] = val

            def store_state(sf, sb):
                if emit_state:
                    sfo_ref[...] = sf
                    sbo_ref[...] = sb
        io.update(store_o=store_o, store_state=store_state)
        _ret_head(lg_ref[0, h], lg_ref[1, h], io, tabs, nc, kvf_ref.at[hh], kvb_ref.at[hh], state_ref.at[hh])


def _retention(proj, lg, gn, s0f, s0b, layer, tabs, batch, n, *, emit_state, all_heads):
    rope = tabs is not None
    if all_heads:
        wide = (OFF_CQ - OFF_BQ) // 2
        assert OFF_BQ % wide == 0 and wide == 2 * B_HEADS * B_QK_DIM + 2 * B_V_DIM
        st_spec = pl.BlockSpec((None, B_HEADS, B_QK_DIM, B_V_DIM), lambda b, h, lg_: (b, 0, 0, 0))
        in_specs = [
            pl.BlockSpec((n, wide), lambda b, h, lg_: (b, OFF_BQ // wide)),
            pl.BlockSpec((n, wide), lambda b, h, lg_: (b, OFF_BQ // wide + 1)),
            pl.BlockSpec((None, 1, B_HEADS * B_V_DIM), lambda b, h, lg_: (layer, 0, 0)),
        ]
        args = [proj, proj, gn]
        out_specs = [pl.BlockSpec((n, B_HEADS * B_V_DIM), lambda b, h, lg_: (b, 0))]
        heads_per_step = B_HEADS
    else:
        qb, kb = OFF_BQ // B_QK_DIM, OFF_BK // B_QK_DIM
        vb, gb = OFF_BV // B_V_DIM, OFF_BG // B_V_DIM
        st_spec = pl.BlockSpec((None, None, B_QK_DIM, B_V_DIM), lambda b, h, lg_: (b, h, 0, 0))
        in_specs = [
            pl.BlockSpec((n, B_QK_DIM), lambda b, h, lg_: (b, qb + h)),
            pl.BlockSpec((n, B_QK_DIM), lambda b, h, lg_: (b, kb + h)),
            pl.BlockSpec((n, B_V_DIM), lambda b, h, lg_: (b, vb + h)),
            pl.BlockSpec((n, B_V_DIM), lambda b, h, lg_: (b, gb + h)),
            pl.BlockSpec((None, 1, B_V_DIM), lambda b, h, lg_: (layer, 0, h)),
        ]
        args = [proj, proj, proj, proj, gn]
        out_specs = [pl.BlockSpec((n, B_V_DIM), lambda b, h, lg_: (b, h))]
        heads_per_step = 1
    in_specs += [st_spec, st_spec]
    args += [s0f, s0b]
    if rope:
        in_specs += [pl.BlockSpec((n, LANE), lambda b, h, lg_: (0, 0))] * 3
        args += list(tabs)
    out_shape = [jax.ShapeDtypeStruct((batch * n, B_HEADS * B_V_DIM), BF16)]
    if emit_state:
        out_specs += [st_spec, st_spec]
        out_shape += [jax.ShapeDtypeStruct((batch, B_HEADS, B_QK_DIM, B_V_DIM), F32)] * 2
    nc = n // B_CHUNK
    return pl.pallas_call(
        functools.partial(_ret_kernel, n=n, rope=rope, emit_state=emit_state, all_heads=all_heads),
        grid_spec=pltpu.PrefetchScalarGridSpec(
            num_scalar_prefetch=1,
            grid=(batch, B_HEADS // heads_per_step),
            in_specs=in_specs,
            out_specs=out_specs,
            scratch_shapes=[pltpu.VMEM((heads_per_step, nc, B_QK_DIM, B_V_DIM), F32),
                            pltpu.VMEM((heads_per_step, nc, B_QK_DIM, B_V_DIM), F32),
                            pltpu.VMEM((heads_per_step, nc, 2 * B_QK_DIM, B_V_DIM), BF16)],
        ),
        out_shape=out_shape,
        compiler_params=_params("arbitrary", "arbitrary"),
        name="retention",
    )(lg, *args)


def _merge_kernel(oa_ref, ob_ref, oc_ref, wa_ref, wb_ref, wc_ref, ga_ref, gb_ref, gc_ref, o_ref):
    def branch(o, w, g):
        return jax.nn.sigmoid(g[...].astype(F32)) * jnp.dot(o[...], w[...], preferred_element_type=F32)

    o_ref[...] = (branch(oa_ref, wa_ref, ga_ref) + branch(ob_ref, wb_ref, gb_ref)
                  + branch(oc_ref, wc_ref, gc_ref)).astype(BF16)


def _merge(oa, ob, oc, wa, wb, wc, gates, layer, d):
    m = oa.shape[0]
    tm = min(1024, m)
    tn = min(1024, d)
    nb = d // tn
    o_spec = lambda o: pl.BlockSpec((tm, o.shape[1]), lambda i, j: (i, 0))
    w_spec = lambda w: pl.BlockSpec((None, w.shape[1], tn), lambda i, j: (layer, 0, j))
    g_spec = lambda br: pl.BlockSpec((tm, tn), lambda i, j: (i, br * nb + j))
    return pl.pallas_call(
        _merge_kernel,
        grid=(m // tm, nb),
        in_specs=[o_spec(oa), o_spec(ob), o_spec(oc), w_spec(wa), w_spec(wb), w_spec(wc),
                  g_spec(0), g_spec(1), g_spec(2)],
        out_specs=pl.BlockSpec((tm, tn), lambda i, j: (i, j)),
        out_shape=jax.ShapeDtypeStruct((m, d), BF16),
        compiler_params=_params("arbitrary", "arbitrary"),
        name="merge",
    )(oa, ob, oc, wa, wb, wc, gates, gates, gates)


def _rope_tables(n_tokens, dim, pad):
    pos = jnp.arange(n_tokens, dtype=jnp.int32)
    row = (pos // GRID_W).astype(F32)[:, None]
    col = (pos % GRID_W).astype(F32)[:, None]
    quarter = dim // 4
    inv = ROPE_THETA ** (-jnp.arange(quarter, dtype=F32) / quarter)[None, :]
    ang = jnp.concatenate([row * inv, row * inv, col * inv, col * inv], axis=1)
    cos, sin = jnp.cos(ang), jnp.sin(ang)
    first = (jnp.arange(dim) // quarter) % 2 == 0
    sin_lo = jnp.where(first[None, :], -sin, 0.0)
    sin_hi = jnp.where(first[None, :], 0.0, sin)
    if pad:
        z = jnp.zeros_like(cos)
        tabs = [jnp.concatenate([t, z], axis=1) for t in (cos, sin_lo, sin_hi)]
    else:
        tabs = [jnp.tile(t, (1, LANE // dim)) for t in (cos, sin_lo, sin_hi)]
    return tuple(tabs)


def kernel(x_prompt, x_sample, c, cache_attn_k, cache_attn_v, state_ret_fwd, state_ret_bwd, cache_mla_ckv, cache_mla_krope, c_ctx, w_mod, b_mod, g_pre_mix, g_post_mix, g_pre_mlp, g_post_mlp, w_in, attn_q_norm, attn_k_norm, ret_decay_fwd, ret_decay_bwd, ret_gn, mla_q_norm, mla_kv_norm, w_mla_uq, w_mla_ukv, w_branch_a, w_branch_b, w_branch_c, w_out, w_mlp_up, w_mlp_down):
    bp, n_p, d = x_prompt.shape
    bs, n_s, _ = x_sample.shape
    depth = w_in.shape[0]
    past = cache_attn_k.shape[2]
    assert n_p % B_CHUNK == 0 and n_s % B_CHUNK == 0

    w_main, w_gate = _split_w_in(w_in)
    w_uq = jnp.pad(w_mla_uq.reshape(depth, C_Q_RANK, C_HEADS, C_NOPE_DIM + C_ROPE_DIM),
                   ((0, 0), (0, 0), (0, 0), (0, C_QK_PAD - C_NOPE_DIM - C_ROPE_DIM)))
    w_uq = w_uq.reshape(depth, C_Q_RANK, C_HEADS * C_QK_PAD).astype(BF16)
    w_ukv = w_mla_ukv.astype(BF16)
    wa, wb, wc = w_branch_a.astype(BF16), w_branch_b.astype(BF16), w_branch_c.astype(BF16)
    w_o, w_up, w_dn = w_out.astype(BF16), w_mlp_up.astype(BF16), w_mlp_down.astype(BF16)
    vec = lambda g: g.reshape(depth, 1, g.shape[-1])
    g_pre_mix, g_post_mix, g_pre_mlp, g_post_mlp = map(vec, (g_pre_mix, g_post_mix, g_pre_mlp, g_post_mlp))
    gqn, gkn, gcq, gckv, gret = map(vec, (attn_q_norm, attn_k_norm, mla_q_norm, mla_kv_norm, ret_gn))
    lg = jnp.stack([jax.nn.log_sigmoid(ret_decay_fwd.astype(F32)),
                    jax.nn.log_sigmoid(ret_decay_bwd.astype(F32))], axis=1)

    rows = -(-(1 + bs) // 8) * 8
    cvec = jnp.zeros((rows, d), F32).at[0].set(c_ctx).at[1:1 + bs].set(c)
    mod = _modulation(cvec, w_mod, b_mod).reshape(depth, rows, 6, d)
    mod = jnp.pad(mod, ((0, 0), (0, 0), (0, MOD_ROWS - 6), (0, 0)))

    tabs128 = _rope_tables(n_s, HEAD_DIM, pad=False)
    tabs64 = _rope_tables(n_s, C_ROPE_DIM, pad=False)
    tabs64p = _rope_tables(n_s, C_ROPE_DIM, pad=True)

    ctx_k = cache_attn_k.astype(BF16).reshape(bs, depth, past, A_KV_HEADS * HEAD_DIM)
    ctx_v = cache_attn_v.astype(BF16).reshape(bs, depth, past, A_KV_HEADS * HEAD_DIM)
    ctx_kr = jnp.pad(cache_mla_krope, ((0, 0), (0, 0), (0, 0), (0, LANE - C_ROPE_DIM)))
    zero_state = jnp.zeros((bp, B_HEADS, B_QK_DIM, B_V_DIM), F32)

    streams = {
        "p": dict(x=x_prompt.reshape(bp * n_p, d), batch=bp, n=n_p, grp=lambda r: 0, latent=False),
        "s": dict(x=x_sample.reshape(bs * n_s, d), batch=bs, n=n_s, grp=lambda r: 1 + r // n_s, latent=True),
    }
    for st in streams.values():
        st["h"] = _prenorm(st["x"], g_pre_mix, mod, 0, st["grp"])
    caches = [[] for _ in range(6)]

    for l in range(depth):
        for st in streams.values():
            batch, n, grp, latent = st["batch"], st["n"], st["grp"], st["latent"]
            proj = _matmul(st["h"], w_main, l, tm=2048, tn=512, w_transposed=True, name="w_in")
            gates = _matmul(st["h"], w_gate, l, tm=2048, tn=512, w_transposed=True, name="w_gates")
            t128, t64, t64p = (tabs128, tabs64, tabs64p) if latent else (None, None, None)

            pa = _prep_a(proj, gqn, gkn, l, t128, n, cache=not latent)
            qa, ka = pa[0], pa[1]
            ctx_a = (ctx_k[:, l].reshape(bs * past, -1), ctx_v[:, l].reshape(bs * past, -1)) if latent else None
            o_a = _attention(qa, ka, 0, proj, OFF_AV // HEAD_DIM, ctx_a, batch, n,
                             groups=A_KV_HEADS, gps=1 if latent else A_KV_HEADS, rep=A_HEADS // A_KV_HEADS,
                             dqk=HEAD_DIM, dv=HEAD_DIM, rows=1024, name="attn_a")

            if latent:
                s0f, s0b = state_ret_fwd[:, l], state_ret_bwd[:, l]
            else:
                s0f = s0b = zero_state
            rb = _retention(proj, lg[l], gret, s0f, s0b, l, t128, batch, n,
                            emit_state=not latent, all_heads=not latent)
            o_b = rb[0]

            qc = _prep_cq(proj, gcq, w_uq, l, t64p, n)
            pc = _prep_ckv(proj, OFF_CKV // C_KV_RANK, proj, OFF_CKR // LANE, gckv, w_ukv, l, t64, n,
                           norm=True, cache=not latent)
            kc, vc = pc[0], pc[1]
            ctx_c = None
            if latent:
                ctx_c = _prep_ckv(cache_mla_ckv[:, l].reshape(bs * past, C_KV_RANK), 0,
                                  ctx_kr[:, l].reshape(bs * past, LANE), 0, gckv, w_ukv, l, None, past,
                                  norm=False, cache=False)
            o_c = _attention(qc, kc, 0, vc, 0, ctx_c, batch, n,
                             groups=C_HEADS, gps=1 if latent else C_HEADS, rep=1,
                             dqk=C_QK_PAD, dv=C_V_DIM, rows=1024, name="attn_c")

            if not latent:
                for dst, val in zip(caches, (pa[2], pa[3], rb[1], rb[2], pc[2], pc[3])):
                    dst.append(val)

            merged = _merge(o_a, o_b, o_c, wa, wb, wc, gates, l, d)
            x, h2 = _matmul_row(merged, w_o, st["x"], g_post_mix, g_pre_mlp, mod, l, l, grp,
                                gate_row=2, next_rows=(3, 4), tm=512, tk=2048, name="w_out")
            hid = _matmul(h2, w_up, l, tm=2048, tn=512, relu2=True, name="mlp_up")
            last = l == depth - 1
            res = _matmul_row(hid, w_dn, x, g_post_mlp, g_pre_mix, mod, l, min(l + 1, depth - 1), grp,
                              gate_row=5, next_rows=None if last else (0, 1), tm=512, tk=2048, name="mlp_down")
            st["x"] = res[0]
            st["h"] = None if last else res[1]

    y_p = streams["p"]["x"].reshape(bp, n_p, d)
    y_s = streams["s"]["x"].reshape(bs, n_s, d)
    nk, nv, sf, sb, ckv, kr = caches
    new_attn_k = jnp.stack(nk, axis=1).reshape(bp, n_p, depth, A_KV_HEADS, HEAD_DIM).transpose(0, 2, 1, 3, 4)
    new_attn_v = jnp.stack(nv, axis=1).reshape(bp, n_p, depth, A_KV_HEADS, HEAD_DIM).transpose(0, 2, 1, 3, 4)
    new_ret_fwd = jnp.stack(sf, axis=1)
    new_ret_bwd = jnp.stack(sb, axis=1)
    new_mla_ckv = jnp.stack(ckv, axis=1).reshape(bp, n_p, depth, C_KV_RANK).transpose(0, 2, 1, 3)
    new_mla_krope = jnp.stack(kr, axis=1).reshape(bp, n_p, depth, C_ROPE_DIM).transpose(0, 2, 1, 3)
    return (y_p, y_s, new_attn_k, new_attn_v, new_ret_fwd, new_ret_bwd, new_mla_ckv, new_mla_krope)
```

```python
import functools

import jax
import jax.numpy as jnp
from jax import lax
from jax.experimental import pallas as pl
from jax.experimental.pallas import tpu as pltpu

F32 = jnp.float32
BF16 = jnp.bfloat16

EPS = 1e-6
LOG2E = 1.4426950408889634
ROPE_THETA = 10000.0
GRID_W = 64
LANE = 128
HEAD_DIM = 128
A_HEADS = 8
A_KV_HEADS = 2
B_HEADS = 4
B_QK_DIM = 128
B_V_DIM = 256
B_CHUNK = 128
C_HEADS = 8
C_Q_RANK = 512
C_KV_RANK = 256
C_NOPE_DIM = 128
C_ROPE_DIM = 64
C_V_DIM = 128
C_QK_PAD = 256

OFF_AQ = 0
OFF_AK = OFF_AQ + A_HEADS * HEAD_DIM
OFF_AV = OFF_AK + A_KV_HEADS * HEAD_DIM
OFF_BQ = OFF_AV + A_KV_HEADS * HEAD_DIM
OFF_BK = OFF_BQ + B_HEADS * B_QK_DIM
OFF_BV = OFF_BK + B_HEADS * B_QK_DIM
OFF_BG = OFF_BV + B_HEADS * B_V_DIM
OFF_CQ = OFF_BG + B_HEADS * B_V_DIM
OFF_CKV = OFF_CQ + C_Q_RANK
OFF_CKR = OFF_CKV + C_KV_RANK
IN_MAIN = OFF_CKR + C_ROPE_DIM
MAIN_W = 5632
MOD_ROWS = 8

VMEM_LIMIT = 56 * 1024 * 1024


def _params(*sem):
    return pltpu.CompilerParams(dimension_semantics=sem, vmem_limit_bytes=VMEM_LIMIT)


def _rms(x, g):
    return x * lax.rsqrt(jnp.mean(x * x, axis=-1, keepdims=True) + EPS) * g


def _unroll(trips, want):
    while trips % want:
        want //= 2
    return want


def _rope(x, cos, sin_lo, sin_hi, shift):
    return (x * cos + pltpu.roll(x, LANE - shift, 1) * sin_lo + pltpu.roll(x, shift, 1) * sin_hi)


def _mod_kernel(c_ref, w_ref, b_ref, o_ref):
    c = c_ref[...]
    a = (c * jax.nn.sigmoid(c)).astype(BF16)
    o_ref[...] = jnp.dot(a, w_ref[...].astype(BF16), preferred_element_type=F32) + b_ref[...]


def _modulation(cvec, w_mod, b_mod):
    depth, d, n = w_mod.shape
    rows = cvec.shape[0]
    tn = min(1024, n)
    return pl.pallas_call(
        _mod_kernel,
        grid=(depth, n // tn),
        in_specs=[
            pl.BlockSpec((rows, d), lambda l, j: (0, 0)),
            pl.BlockSpec((None, d, tn), lambda l, j: (l, 0, j)),
            pl.BlockSpec((None, 1, tn), lambda l, j: (l, 0, j)),
        ],
        out_specs=pl.BlockSpec((None, rows, tn), lambda l, j: (l, 0, j)),
        out_shape=jax.ShapeDtypeStruct((depth, rows, n), F32),
        compiler_params=_params("arbitrary", "arbitrary"),
        name="modulation",
    )(cvec, w_mod, b_mod.reshape(depth, 1, n))


def _cast_kernel(w_ref, o_ref):
    o_ref[...] = w_ref[0].astype(BF16)


def _cast_rows(w_t, row0, n_rows, name):
    depth, _, d = w_t.shape
    tr = 512
    assert n_rows % tr == 0
    return pl.pallas_call(
        _cast_kernel,
        grid=(depth, n_rows // tr),
        in_specs=[pl.BlockSpec((pl.Element(1), pl.Element(tr), pl.Element(d)),
                               lambda l, i: (l, pl.multiple_of(row0 + i * tr, 64), 0))],
        out_specs=pl.BlockSpec((None, tr, d), lambda l, i: (l, i, 0)),
        out_shape=jax.ShapeDtypeStruct((depth, n_rows, d), BF16),
        compiler_params=_params("arbitrary", "arbitrary"),
        name=name,
    )(w_t)


def _split_w_in(w_in):
    width = w_in.shape[-1]
    w_t = jnp.transpose(w_in, (0, 2, 1))
    return (_cast_rows(w_t, 0, MAIN_W, "cast_w_main"),
            _cast_rows(w_t, IN_MAIN, width - IN_MAIN, "cast_w_gate"))


def _prenorm_kernel(x_ref, g_ref, mod_ref, h_ref):
    y = _rms(x_ref[...], g_ref[...])
    h_ref[...] = (y * (1.0 + mod_ref[1:2, :]) + mod_ref[0:1, :]).astype(BF16)


def _prenorm(x, g, mod, layer, grp):
    m, d = x.shape
    tm = min(512, m)
    return pl.pallas_call(
        _prenorm_kernel,
        grid=(m // tm,),
        in_specs=[
            pl.BlockSpec((tm, d), lambda i: (i, 0)),
            pl.BlockSpec((None, 1, d), lambda i: (layer, 0, 0)),
            pl.BlockSpec((None, None, MOD_ROWS, d), lambda i: (layer, grp(i * tm), 0, 0)),
        ],
        out_specs=pl.BlockSpec((tm, d), lambda i: (i, 0)),
        out_shape=jax.ShapeDtypeStruct((m, d), BF16),
        compiler_params=_params("arbitrary"),
        name="prenorm",
    )(x, g, mod)


def _mm_kernel(a_ref, w_ref, o_ref, *, relu2, w_transposed):
    contract = (((1,), (1 if w_transposed else 0,)), ((), ()))
    y = lax.dot_general(a_ref[...], w_ref[...], contract, preferred_element_type=F32)
    if relu2:
        y = jnp.square(jnp.maximum(y, 0.0))
    o_ref[...] = y.astype(o_ref.dtype)


def _matmul(a, w, layer, *, tm, tn, relu2=False, w_transposed=False, name):
    m, k = a.shape
    n = w.shape[1] if w_transposed else w.shape[2]
    tm, tn = min(tm, m), min(tn, n)
    if w_transposed:
        w_spec = pl.BlockSpec((None, tn, k), lambda i, j: (layer, j, 0))
    else:
        w_spec = pl.BlockSpec((None, k, tn), lambda i, j: (layer, 0, j))
    return pl.pallas_call(
        functools.partial(_mm_kernel, relu2=relu2, w_transposed=w_transposed),
        grid=(m // tm, n // tn),
        in_specs=[
            pl.BlockSpec((tm, k), lambda i, j: (i, 0)),
            w_spec,
        ],
        out_specs=pl.BlockSpec((tm, tn), lambda i, j: (i, j)),
        out_shape=jax.ShapeDtypeStruct((m, n), BF16),
        compiler_params=_params("arbitrary", "arbitrary"),
        name=name,
    )(a, w)


def _mm_row_kernel(a_ref, w_ref, x_ref, gpost_ref, mod_ref, gnext_ref, modn_ref, *rest,
                   gate_row, next_rows, nk, nt):
    if next_rows is None:
        xo_ref, acc0_ref, acc1_ref = rest
    else:
        xo_ref, ho_ref, acc0_ref, acc1_ref = rest
    i = pl.program_id(0)
    k = pl.program_id(1)
    tm, d = acc0_ref.shape
    sub = tm // nk

    @pl.when((i == 0) & (k == 0))
    def _():
        acc0_ref[...] = jnp.zeros_like(acc0_ref)
        acc1_ref[...] = jnp.zeros_like(acc1_ref)

    def epilogue(acc_ref):
        rows = pl.ds(pl.multiple_of(k * sub, sub), sub)
        f = acc_ref[rows, :]
        acc_ref[rows, :] = jnp.zeros((sub, d), F32)
        xn = x_ref[rows, :] + _rms(f, gpost_ref[...] * mod_ref[gate_row:gate_row + 1, :])
        xo_ref[rows, :] = xn
        if next_rows is not None:
            shift_row, scale_row = next_rows
            y = _rms(xn, gnext_ref[...] * (1.0 + modn_ref[scale_row:scale_row + 1, :]))
            ho_ref[rows, :] = (y + modn_ref[shift_row:shift_row + 1, :]).astype(BF16)

    for parity, (acc_mm, acc_ep) in enumerate(((acc0_ref, acc1_ref), (acc1_ref, acc0_ref))):
        @pl.when((i < nt) & (i % 2 == parity))
        def _():
            epilogue(acc_ep)
            acc_mm[...] += jnp.dot(a_ref[...], w_ref[...], preferred_element_type=F32)

    @pl.when(i == nt)
    def _():
        epilogue(acc0_ref if (nt - 1) % 2 == 0 else acc1_ref)


def _matmul_row(a, w, x, gpost, gnext, mod, layer, next_layer, grp, *, gate_row, next_rows, tm, tk, name):
    m, kdim = a.shape
    d = w.shape[-1]
    tm, tk = min(tm, m), min(tk, kdim)
    nk, nt = kdim // tk, m // tm
    assert tm % nk == 0 and (tm // nk) % 16 == 0
    cur = lambda i: jnp.minimum(i, nt - 1)
    kcur = lambda i, k: jnp.where(i < nt, k, nk - 1)
    prev = lambda i: jnp.maximum(i - 1, 0)
    vec = pl.BlockSpec((None, 1, d), lambda i, k: (layer, 0, 0))
    vecn = pl.BlockSpec((None, 1, d), lambda i, k: (next_layer, 0, 0))
    out_specs = [pl.BlockSpec((tm, d), lambda i, k: (prev(i), 0))]
    out_shape = [jax.ShapeDtypeStruct((m, d), F32)]
    if next_rows is not None:
        out_specs.append(pl.BlockSpec((tm, d), lambda i, k: (prev(i), 0)))
        out_shape.append(jax.ShapeDtypeStruct((m, d), BF16))
    return pl.pallas_call(
        functools.partial(_mm_row_kernel, gate_row=gate_row, next_rows=next_rows, nk=nk, nt=nt),
        grid=(nt + 1, nk),
        in_specs=[
            pl.BlockSpec((tm, tk), lambda i, k: (cur(i), kcur(i, k))),
            pl.BlockSpec((None, tk, d), lambda i, k: (layer, kcur(i, k), 0)),
            pl.BlockSpec((tm, d), lambda i, k: (prev(i), 0)),
            vec,
            pl.BlockSpec((None, None, MOD_ROWS, d), lambda i, k: (layer, grp(prev(i) * tm), 0, 0)),
            vecn,
            pl.BlockSpec((None, None, MOD_ROWS, d), lambda i, k: (next_layer, grp(prev(i) * tm), 0, 0)),
        ],
        out_specs=out_specs,
        out_shape=out_shape,
        scratch_shapes=[pltpu.VMEM((tm, d), F32), pltpu.VMEM((tm, d), F32)],
        compiler_params=_params("arbitrary", "arbitrary"),
        name=name,
    )(a, w, x, gpost, mod, gnext, mod)


def _prep_a_kernel(p_ref, gq_ref, gk_ref, *rest, rope, cache):
    rest = list(rest)
    tabs = [rest.pop(0) for _ in range(3)] if rope else None
    q_ref, k_ref = rest[:2]
    scale = HEAD_DIM ** -0.5 * LOG2E
    for h in range(A_HEADS):
        y = _rms(p_ref[:, h * HEAD_DIM:(h + 1) * HEAD_DIM].astype(F32), gq_ref[...])
        if rope:
            y = _rope(y, tabs[0][...], tabs[1][...], tabs[2][...], HEAD_DIM // 4)
        q_ref[:, h * HEAD_DIM:(h + 1) * HEAD_DIM] = (y * scale).astype(BF16)
    for g in range(A_KV_HEADS):
        lo = OFF_AK + g * HEAD_DIM
        y = _rms(p_ref[:, lo:lo + HEAD_DIM].astype(F32), gk_ref[...])
        if cache:
            rest[2][:, g * HEAD_DIM:(g + 1) * HEAD_DIM] = y
        if rope:
            y = _rope(y, tabs[0][...], tabs[1][...], tabs[2][...], HEAD_DIM // 4)
        k_ref[:, g * HEAD_DIM:(g + 1) * HEAD_DIM] = y.astype(BF16)
    if cache:
        rest[3][...] = p_ref[:, OFF_AV:OFF_BQ].astype(F32)


def _prep_cq_kernel(p_ref, g_ref, w_ref, *rest, rope):
    rest = list(rest)
    tabs = [rest.pop(0) for _ in range(3)] if rope else None
    q_ref = rest[0]
    scale = (C_NOPE_DIM + C_ROPE_DIM) ** -0.5 * LOG2E
    y = _rms(p_ref[...].astype(F32), g_ref[...]).astype(BF16)
    z = jnp.dot(y, w_ref[...], preferred_element_type=F32)
    for h in range(C_HEADS):
        lo = h * C_QK_PAD
        q_ref[:, lo:lo + C_NOPE_DIM] = (z[:, lo:lo + C_NOPE_DIM] * scale).astype(BF16)
        r = z[:, lo + C_NOPE_DIM:lo + C_QK_PAD]
        if rope:
            r = _rope(r, tabs[0][...], tabs[1][...], tabs[2][...], C_ROPE_DIM // 4)
        q_ref[:, lo + C_NOPE_DIM:lo + C_QK_PAD] = (r * scale).astype(BF16)


def _prep_ckv_kernel(ckv_ref, kr_ref, g_ref, w_ref, *rest, norm, rope, cache):
    rest = list(rest)
    tabs = [rest.pop(0) for _ in range(3)] if rope else None
    kc_ref, vc_ref = rest[:2]
    x = ckv_ref[...].astype(F32)
    if norm:
        x = _rms(x, g_ref[...])
    kr = kr_ref[...].astype(F32)
    if cache:
        rest[2][...] = x
        rest[3][...] = kr[:, :C_ROPE_DIM]
    if rope:
        kr = _rope(kr, tabs[0][...], tabs[1][...], tabs[2][...], C_ROPE_DIM // 4)
    krb = kr.astype(BF16)
    z = jnp.dot(x.astype(BF16), w_ref[...], preferred_element_type=F32)
    up = C_NOPE_DIM + C_V_DIM
    for h in range(C_HEADS):
        kc_ref[:, h * C_QK_PAD:h * C_QK_PAD + C_NOPE_DIM] = z[:, h * up:h * up + C_NOPE_DIM].astype(BF16)
        kc_ref[:, h * C_QK_PAD + C_NOPE_DIM:(h + 1) * C_QK_PAD] = krb
        vc_ref[:, h * C_V_DIM:(h + 1) * C_V_DIM] = z[:, h * up + C_NOPE_DIM:(h + 1) * up].astype(BF16)


def _prep_ckv(ckv_src, ckv_blk, kr_src, kr_blk, g, w_ukv, layer, tabs, n_per, *, norm, cache):
    m = ckv_src.shape[0]
    tr = min(512, n_per)
    rope = tabs is not None
    in_specs = [
        pl.BlockSpec((tr, C_KV_RANK), lambda i: (i, ckv_blk)),
        pl.BlockSpec((tr, LANE), lambda i: (i, kr_blk)),
        pl.BlockSpec((None, 1, C_KV_RANK), lambda i: (layer, 0, 0)),
        pl.BlockSpec((None, C_KV_RANK, C_HEADS * (C_NOPE_DIM + C_V_DIM)), lambda i: (layer, 0, 0)),
    ]
    args = [ckv_src, kr_src, g, w_ukv]
    if rope:
        nt = n_per // tr
        in_specs += [pl.BlockSpec((tr, LANE), lambda i: (i % nt, 0))] * 3
        args += list(tabs)
    out_specs = [pl.BlockSpec((tr, C_HEADS * C_QK_PAD), lambda i: (i, 0)),
                 pl.BlockSpec((tr, C_HEADS * C_V_DIM), lambda i: (i, 0))]
    out_shape = [jax.ShapeDtypeStruct((m, C_HEADS * C_QK_PAD), BF16),
                 jax.ShapeDtypeStruct((m, C_HEADS * C_V_DIM), BF16)]
    if cache:
        out_specs += [pl.BlockSpec((tr, C_KV_RANK), lambda i: (i, 0)),
                      pl.BlockSpec((tr, C_ROPE_DIM), lambda i: (i, 0))]
        out_shape += [jax.ShapeDtypeStruct((m, C_KV_RANK), F32),
                      jax.ShapeDtypeStruct((m, C_ROPE_DIM), F32)]
    return pl.pallas_call(
        functools.partial(_prep_ckv_kernel, norm=norm, rope=rope, cache=cache),
        grid=(m // tr,),
        in_specs=in_specs,
        out_specs=out_specs,
        out_shape=out_shape,
        compiler_params=_params("arbitrary"),
        name="prep_ckv",
    )(*args)


def _prep_all_kernel(pa_ref, pcq_ref, ckv_ref, kr_ref, gq_ref, gk_ref, gcq_ref, gckv_ref, wuq_ref, wukv_ref,
                     *rest, rope, cache):
    rest = list(rest)
    t128, t64, t64p = ([rest.pop(0) for _ in range(3)] if rope else [] for _ in range(3))
    n_out = 4 if cache else 2
    a_out = [rest.pop(0) for _ in range(n_out)]
    cq_out = [rest.pop(0)]
    ckv_out = [rest.pop(0) for _ in range(n_out)]
    _prep_a_kernel(pa_ref, gq_ref, gk_ref, *t128, *a_out, rope=rope, cache=cache)
    _prep_cq_kernel(pcq_ref, gcq_ref, wuq_ref, *t64p, *cq_out, rope=rope)
    _prep_ckv_kernel(ckv_ref, kr_ref, gckv_ref, wukv_ref, *t64, *ckv_out, norm=True, rope=rope, cache=cache)


def _prep_all(proj, gq, gk, gcq, gckv, w_uq, w_ukv, layer, tabs, n_per, *, cache):
    m = proj.shape[0]
    tr = min(512, n_per)
    rope = tabs is not None
    kvw = A_KV_HEADS * HEAD_DIM
    vec = lambda w: pl.BlockSpec((None, 1, w), lambda i: (layer, 0, 0))
    mat = lambda w: pl.BlockSpec((None,) + w.shape[1:], lambda i: (layer, 0, 0))
    row = lambda w, blk=0: pl.BlockSpec((tr, w), lambda i: (i, blk))
    in_specs = [row(OFF_BQ), row(C_Q_RANK, OFF_CQ // C_Q_RANK), row(C_KV_RANK, OFF_CKV // C_KV_RANK),
                row(LANE, OFF_CKR // LANE), vec(HEAD_DIM), vec(HEAD_DIM), vec(C_Q_RANK), vec(C_KV_RANK),
                mat(w_uq), mat(w_ukv)]
    args = [proj, proj, proj, proj, gq, gk, gcq, gckv, w_uq, w_ukv]
    if rope:
        nt = n_per // tr
        in_specs += [pl.BlockSpec((tr, LANE), lambda i: (i % nt, 0))] * 9
        args += [t for group in tabs for t in group]
    out = lambda w, dt: (row(w), jax.ShapeDtypeStruct((m, w), dt))
    outs = [out(A_HEADS * HEAD_DIM, BF16), out(kvw, BF16)]
    outs += [out(kvw, F32), out(kvw, F32)] if cache else []
    outs += [out(C_HEADS * C_QK_PAD, BF16), out(C_HEADS * C_QK_PAD, BF16), out(C_HEADS * C_V_DIM, BF16)]
    outs += [out(C_KV_RANK, F32), out(C_ROPE_DIM, F32)] if cache else []
    res = pl.pallas_call(
        functools.partial(_prep_all_kernel, rope=rope, cache=cache),
        grid=(m // tr,),
        in_specs=in_specs,
        out_specs=[o[0] for o in outs],
        out_shape=[o[1] for o in outs],
        compiler_params=_params("arbitrary"),
        name="prep_all",
    )(*args)
    n_out = 4 if cache else 2
    return res[:n_out], res[n_out], res[n_out + 1:]


def _attn_kernel(q_ref, k_ref, v_ref, *rest, gps, rep, dqk, dv, tq, tk, ctx):
    if ctx:
        k2_ref, v2_ref, o_ref, acc_ref, s0_ref, s1_ref = rest
    else:
        o_ref, acc_ref, s0_ref, s1_ref = rest
    s_refs = (s0_ref, s1_ref)

    items = []
    for g in range(gps):
        chunks = [(k_ref, v_ref, c * tk, tk) for c in range(k_ref.shape[0] // tk)]
        if ctx:
            nctx = k2_ref.shape[0]
            chunks += [(k2_ref, v2_ref, lo, min(tk, nctx - lo)) for lo in range(0, nctx, tk)]
        items += [(g, c == 0, c == len(chunks) - 1) + ch for c, ch in enumerate(chunks)]

    def queries(g):
        cols = [q_ref[:, (g * rep + r) * dqk:(g * rep + r + 1) * dqk] for r in range(rep)]
        return cols[0] if rep == 1 else jnp.concatenate(cols, axis=0)

    def scores(t):
        g, _, _, kr, _, lo, size = items[t]
        s_refs[t % 2][:, :size] = lax.dot_general(
            queries(g), kr[lo:lo + size, g * dqk:(g + 1) * dqk], (((1,), (1,)), ((), ())),
            preferred_element_type=F32)

    scores(0)
    m = None
    for t, (g, first, last, _, vr, lo, size) in enumerate(items):
        if t + 1 < len(items):
            scores(t + 1)
        s = s_refs[t % 2][:, :size]
        m_blk = jnp.max(s, axis=-1, keepdims=True)
        m_new = m_blk if first else jnp.maximum(m, m_blk)
        p = jnp.exp2(s - m_new).astype(BF16)
        v_ext = jnp.concatenate([vr[lo:lo + size, g * dv:(g + 1) * dv], jnp.ones((size, dv), BF16)], axis=1)
        pv = jnp.dot(p, v_ext, preferred_element_type=F32)
        if first:
            acc_ref[g] = pv
        else:
            acc_ref[g] = jnp.exp2(m - m_new) * acc_ref[g] + pv
        m = m_new
        if last:
            o = acc_ref[g, :, :dv] / acc_ref[g, :, dv:]
            for r in range(rep):
                o_ref[:, (g * rep + r) * dv:(g * rep + r + 1) * dv] = o[r * tq:(r + 1) * tq].astype(o_ref.dtype)


def _attention(q, k, kblk0, v, vblk0, ctx_kv, batch, n, *, groups, gps, rep, dqk, dv, rows, name):
    tq = min(rows // rep, n)
    tk = min(512, n)
    nq = n // tq
    assert groups % gps == 0 and kblk0 % gps == 0 and vblk0 % gps == 0 and (ctx_kv is None or gps == 1)
    in_specs = [
        pl.BlockSpec((tq, gps * rep * dqk), lambda b, g, i: (b * nq + i, g)),
        pl.BlockSpec((n, gps * dqk), lambda b, g, i: (b, kblk0 // gps + g)),
        pl.BlockSpec((n, gps * dv), lambda b, g, i: (b, vblk0 // gps + g)),
    ]
    args = [q, k, v]
    if ctx_kv is not None:
        k2, v2 = ctx_kv
        nc = k2.shape[0] // batch
        in_specs += [pl.BlockSpec((nc, dqk), lambda b, g, i: (b, g)),
                     pl.BlockSpec((nc, dv), lambda b, g, i: (b, g))]
        args += [k2, v2]
    return pl.pallas_call(
        functools.partial(_attn_kernel, gps=gps, rep=rep, dqk=dqk, dv=dv, tq=tq, tk=tk, ctx=ctx_kv is not None),
        grid=(batch, groups // gps, nq),
        in_specs=in_specs,
        out_specs=pl.BlockSpec((tq, gps * rep * dv), lambda b, g, i: (b * nq + i, g)),
        out_shape=jax.ShapeDtypeStruct((batch * n, groups * rep * dv), BF16),
        scratch_shapes=[pltpu.VMEM((gps, rep * tq, 2 * dv), F32),
                        pltpu.VMEM((rep * tq, tk), F32),
                        pltpu.VMEM((rep * tq, tk), F32)],
        compiler_params=_params("arbitrary", "arbitrary", "arbitrary"),
        name=name,
    )(*args)


def _ret_head(lgf, lgb, io, tabs, nc, kvf_ref, kvb_ref, state_ref):
    c_len = B_CHUNK
    rope = tabs is not None
    ii = lax.broadcasted_iota(jnp.int32, (c_len, c_len), 0)
    jj = lax.broadcasted_iota(jnp.int32, (c_len, c_len), 1)
    diff = (ii - jj).astype(F32)
    dmat = (jnp.where(diff >= 0, jnp.exp(jnp.maximum(diff, 0.0) * lgf), 0.0)
            + jnp.where(diff <= 0, jnp.exp(jnp.maximum(-diff, 0.0) * lgb), 0.0))
    ri = lax.broadcasted_iota(jnp.int32, (c_len, 1), 0).astype(F32)
    qdec_f = jnp.exp((ri + 1.0) * lgf)
    qdec_b = jnp.exp((c_len - ri) * lgb)
    kdec_f = jnp.exp((c_len - 1.0 - ri) * lgf)
    kdec_b = jnp.exp(ri * lgb)
    one = jnp.ones((1, 1), F32)
    cdec_f = jnp.exp(one * (c_len * lgf))
    cdec_b = jnp.exp(one * (c_len * lgb))
    kscale = B_QK_DIM ** -0.5

    def roped(load, lo):
        x = load(lo).astype(F32)
        if rope:
            t = [tab[pl.ds(lo, c_len), :] for tab in tabs]
            x = _rope(x, t[0], t[1], t[2], B_QK_DIM // 4)
        return x

    def increments(c, carry):
        lo = pl.multiple_of(c * c_len, c_len)
        kc = roped(io["k"], lo) * kscale
        kk = jnp.concatenate([kc * kdec_f, kc * kdec_b], axis=1).T.astype(BF16)
        kv = jnp.dot(kk, io["v"](lo), preferred_element_type=F32)
        kvf_ref[c] = kv[:B_QK_DIM]
        kvb_ref[c] = kv[B_QK_DIM:]
        return carry

    lax.fori_loop(0, nc, increments, 0, unroll=_unroll(nc, 8))

    def scan_f(c, s):
        state_ref[c, :B_QK_DIM, :] = s.astype(BF16)
        return cdec_f * s + kvf_ref[c]

    def scan_b(t, s):
        c = nc - 1 - t
        state_ref[c, B_QK_DIM:, :] = s.astype(BF16)
        return cdec_b * s + kvb_ref[c]

    s_f = lax.fori_loop(0, nc, scan_f, io["s0f"], unroll=_unroll(nc, 2))
    s_b = lax.fori_loop(0, nc, scan_b, io["s0b"], unroll=_unroll(nc, 2))
    io["store_state"](s_f, s_b)

    def outputs(c, carry):
        lo = pl.multiple_of(c * c_len, c_len)
        qc = roped(io["q"], lo)
        kc = (roped(io["k"], lo) * kscale).astype(BF16)
        s = lax.dot_general(qc.astype(BF16), kc, (((1,), (1,)), ((), ())), preferred_element_type=F32) * dmat
        o = jnp.dot(s.astype(BF16), io["v"](lo), preferred_element_type=F32)
        qq = jnp.concatenate([qc * qdec_f, qc * qdec_b], axis=1).astype(BF16)
        o = o + jnp.dot(qq, state_ref[c], preferred_element_type=F32)
        gate = io["gate"](lo).astype(F32)
        io["store_o"](lo, (gate * jax.nn.sigmoid(gate) * _rms(o, io["gn"])).astype(BF16))
        return carry

    lax.fori_loop(0, nc, outputs, 0, unroll=_unroll(nc, 4))


def _ret_kernel(lg_ref, *refs, n, rope, emit_state, all_heads):
    refs = list(refs)
    data = [refs.pop(0) for _ in range(2 if all_heads else 4)]
    gn_ref, s0f_ref, s0b_ref = refs.pop(0), refs.pop(0), refs.pop(0)
    tabs = [refs.pop(0) for _ in range(3)] if rope else None
    o_ref = refs.pop(0)
    sfo_ref, sbo_ref = (refs.pop(0), refs.pop(0)) if emit_state else (None, None)
    kvf_ref, kvb_ref, state_ref = refs
    nc = n // B_CHUNK
    rows = lambda lo: pl.ds(lo, B_CHUNK)
    qw, vw = B_HEADS * B_QK_DIM, B_V_DIM

    for hh in range(B_HEADS if all_heads else 1):
        if all_heads:
            h = hh
            blk1, blk2 = data
            v_src = (blk1, 2 * qw + hh * vw) if hh < 2 else (blk2, (hh - 2) * vw)
            col = lambda ref, c0, w: (lambda lo: ref[rows(lo), c0:c0 + w])
            io = dict(
                q=col(blk1, hh * B_QK_DIM, B_QK_DIM), k=col(blk1, qw + hh * B_QK_DIM, B_QK_DIM),
                v=col(v_src[0], v_src[1], vw), gate=col(blk2, 2 * vw + hh * vw, vw),
                gn=gn_ref[:, hh * vw:(hh + 1) * vw], s0f=s0f_ref[hh], s0b=s0b_ref[hh])

            def store_o(lo, val, hh=hh):
                o_ref[rows(lo), hh * vw:(hh + 1) * vw] = val

            def store_state(sf, sb, hh=hh):
                if emit_state:
                    sfo_ref[hh] = sf
                    sbo_ref[hh] = sb
        else:
            h = pl.program_id(1)
            whole = lambda ref: (lambda lo: ref[rows(lo), :])
            io = dict(q=whole(data[0]), k=whole(data[1]), v=whole(data[2]), gate=whole(data[3]),
                      gn=gn_ref[...], s0f=s0f_ref[...], s0b=s0b_ref[...])

            def store_o(lo, val):
                o_ref[rows(lo), :] = val

            def store_state(sf, sb):
                if emit_state:
                    sfo_ref[...] = sf
                    sbo_ref[...] = sb
        io.update(store_o=store_o, store_state=store_state)
        _ret_head(lg_ref[0, h], lg_ref[1, h], io, tabs, nc, kvf_ref.at[hh], kvb_ref.at[hh], state_ref.at[hh])


def _retention(proj, lg, gn, s0f, s0b, layer, tabs, batch, n, *, emit_state, all_heads):
    rope = tabs is not None
    if all_heads:
        wide = (OFF_CQ - OFF_BQ) // 2
        assert OFF_BQ % wide == 0 and wide == 2 * B_HEADS * B_QK_DIM + 2 * B_V_DIM
        st_spec = pl.BlockSpec((None, B_HEADS, B_QK_DIM, B_V_DIM), lambda b, h, lg_: (b, 0, 0, 0))
        in_specs = [
            pl.BlockSpec((n, wide), lambda b, h, lg_: (b, OFF_BQ // wide)),
            pl.BlockSpec((n, wide), lambda b, h, lg_: (b, OFF_BQ // wide + 1)),
            pl.BlockSpec((None, 1, B_HEADS * B_V_DIM), lambda b, h, lg_: (layer, 0, 0)),
        ]
        args = [proj, proj, gn]
        out_specs = [pl.BlockSpec((n, B_HEADS * B_V_DIM), lambda b, h, lg_: (b, 0))]
        heads_per_step = B_HEADS
    else:
        qb, kb = OFF_BQ // B_QK_DIM, OFF_BK // B_QK_DIM
        vb, gb = OFF_BV // B_V_DIM, OFF_BG // B_V_DIM
        st_spec = pl.BlockSpec((None, None, B_QK_DIM, B_V_DIM), lambda b, h, lg_: (b, h, 0, 0))
        in_specs = [
            pl.BlockSpec((n, B_QK_DIM), lambda b, h, lg_: (b, qb + h)),
            pl.BlockSpec((n, B_QK_DIM), lambda b, h, lg_: (b, kb + h)),
            pl.BlockSpec((n, B_V_DIM), lambda b, h, lg_: (b, vb + h)),
            pl.BlockSpec((n, B_V_DIM), lambda b, h, lg_: (b, gb + h)),
            pl.BlockSpec((None, 1, B_V_DIM), lambda b, h, lg_: (layer, 0, h)),
        ]
        args = [proj, proj, proj, proj, gn]
        out_specs = [pl.BlockSpec((n, B_V_DIM), lambda b, h, lg_: (b, h))]
        heads_per_step = 1
    in_specs += [st_spec, st_spec]
    args += [s0f, s0b]
    if rope:
        in_specs += [pl.BlockSpec((n, LANE), lambda b, h, lg_: (0, 0))] * 3
        args += list(tabs)
    out_shape = [jax.ShapeDtypeStruct((batch * n, B_HEADS * B_V_DIM), BF16)]
    if emit_state:
        out_specs += [st_spec, st_spec]
        out_shape += [jax.ShapeDtypeStruct((batch, B_HEADS, B_QK_DIM, B_V_DIM), F32)] * 2
    nc = n // B_CHUNK
    return pl.pallas_call(
        functools.partial(_ret_kernel, n=n, rope=rope, emit_state=emit_state, all_heads=all_heads),
        grid_spec=pltpu.PrefetchScalarGridSpec(
            num_scalar_prefetch=1,
            grid=(batch, B_HEADS // heads_per_step),
            in_specs=in_specs,
            out_specs=out_specs,
            scratch_shapes=[pltpu.VMEM((heads_per_step, nc, B_QK_DIM, B_V_DIM), F32),
                            pltpu.VMEM((heads_per_step, nc, B_QK_DIM, B_V_DIM), F32),
                            pltpu.VMEM((heads_per_step, nc, 2 * B_QK_DIM, B_V_DIM), BF16)],
        ),
        out_shape=out_shape,
        compiler_params=_params("arbitrary", "arbitrary"),
        name="retention",
    )(lg, *args)


def _merge_kernel(oa_ref, ob_ref, oc_ref, wa_ref, wb_ref, wc_ref, ga_ref, gb_ref, gc_ref, o_ref):
    def branch(o, w, g):
        return jax.nn.sigmoid(g[...].astype(F32)) * jnp.dot(o[...], w[...], preferred_element_type=F32)

    o_ref[...] = (branch(oa_ref, wa_ref, ga_ref) + branch(ob_ref, wb_ref, gb_ref)
                  + branch(oc_ref, wc_ref, gc_ref)).astype(BF16)


def _merge(oa, ob, oc, wa, wb, wc, gates, layer, d):
    m = oa.shape[0]
    tm = min(1024, m)
    tn = min(1024, d)
    nb = d // tn
    o_spec = lambda o: pl.BlockSpec((tm, o.shape[1]), lambda i, j: (i, 0))
    w_spec = lambda w: pl.BlockSpec((None, w.shape[1], tn), lambda i, j: (layer, 0, j))
    g_spec = lambda br: pl.BlockSpec((tm, tn), lambda i, j: (i, br * nb + j))
    return pl.pallas_call(
        _merge_kernel,
        grid=(m // tm, nb),
        in_specs=[o_spec(oa), o_spec(ob), o_spec(oc), w_spec(wa), w_spec(wb), w_spec(wc),
                  g_spec(0), g_spec(1), g_spec(2)],
        out_specs=pl.BlockSpec((tm, tn), lambda i, j: (i, j)),
        out_shape=jax.ShapeDtypeStruct((m, d), BF16),
        compiler_params=_params("arbitrary", "arbitrary"),
        name="merge",
    )(oa, ob, oc, wa, wb, wc, gates, gates, gates)


def _rope_tables(n_tokens, dim, pad):
    pos = jnp.arange(n_tokens, dtype=jnp.int32)
    row = (pos // GRID_W).astype(F32)[:, None]
    col = (pos % GRID_W).astype(F32)[:, None]
    quarter = dim // 4
    inv = ROPE_THETA ** (-jnp.arange(quarter, dtype=F32) / quarter)[None, :]
    ang = jnp.concatenate([row * inv, row * inv, col * inv, col * inv], axis=1)
    cos, sin = jnp.cos(ang), jnp.sin(ang)
    first = (jnp.arange(dim) // quarter) % 2 == 0
    sin_lo = jnp.where(first[None, :], -sin, 0.0)
    sin_hi = jnp.where(first[None, :], 0.0, sin)
    if pad:
        z = jnp.zeros_like(cos)
        tabs = [jnp.concatenate([t, z], axis=1) for t in (cos, sin_lo, sin_hi)]
    else:
        tabs = [jnp.tile(t, (1, LANE // dim)) for t in (cos, sin_lo, sin_hi)]
    return tuple(tabs)


def kernel(x_prompt, x_sample, c, cache_attn_k, cache_attn_v, state_ret_fwd, state_ret_bwd, cache_mla_ckv, cache_mla_krope, c_ctx, w_mod, b_mod, g_pre_mix, g_post_mix, g_pre_mlp, g_post_mlp, w_in, attn_q_norm, attn_k_norm, ret_decay_fwd, ret_decay_bwd, ret_gn, mla_q_norm, mla_kv_norm, w_mla_uq, w_mla_ukv, w_branch_a, w_branch_b, w_branch_c, w_out, w_mlp_up, w_mlp_down):
    bp, n_p, d = x_prompt.shape
    bs, n_s, _ = x_sample.shape
    depth = w_in.shape[0]
    past = cache_attn_k.shape[2]
    assert n_p % B_CHUNK == 0 and n_s % B_CHUNK == 0

    w_main, w_gate = _split_w_in(w_in)
    w_uq = jnp.pad(w_mla_uq.reshape(depth, C_Q_RANK, C_HEADS, C_NOPE_DIM + C_ROPE_DIM),
                   ((0, 0), (0, 0), (0, 0), (0, C_QK_PAD - C_NOPE_DIM - C_ROPE_DIM)))
    w_uq = w_uq.reshape(depth, C_Q_RANK, C_HEADS * C_QK_PAD).astype(BF16)
    w_ukv = w_mla_ukv.astype(BF16)
    wa, wb, wc = w_branch_a.astype(BF16), w_branch_b.astype(BF16), w_branch_c.astype(BF16)
    w_o, w_up, w_dn = w_out.astype(BF16), w_mlp_up.astype(BF16), w_mlp_down.astype(BF16)
    vec = lambda g: g.reshape(depth, 1, g.shape[-1])
    g_pre_mix, g_post_mix, g_pre_mlp, g_post_mlp = map(vec, (g_pre_mix, g_post_mix, g_pre_mlp, g_post_mlp))
    gqn, gkn, gcq, gckv, gret = map(vec, (attn_q_norm, attn_k_norm, mla_q_norm, mla_kv_norm, ret_gn))
    lg = jnp.stack([jax.nn.log_sigmoid(ret_decay_fwd.astype(F32)),
                    jax.nn.log_sigmoid(ret_decay_bwd.astype(F32))], axis=1)

    rows = -(-(1 + bs) // 8) * 8
    cvec = jnp.zeros((rows, d), F32).at[0].set(c_ctx).at[1:1 + bs].set(c)
    mod = _modulation(cvec, w_mod, b_mod).reshape(depth, rows, 6, d)
    mod = jnp.pad(mod, ((0, 0), (0, 0), (0, MOD_ROWS - 6), (0, 0)))

    tabs128 = _rope_tables(n_s, HEAD_DIM, pad=False)
    tabs64 = _rope_tables(n_s, C_ROPE_DIM, pad=False)
    tabs64p = _rope_tables(n_s, C_ROPE_DIM, pad=True)

    ctx_k = cache_attn_k.astype(BF16).reshape(bs, depth, past, A_KV_HEADS * HEAD_DIM)
    ctx_v = cache_attn_v.astype(BF16).reshape(bs, depth, past, A_KV_HEADS * HEAD_DIM)
    ctx_kr = jnp.pad(cache_mla_krope, ((0, 0), (0, 0), (0, 0), (0, LANE - C_ROPE_DIM)))
    zero_state = jnp.zeros((bp, B_HEADS, B_QK_DIM, B_V_DIM), F32)

    streams = {
        "p": dict(x=x_prompt.reshape(bp * n_p, d), batch=bp, n=n_p, grp=lambda r: 0, latent=False),
        "s": dict(x=x_sample.reshape(bs * n_s, d), batch=bs, n=n_s, grp=lambda r: 1 + r // n_s, latent=True),
    }
    for st in streams.values():
        st["h"] = _prenorm(st["x"], g_pre_mix, mod, 0, st["grp"])
    caches = [[] for _ in range(6)]

    for l in range(depth):
        for st in streams.values():
            batch, n, grp, latent = st["batch"], st["n"], st["grp"], st["latent"]
            proj = _matmul(st["h"], w_main, l, tm=2048, tn=512, w_transposed=True, name="w_in")
            gates = _matmul(st["h"], w_gate, l, tm=2048, tn=512, w_transposed=True, name="w_gates")
            t128 = tabs128 if latent else None

            tabs = (tabs128, tabs64, tabs64p) if latent else None
            pa, qc, pc = _prep_all(proj, gqn, gkn, gcq, gckv, w_uq, w_ukv, l, tabs, n, cache=not latent)

            qa, ka = pa[0], pa[1]
            ctx_a = (ctx_k[:, l].reshape(bs * past, -1), ctx_v[:, l].reshape(bs * past, -1)) if latent else None
            o_a = _attention(qa, ka, 0, proj, OFF_AV // HEAD_DIM, ctx_a, batch, n,
                             groups=A_KV_HEADS, gps=1 if latent else A_KV_HEADS, rep=A_HEADS // A_KV_HEADS,
                             dqk=HEAD_DIM, dv=HEAD_DIM, rows=1024, name="attn_a")

            if latent:
                s0f, s0b = state_ret_fwd[:, l], state_ret_bwd[:, l]
            else:
                s0f = s0b = zero_state
            rb = _retention(proj, lg[l], gret, s0f, s0b, l, t128, batch, n,
                            emit_state=not latent, all_heads=not latent)
            o_b = rb[0]

            kc, vc = pc[0], pc[1]
            ctx_c = None
            if latent:
                ctx_c = _prep_ckv(cache_mla_ckv[:, l].reshape(bs * past, C_KV_RANK), 0,
                                  ctx_kr[:, l].reshape(bs * past, LANE), 0, gckv, w_ukv, l, None, past,
                                  norm=False, cache=False)
            o_c = _attention(qc, kc, 0, vc, 0, ctx_c, batch, n,
                             groups=C_HEADS, gps=1 if latent else C_HEADS, rep=1,
                             dqk=C_QK_PAD, dv=C_V_DIM, rows=1024, name="attn_c")

            if not latent:
                for dst, val in zip(caches, (pa[2], pa[3], rb[1], rb[2], pc[2], pc[3])):
                    dst.append(val)

            merged = _merge(o_a, o_b, o_c, wa, wb, wc, gates, l, d)
            x, h2 = _matmul_row(merged, w_o, st["x"], g_post_mix, g_pre_mlp, mod, l, l, grp,
                                gate_row=2, next_rows=(3, 4), tm=512, tk=2048, name="w_out")
            hid = _matmul(h2, w_up, l, tm=2048, tn=512, relu2=True, name="mlp_up")
            last = l == depth - 1
            res = _matmul_row(hid, w_dn, x, g_post_mlp, g_pre_mix, mod, l, min(l + 1, depth - 1), grp,
                              gate_row=5, next_rows=None if last else (0, 1), tm=512, tk=2048, name="mlp_down")
            st["x"] = res[0]
            st["h"] = None if last else res[1]

    y_p = streams["p"]["x"].reshape(bp, n_p, d)
    y_s = streams["s"]["x"].reshape(bs, n_s, d)
    nk, nv, sf, sb, ckv, kr = caches
    new_attn_k = jnp.stack(nk, axis=1).reshape(bp, n_p, depth, A_KV_HEADS, HEAD_DIM).transpose(0, 2, 1, 3, 4)
    new_attn_v = jnp.stack(nv, axis=1).reshape(bp, n_p, depth, A_KV_HEADS, HEAD_DIM).transpose(0, 2, 1, 3, 4)
    new_ret_fwd = jnp.stack(sf, axis=1)
    new_ret_bwd = jnp.stack(sb, axis=1)
    new_mla_ckv = jnp.stack(ckv, axis=1).reshape(bp, n_p, depth, C_KV_RANK).transpose(0, 2, 1, 3)
    new_mla_krope = jnp.stack(kr, axis=1).reshape(bp, n_p, depth, C_ROPE_DIM).transpose(0, 2, 1, 3)
    return (y_p, y_s, new_attn_k, new_attn_v, new_ret_fwd, new_ret_bwd, new_mla_ckv, new_mla_krope)
```

```python
import functools

import jax
import jax.numpy as jnp
from jax import lax
from jax.experimental import pallas as pl
from jax.experimental.pallas import tpu as pltpu

F32 = jnp.float32
BF16 = jnp.bfloat16

EPS = 1e-6
LOG2E = 1.4426950408889634
ROPE_THETA = 10000.0
GRID_W = 64
LANE = 128
HEAD_DIM = 128
A_HEADS = 8
A_KV_HEADS = 2
B_HEADS = 4
B_QK_DIM = 128
B_V_DIM = 256
B_CHUNK = 128
C_HEADS = 8
C_Q_RANK = 512
C_KV_RANK = 256
C_NOPE_DIM = 128
C_ROPE_DIM = 64
C_V_DIM = 128
C_QK_PAD = 256

OFF_AQ = 0
OFF_AK = OFF_AQ + A_HEADS * HEAD_DIM
OFF_AV = OFF_AK + A_KV_HEADS * HEAD_DIM
OFF_BQ = OFF_AV + A_KV_HEADS * HEAD_DIM
OFF_BK = OFF_BQ + B_HEADS * B_QK_DIM
OFF_BV = OFF_BK + B_HEADS * B_QK_DIM
OFF_BG = OFF_BV + B_HEADS * B_V_DIM
OFF_CQ = OFF_BG + B_HEADS * B_V_DIM
OFF_CKV = OFF_CQ + C_Q_RANK
OFF_CKR = OFF_CKV + C_KV_RANK
IN_MAIN = OFF_CKR + C_ROPE_DIM
MAIN_W = 5632
MOD_ROWS = 8

VMEM_LIMIT = 56 * 1024 * 1024


def _params(*sem):
    return pltpu.CompilerParams(dimension_semantics=sem, vmem_limit_bytes=VMEM_LIMIT)


def _rms(x, g):
    return x * lax.rsqrt(jnp.mean(x * x, axis=-1, keepdims=True) + EPS) * g


def _unroll(trips, want):
    while trips % want:
        want //= 2
    return want


def _rope(x, cos, sin_lo, sin_hi, shift):
    return (x * cos + pltpu.roll(x, LANE - shift, 1) * sin_lo + pltpu.roll(x, shift, 1) * sin_hi)


def _mod_kernel(c_ref, w_ref, b_ref, o_ref):
    c = c_ref[...]
    a = (c * jax.nn.sigmoid(c)).astype(BF16)
    o_ref[...] = jnp.dot(a, w_ref[...].astype(BF16), preferred_element_type=F32) + b_ref[...]


def _modulation(cvec, w_mod, b_mod):
    depth, d, n = w_mod.shape
    rows = cvec.shape[0]
    tn = min(1024, n)
    return pl.pallas_call(
        _mod_kernel,
        grid=(depth, n // tn),
        in_specs=[
            pl.BlockSpec((rows, d), lambda l, j: (0, 0)),
            pl.BlockSpec((None, d, tn), lambda l, j: (l, 0, j)),
            pl.BlockSpec((None, 1, tn), lambda l, j: (l, 0, j)),
        ],
        out_specs=pl.BlockSpec((None, rows, tn), lambda l, j: (l, 0, j)),
        out_shape=jax.ShapeDtypeStruct((depth, rows, n), F32),
        compiler_params=_params("arbitrary", "arbitrary"),
        name="modulation",
    )(cvec, w_mod, b_mod.reshape(depth, 1, n))


def _prenorm_kernel(x_ref, g_ref, mod_ref, h_ref):
    y = _rms(x_ref[...], g_ref[...])
    h_ref[...] = (y * (1.0 + mod_ref[1:2, :]) + mod_ref[0:1, :]).astype(BF16)


def _prenorm(x, g, mod, layer, grp):
    m, d = x.shape
    tm = min(512, m)
    return pl.pallas_call(
        _prenorm_kernel,
        grid=(m // tm,),
        in_specs=[
            pl.BlockSpec((tm, d), lambda i: (i, 0)),
            pl.BlockSpec((None, 1, d), lambda i: (layer, 0, 0)),
            pl.BlockSpec((None, None, MOD_ROWS, d), lambda i: (layer, grp(i * tm), 0, 0)),
        ],
        out_specs=pl.BlockSpec((tm, d), lambda i: (i, 0)),
        out_shape=jax.ShapeDtypeStruct((m, d), BF16),
        compiler_params=_params("arbitrary"),
        name="prenorm",
    )(x, g, mod)


def _mm_kernel(a_ref, w_ref, o_ref, *, relu2, w_transposed):
    w = (w_ref[0] if w_transposed else w_ref[...]).astype(BF16)
    contract = (((1,), (1 if w_transposed else 0,)), ((), ()))
    y = lax.dot_general(a_ref[...], w, contract, preferred_element_type=F32)
    if relu2:
        y = jnp.square(jnp.maximum(y, 0.0))
    o_ref[...] = y.astype(o_ref.dtype)


def _matmul(a, w, layer, *, tm, tn, relu2=False, w_transposed=False, w_row0=0, n=None, name):
    m, k = a.shape
    n = n if w_transposed else w.shape[2]
    tm, tn = min(tm, m), min(tn, n)
    if w_transposed:
        w_spec = pl.BlockSpec((pl.Element(1), pl.Element(tn), pl.Element(k)),
                              lambda i, j: (layer, pl.multiple_of(w_row0 + j * tn, 64), 0))
    else:
        w_spec = pl.BlockSpec((None, k, tn), lambda i, j: (layer, 0, j))
    return pl.pallas_call(
        functools.partial(_mm_kernel, relu2=relu2, w_transposed=w_transposed),
        grid=(m // tm, n // tn),
        in_specs=[
            pl.BlockSpec((tm, k), lambda i, j: (i, 0)),
            w_spec,
        ],
        out_specs=pl.BlockSpec((tm, tn), lambda i, j: (i, j)),
        out_shape=jax.ShapeDtypeStruct((m, n), BF16),
        compiler_params=_params("arbitrary", "arbitrary"),
        name=name,
    )(a, w)


def _mm_row_kernel(a_ref, w_ref, x_ref, gpost_ref, mod_ref, gnext_ref, modn_ref, *rest,
                   gate_row, next_rows, nk, nt):
    if next_rows is None:
        xo_ref, acc0_ref, acc1_ref = rest
    else:
        xo_ref, ho_ref, acc0_ref, acc1_ref = rest
    i = pl.program_id(0)
    k = pl.program_id(1)
    tm, d = acc0_ref.shape
    sub = tm // nk

    @pl.when((i == 0) & (k == 0))
    def _():
        acc0_ref[...] = jnp.zeros_like(acc0_ref)
        acc1_ref[...] = jnp.zeros_like(acc1_ref)

    def epilogue(acc_ref):
        rows = slice(None) if nk == 1 else pl.ds(pl.multiple_of(k * sub, sub), sub)
        f = acc_ref[rows, :]
        acc_ref[rows, :] = jnp.zeros((sub, d), F32)
        xn = x_ref[rows, :] + _rms(f, gpost_ref[...] * mod_ref[gate_row:gate_row + 1, :])
        xo_ref[rows, :] = xn
        if next_rows is not None:
            shift_row, scale_row = next_rows
            y = _rms(xn, gnext_ref[...] * (1.0 + modn_ref[scale_row:scale_row + 1, :]))
            ho_ref[rows, :] = (y + modn_ref[shift_row:shift_row + 1, :]).astype(BF16)

    for parity, (acc_mm, acc_ep) in enumerate(((acc0_ref, acc1_ref), (acc1_ref, acc0_ref))):
        @pl.when((i < nt) & (i % 2 == parity))
        def _():
            epilogue(acc_ep)
            acc_mm[...] += jnp.dot(a_ref[...], w_ref[...], preferred_element_type=F32)

    @pl.when(i == nt)
    def _():
        epilogue(acc0_ref if (nt - 1) % 2 == 0 else acc1_ref)


def _matmul_row(a, w, x, gpost, gnext, mod, layer, next_layer, grp, *, gate_row, next_rows, tm, tk, name):
    m, kdim = a.shape
    d = w.shape[-1]
    tm, tk = min(tm, m), min(tk, kdim)
    nk, nt = kdim // tk, m // tm
    assert tm % nk == 0 and (tm // nk) % 16 == 0
    cur = lambda i: jnp.minimum(i, nt - 1)
    kcur = lambda i, k: jnp.where(i < nt, k, nk - 1)
    prev = lambda i: jnp.maximum(i - 1, 0)
    vec = pl.BlockSpec((None, 1, d), lambda i, k: (layer, 0, 0))
    vecn = pl.BlockSpec((None, 1, d), lambda i, k: (next_layer, 0, 0))
    out_specs = [pl.BlockSpec((tm, d), lambda i, k: (prev(i), 0))]
    out_shape = [jax.ShapeDtypeStruct((m, d), F32)]
    if next_rows is not None:
        out_specs.append(pl.BlockSpec((tm, d), lambda i, k: (prev(i), 0)))
        out_shape.append(jax.ShapeDtypeStruct((m, d), BF16))
    return pl.pallas_call(
        functools.partial(_mm_row_kernel, gate_row=gate_row, next_rows=next_rows, nk=nk, nt=nt),
        grid=(nt + 1, nk),
        in_specs=[
            pl.BlockSpec((tm, tk), lambda i, k: (cur(i), kcur(i, k))),
            pl.BlockSpec((None, tk, d), lambda i, k: (layer, kcur(i, k), 0)),
            pl.BlockSpec((tm, d), lambda i, k: (prev(i), 0)),
            vec,
            pl.BlockSpec((None, None, MOD_ROWS, d), lambda i, k: (layer, grp(prev(i) * tm), 0, 0)),
            vecn,
            pl.BlockSpec((None, None, MOD_ROWS, d), lambda i, k: (next_layer, grp(prev(i) * tm), 0, 0)),
        ],
        out_specs=out_specs,
        out_shape=out_shape,
        scratch_shapes=[pltpu.VMEM((tm, d), F32), pltpu.VMEM((tm, d), F32)],
        compiler_params=_params("arbitrary", "arbitrary"),
        name=name,
    )(a, w, x, gpost, mod, gnext, mod)


def _prep_a_kernel(p_ref, gq_ref, gk_ref, *rest, rope, cache):
    rest = list(rest)
    tabs = [rest.pop(0) for _ in range(3)] if rope else None
    q_ref, k_ref = rest[:2]
    scale = HEAD_DIM ** -0.5 * LOG2E
    for h in range(A_HEADS):
        y = _rms(p_ref[:, h * HEAD_DIM:(h + 1) * HEAD_DIM].astype(F32), gq_ref[...])
        if rope:
            y = _rope(y, tabs[0][...], tabs[1][...], tabs[2][...], HEAD_DIM // 4)
        q_ref[:, h * HEAD_DIM:(h + 1) * HEAD_DIM] = (y * scale).astype(BF16)
    for g in range(A_KV_HEADS):
        lo = OFF_AK + g * HEAD_DIM
        y = _rms(p_ref[:, lo:lo + HEAD_DIM].astype(F32), gk_ref[...])
        if cache:
            rest[2][:, g * HEAD_DIM:(g + 1) * HEAD_DIM] = y
        if rope:
            y = _rope(y, tabs[0][...], tabs[1][...], tabs[2][...], HEAD_DIM // 4)
        k_ref[:, g * HEAD_DIM:(g + 1) * HEAD_DIM] = y.astype(BF16)
    if cache:
        rest[3][...] = p_ref[:, OFF_AV:OFF_BQ].astype(F32)


def _prep_cq_kernel(p_ref, g_ref, w_ref, *rest, rope):
    rest = list(rest)
    tabs = [rest.pop(0) for _ in range(3)] if rope else None
    q_ref = rest[0]
    scale = (C_NOPE_DIM + C_ROPE_DIM) ** -0.5 * LOG2E
    y = _rms(p_ref[...].astype(F32), g_ref[...]).astype(BF16)
    z = jnp.dot(y, w_ref[...], preferred_element_type=F32)
    for h in range(C_HEADS):
        lo = h * C_QK_PAD
        q_ref[:, lo:lo + C_NOPE_DIM] = (z[:, lo:lo + C_NOPE_DIM] * scale).astype(BF16)
        r = z[:, lo + C_NOPE_DIM:lo + C_QK_PAD]
        if rope:
            r = _rope(r, tabs[0][...], tabs[1][...], tabs[2][...], C_ROPE_DIM // 4)
        q_ref[:, lo + C_NOPE_DIM:lo + C_QK_PAD] = (r * scale).astype(BF16)


def _prep_ckv_kernel(ckv_ref, kr_ref, g_ref, w_ref, *rest, norm, rope, cache):
    rest = list(rest)
    tabs = [rest.pop(0) for _ in range(3)] if rope else None
    kc_ref, vc_ref = rest[:2]
    x = ckv_ref[...].astype(F32)
    if norm:
        x = _rms(x, g_ref[...])
    kr = kr_ref[...].astype(F32)
    if cache:
        rest[2][...] = x
        rest[3][...] = kr[:, :C_ROPE_DIM]
    if rope:
        kr = _rope(kr, tabs[0][...], tabs[1][...], tabs[2][...], C_ROPE_DIM // 4)
    krb = kr.astype(BF16)
    z = jnp.dot(x.astype(BF16), w_ref[...], preferred_element_type=F32)
    up = C_NOPE_DIM + C_V_DIM
    for h in range(C_HEADS):
        kc_ref[:, h * C_QK_PAD:h * C_QK_PAD + C_NOPE_DIM] = z[:, h * up:h * up + C_NOPE_DIM].astype(BF16)
        kc_ref[:, h * C_QK_PAD + C_NOPE_DIM:(h + 1) * C_QK_PAD] = krb
        vc_ref[:, h * C_V_DIM:(h + 1) * C_V_DIM] = z[:, h * up + C_NOPE_DIM:(h + 1) * up].astype(BF16)


def _prep_ckv(ckv_src, ckv_blk, kr_src, kr_blk, g, w_ukv, layer, tabs, n_per, *, norm, cache):
    m = ckv_src.shape[0]
    tr = min(512, n_per)
    rope = tabs is not None
    in_specs = [
        pl.BlockSpec((tr, C_KV_RANK), lambda i: (i, ckv_blk)),
        pl.BlockSpec((tr, LANE), lambda i: (i, kr_blk)),
        pl.BlockSpec((None, 1, C_KV_RANK), lambda i: (layer, 0, 0)),
        pl.BlockSpec((None, C_KV_RANK, C_HEADS * (C_NOPE_DIM + C_V_DIM)), lambda i: (layer, 0, 0)),
    ]
    args = [ckv_src, kr_src, g, w_ukv]
    if rope:
        nt = n_per // tr
        in_specs += [pl.BlockSpec((tr, LANE), lambda i: (i % nt, 0))] * 3
        args += list(tabs)
    out_specs = [pl.BlockSpec((tr, C_HEADS * C_QK_PAD), lambda i: (i, 0)),
                 pl.BlockSpec((tr, C_HEADS * C_V_DIM), lambda i: (i, 0))]
    out_shape = [jax.ShapeDtypeStruct((m, C_HEADS * C_QK_PAD), BF16),
                 jax.ShapeDtypeStruct((m, C_HEADS * C_V_DIM), BF16)]
    if cache:
        out_specs += [pl.BlockSpec((tr, C_KV_RANK), lambda i: (i, 0)),
                      pl.BlockSpec((tr, C_ROPE_DIM), lambda i: (i, 0))]
        out_shape += [jax.ShapeDtypeStruct((m, C_KV_RANK), F32),
                      jax.ShapeDtypeStruct((m, C_ROPE_DIM), F32)]
    return pl.pallas_call(
        functools.partial(_prep_ckv_kernel, norm=norm, rope=rope, cache=cache),
        grid=(m // tr,),
        in_specs=in_specs,
        out_specs=out_specs,
        out_shape=out_shape,
        compiler_params=_params("arbitrary"),
        name="prep_ckv",
    )(*args)


def _prep_all_kernel(pa_ref, pcq_ref, ckv_ref, kr_ref, gq_ref, gk_ref, gcq_ref, gckv_ref, wuq_ref, wukv_ref,
                     *rest, rope, cache):
    rest = list(rest)
    t128, t64, t64p = ([rest.pop(0) for _ in range(3)] if rope else [] for _ in range(3))
    n_out = 4 if cache else 2
    a_out = [rest.pop(0) for _ in range(n_out)]
    cq_out = [rest.pop(0)]
    ckv_out = [rest.pop(0) for _ in range(n_out)]
    _prep_a_kernel(pa_ref, gq_ref, gk_ref, *t128, *a_out, rope=rope, cache=cache)
    _prep_cq_kernel(pcq_ref, gcq_ref, wuq_ref, *t64p, *cq_out, rope=rope)
    _prep_ckv_kernel(ckv_ref, kr_ref, gckv_ref, wukv_ref, *t64, *ckv_out, norm=True, rope=rope, cache=cache)


def _prep_all(proj, gq, gk, gcq, gckv, w_uq, w_ukv, layer, tabs, n_per, *, cache):
    m = proj.shape[0]
    tr = min(512, n_per)
    rope = tabs is not None
    kvw = A_KV_HEADS * HEAD_DIM
    vec = lambda w: pl.BlockSpec((None, 1, w), lambda i: (layer, 0, 0))
    mat = lambda w: pl.BlockSpec((None,) + w.shape[1:], lambda i: (layer, 0, 0))
    row = lambda w, blk=0: pl.BlockSpec((tr, w), lambda i: (i, blk))
    in_specs = [row(OFF_BQ), row(C_Q_RANK, OFF_CQ // C_Q_RANK), row(C_KV_RANK, OFF_CKV // C_KV_RANK),
                row(LANE, OFF_CKR // LANE), vec(HEAD_DIM), vec(HEAD_DIM), vec(C_Q_RANK), vec(C_KV_RANK),
                mat(w_uq), mat(w_ukv)]
    args = [proj, proj, proj, proj, gq, gk, gcq, gckv, w_uq, w_ukv]
    if rope:
        nt = n_per // tr
        in_specs += [pl.BlockSpec((tr, LANE), lambda i: (i % nt, 0))] * 9
        args += [t for group in tabs for t in group]
    out = lambda w, dt: (row(w), jax.ShapeDtypeStruct((m, w), dt))
    outs = [out(A_HEADS * HEAD_DIM, BF16), out(kvw, BF16)]
    outs += [out(kvw, F32), out(kvw, F32)] if cache else []
    outs += [out(C_HEADS * C_QK_PAD, BF16), out(C_HEADS * C_QK_PAD, BF16), out(C_HEADS * C_V_DIM, BF16)]
    outs += [out(C_KV_RANK, F32), out(C_ROPE_DIM, F32)] if cache else []
    res = pl.pallas_call(
        functools.partial(_prep_all_kernel, rope=rope, cache=cache),
        grid=(m // tr,),
        in_specs=in_specs,
        out_specs=[o[0] for o in outs],
        out_shape=[o[1] for o in outs],
        compiler_params=_params("arbitrary"),
        name="prep_all",
    )(*args)
    n_out = 4 if cache else 2
    return res[:n_out], res[n_out], res[n_out + 1:]


def _attn_kernel(q_ref, k_ref, v_ref, *rest, gps, rep, dqk, dv, tq, tk, ctx):
    if ctx:
        k2_ref, v2_ref, o_ref, acc_ref, s0_ref, s1_ref = rest
    else:
        o_ref, acc_ref, s0_ref, s1_ref = rest
    s_refs = (s0_ref, s1_ref)

    items = []
    for g in range(gps):
        chunks = [(k_ref, v_ref, c * tk, tk) for c in range(k_ref.shape[0] // tk)]
        if ctx:
            nctx = k2_ref.shape[0]
            chunks += [(k2_ref, v2_ref, lo, min(tk, nctx - lo)) for lo in range(0, nctx, tk)]
        items += [(g, c == 0, c == len(chunks) - 1) + ch for c, ch in enumerate(chunks)]

    def queries(g):
        cols = [q_ref[:, (g * rep + r) * dqk:(g * rep + r + 1) * dqk] for r in range(rep)]
        return cols[0] if rep == 1 else jnp.concatenate(cols, axis=0)

    def scores(t):
        g, _, _, kr, _, lo, size = items[t]
        s_refs[t % 2][:, :size] = lax.dot_general(
            queries(g), kr[lo:lo + size, g * dqk:(g + 1) * dqk], (((1,), (1,)), ((), ())),
            preferred_element_type=F32)

    scores(0)
    m = None
    for t, (g, first, last, _, vr, lo, size) in enumerate(items):
        if t + 1 < len(items):
            scores(t + 1)
        s = s_refs[t % 2][:, :size]
        m_blk = jnp.max(s, axis=-1, keepdims=True)
        m_new = m_blk if first else jnp.maximum(m, m_blk)
        p = jnp.exp2(s - m_new).astype(BF16)
        v_ext = jnp.concatenate([vr[lo:lo + size, g * dv:(g + 1) * dv], jnp.ones((size, dv), BF16)], axis=1)
        pv = jnp.dot(p, v_ext, preferred_element_type=F32)
        if first:
            acc_ref[g] = pv
        else:
            acc_ref[g] = jnp.exp2(m - m_new) * acc_ref[g] + pv
        m = m_new
        if last:
            o = acc_ref[g, :, :dv] / acc_ref[g, :, dv:]
            for r in range(rep):
                o_ref[:, (g * rep + r) * dv:(g * rep + r + 1) * dv] = o[r * tq:(r + 1) * tq].astype(o_ref.dtype)


def _attention(q, k, kblk0, v, vblk0, ctx_kv, batch, n, *, groups, gps, rep, dqk, dv, rows, name):
    tq = min(rows // rep, n)
    tk = min(512, n)
    nq = n // tq
    assert groups % gps == 0 and kblk0 % gps == 0 and vblk0 % gps == 0 and (ctx_kv is None or gps == 1)
    in_specs = [
        pl.BlockSpec((tq, gps * rep * dqk), lambda b, g, i: (b * nq + i, g)),
        pl.BlockSpec((n, gps * dqk), lambda b, g, i: (b, kblk0 // gps + g)),
        pl.BlockSpec((n, gps * dv), lambda b, g, i: (b, vblk0 // gps + g)),
    ]
    args = [q, k, v]
    if ctx_kv is not None:
        k2, v2 = ctx_kv
        nc = k2.shape[0] // batch
        in_specs += [pl.BlockSpec((nc, dqk), lambda b, g, i: (b, g)),
                     pl.BlockSpec((nc, dv), lambda b, g, i: (b, g))]
        args += [k2, v2]
    return pl.pallas_call(
        functools.partial(_attn_kernel, gps=gps, rep=rep, dqk=dqk, dv=dv, tq=tq, tk=tk, ctx=ctx_kv is not None),
        grid=(batch, groups // gps, nq),
        in_specs=in_specs,
        out_specs=pl.BlockSpec((tq, gps * rep * dv), lambda b, g, i: (b * nq + i, g)),
        out_shape=jax.ShapeDtypeStruct((batch * n, groups * rep * dv), BF16),
        scratch_shapes=[pltpu.VMEM((gps, rep * tq, 2 * dv), F32),
                        pltpu.VMEM((rep * tq, tk), F32),
                        pltpu.VMEM((rep * tq, tk), F32)],
        compiler_params=_params("arbitrary", "arbitrary", "arbitrary"),
        name=name,
    )(*args)


def _ret_head(lgf, lgb, io, tabs, nc, kvf_ref, kvb_ref, state_ref):
    c_len = B_CHUNK
    rope = tabs is not None
    ii = lax.broadcasted_iota(jnp.int32, (c_len, c_len), 0)
    jj = lax.broadcasted_iota(jnp.int32, (c_len, c_len), 1)
    diff = (ii - jj).astype(F32)
    dmat = (jnp.where(diff >= 0, jnp.exp(jnp.maximum(diff, 0.0) * lgf), 0.0)
            + jnp.where(diff <= 0, jnp.exp(jnp.maximum(-diff, 0.0) * lgb), 0.0))
    ri = lax.broadcasted_iota(jnp.int32, (c_len, 1), 0).astype(F32)
    qdec_f = jnp.exp((ri + 1.0) * lgf)
    qdec_b = jnp.exp((c_len - ri) * lgb)
    kdec_f = jnp.exp((c_len - 1.0 - ri) * lgf)
    kdec_b = jnp.exp(ri * lgb)
    one = jnp.ones((1, 1), F32)
    cdec_f = jnp.exp(one * (c_len * lgf))
    cdec_b = jnp.exp(one * (c_len * lgb))
    kscale = B_QK_DIM ** -0.5

    def roped(load, lo):
        x = load(lo).astype(F32)
        if rope:
            t = [tab[pl.ds(lo, c_len), :] for tab in tabs]
            x = _rope(x, t[0], t[1], t[2], B_QK_DIM // 4)
        return x

    def increments(c, carry):
        lo = pl.multiple_of(c * c_len, c_len)
        kc = roped(io["k"], lo) * kscale
        kk = jnp.concatenate([kc * kdec_f, kc * kdec_b], axis=1).T.astype(BF16)
        kv = jnp.dot(kk, io["v"](lo), preferred_element_type=F32)
        kvf_ref[c] = kv[:B_QK_DIM]
        kvb_ref[c] = kv[B_QK_DIM:]
        return carry

    lax.fori_loop(0, nc, increments, 0, unroll=_unroll(nc, 8))

    def scan_f(c, s):
        state_ref[c, :B_QK_DIM, :] = s.astype(BF16)
        return cdec_f * s + kvf_ref[c]

    def scan_b(t, s):
        c = nc - 1 - t
        state_ref[c, B_QK_DIM:, :] = s.astype(BF16)
        return cdec_b * s + kvb_ref[c]

    s_f = lax.fori_loop(0, nc, scan_f, io["s0f"], unroll=_unroll(nc, 2))
    s_b = lax.fori_loop(0, nc, scan_b, io["s0b"], unroll=_unroll(nc, 2))
    io["store_state"](s_f, s_b)

    def outputs(c, carry):
        lo = pl.multiple_of(c * c_len, c_len)
        qc = roped(io["q"], lo)
        kc = (roped(io["k"], lo) * kscale).astype(BF16)
        s = lax.dot_general(qc.astype(BF16), kc, (((1,), (1,)), ((), ())), preferred_element_type=F32) * dmat
        o = jnp.dot(s.astype(BF16), io["v"](lo), preferred_element_type=F32)
        qq = jnp.concatenate([qc * qdec_f, qc * qdec_b], axis=1).astype(BF16)
        o = o + jnp.dot(qq, state_ref[c], preferred_element_type=F32)
        gate = io["gate"](lo).astype(F32)
        io["store_o"](lo, (gate * jax.nn.sigmoid(gate) * _rms(o, io["gn"])).astype(BF16))
        return carry

    lax.fori_loop(0, nc, outputs, 0, unroll=_unroll(nc, 4))


def _ret_kernel(lg_ref, *refs, n, rope, emit_state, all_heads):
    refs = list(refs)
    data = [refs.pop(0) for _ in range(2 if all_heads else 4)]
    gn_ref, s0f_ref, s0b_ref = refs.pop(0), refs.pop(0), refs.pop(0)
    tabs = [refs.pop(0) for _ in range(3)] if rope else None
    o_ref = refs.pop(0)
    sfo_ref, sbo_ref = (refs.pop(0), refs.pop(0)) if emit_state else (None, None)
    kvf_ref, kvb_ref, state_ref = refs
    nc = n // B_CHUNK
    rows = lambda lo: pl.ds(lo, B_CHUNK)
    qw, vw = B_HEADS * B_QK_DIM, B_V_DIM

    for hh in range(B_HEADS if all_heads else 1):
        if all_heads:
            h = hh
            blk1, blk2 = data
            v_src = (blk1, 2 * qw + hh * vw) if hh < 2 else (blk2, (hh - 2) * vw)
            col = lambda ref, c0, w: (lambda lo: ref[rows(lo), c0:c0 + w])
            io = dict(
                q=col(blk1, hh * B_QK_DIM, B_QK_DIM), k=col(blk1, qw + hh * B_QK_DIM, B_QK_DIM),
                v=col(v_src[0], v_src[1], vw), gate=col(blk2, 2 * vw + hh * vw, vw),
                gn=gn_ref[:, hh * vw:(hh + 1) * vw], s0f=s0f_ref[hh], s0b=s0b_ref[hh])

            def store_o(lo, val, hh=hh):
                o_ref[rows(lo), hh * vw:(hh + 1) * vw] = val

            def store_state(sf, sb, hh=hh):
                if emit_state:
                    sfo_ref[hh] = sf
                    sbo_ref[hh] = sb
        else:
            h = pl.program_id(1)
            whole = lambda ref: (lambda lo: ref[rows(lo), :])
            io = dict(q=whole(data[0]), k=whole(data[1]), v=whole(data[2]), gate=whole(data[3]),
                      gn=gn_ref[...], s0f=s0f_ref[...], s0b=s0b_ref[...])

            def store_o(lo, val):
                o_ref[rows(lo), :] = val

            def store_state(sf, sb):
                if emit_state:
                    sfo_ref[...] = sf
                    sbo_ref[...] = sb
        io.update(store_o=store_o, store_state=store_state)
        _ret_head(lg_ref[0, h], lg_ref[1, h], io, tabs, nc, kvf_ref.at[hh], kvb_ref.at[hh], state_ref.at[hh])


def _retention(proj, lg, gn, s0f, s0b, layer, tabs, batch, n, *, emit_state, all_heads):
    rope = tabs is not None
    if all_heads:
        wide = (OFF_CQ - OFF_BQ) // 2
        assert OFF_BQ % wide == 0 and wide == 2 * B_HEADS * B_QK_DIM + 2 * B_V_DIM
        st_spec = pl.BlockSpec((None, B_HEADS, B_QK_DIM, B_V_DIM), lambda b, h, lg_: (b, 0, 0, 0))
        in_specs = [
            pl.BlockSpec((n, wide), lambda b, h, lg_: (b, OFF_BQ // wide)),
            pl.BlockSpec((n, wide), lambda b, h, lg_: (b, OFF_BQ // wide + 1)),
            pl.BlockSpec((None, 1, B_HEADS * B_V_DIM), lambda b, h, lg_: (layer, 0, 0)),
        ]
        args = [proj, proj, gn]
        out_specs = [pl.BlockSpec((n, B_HEADS * B_V_DIM), lambda b, h, lg_: (b, 0))]
        heads_per_step = B_HEADS
    else:
        qb, kb = OFF_BQ // B_QK_DIM, OFF_BK // B_QK_DIM
        vb, gb = OFF_BV // B_V_DIM, OFF_BG // B_V_DIM
        st_spec = pl.BlockSpec((None, None, B_QK_DIM, B_V_DIM), lambda b, h, lg_: (b, h, 0, 0))
        in_specs = [
            pl.BlockSpec((n, B_QK_DIM), lambda b, h, lg_: (b, qb + h)),
            pl.BlockSpec((n, B_QK_DIM), lambda b, h, lg_: (b, kb + h)),
            pl.BlockSpec((n, B_V_DIM), lambda b, h, lg_: (b, vb + h)),
            pl.BlockSpec((n, B_V_DIM), lambda b, h, lg_: (b, gb + h)),
            pl.BlockSpec((None, 1, B_V_DIM), lambda b, h, lg_: (layer, 0, h)),
        ]
        args = [proj, proj, proj, proj, gn]
        out_specs = [pl.BlockSpec((n, B_V_DIM), lambda b, h, lg_: (b, h))]
        heads_per_step = 1
    in_specs += [st_spec, st_spec]
    args += [s0f, s0b]
    if rope:
        in_specs += [pl.BlockSpec((n, LANE), lambda b, h, lg_: (0, 0))] * 3
        args += list(tabs)
    out_shape = [jax.ShapeDtypeStruct((batch * n, B_HEADS * B_V_DIM), BF16)]
    if emit_state:
        out_specs += [st_spec, st_spec]
        out_shape += [jax.ShapeDtypeStruct((batch, B_HEADS, B_QK_DIM, B_V_DIM), F32)] * 2
    nc = n // B_CHUNK
    return pl.pallas_call(
        functools.partial(_ret_kernel, n=n, rope=rope, emit_state=emit_state, all_heads=all_heads),
        grid_spec=pltpu.PrefetchScalarGridSpec(
            num_scalar_prefetch=1,
            grid=(batch, B_HEADS // heads_per_step),
            in_specs=in_specs,
            out_specs=out_specs,
            scratch_shapes=[pltpu.VMEM((heads_per_step, nc, B_QK_DIM, B_V_DIM), F32),
                            pltpu.VMEM((heads_per_step, nc, B_QK_DIM, B_V_DIM), F32),
                            pltpu.VMEM((heads_per_step, nc, 2 * B_QK_DIM, B_V_DIM), BF16)],
        ),
        out_shape=out_shape,
        compiler_params=_params("arbitrary", "arbitrary"),
        name="retention",
    )(lg, *args)


def _merge_kernel(oa_ref, ob_ref, oc_ref, wa_ref, wb_ref, wc_ref, ga_ref, gb_ref, gc_ref, o_ref):
    def branch(o, w, g):
        return jax.nn.sigmoid(g[...].astype(F32)) * jnp.dot(o[...], w[...], preferred_element_type=F32)

    o_ref[...] = (branch(oa_ref, wa_ref, ga_ref) + branch(ob_ref, wb_ref, gb_ref)
                  + branch(oc_ref, wc_ref, gc_ref)).astype(BF16)


def _merge(oa, ob, oc, wa, wb, wc, gates, layer, d):
    m = oa.shape[0]
    tm = min(1024, m)
    tn = min(1024, d)
    nb = d // tn
    o_spec = lambda o: pl.BlockSpec((tm, o.shape[1]), lambda i, j: (i, 0))
    w_spec = lambda w: pl.BlockSpec((None, w.shape[1], tn), lambda i, j: (layer, 0, j))
    g_spec = lambda br: pl.BlockSpec((tm, tn), lambda i, j: (i, br * nb + j))
    return pl.pallas_call(
        _merge_kernel,
        grid=(m // tm, nb),
        in_specs=[o_spec(oa), o_spec(ob), o_spec(oc), w_spec(wa), w_spec(wb), w_spec(wc),
                  g_spec(0), g_spec(1), g_spec(2)],
        out_specs=pl.BlockSpec((tm, tn), lambda i, j: (i, j)),
        out_shape=jax.ShapeDtypeStruct((m, d), BF16),
        compiler_params=_params("arbitrary", "arbitrary"),
        name="merge",
    )(oa, ob, oc, wa, wb, wc, gates, gates, gates)


def _rope_tables(n_tokens, dim, pad):
    pos = jnp.arange(n_tokens, dtype=jnp.int32)
    row = (pos // GRID_W).astype(F32)[:, None]
    col = (pos % GRID_W).astype(F32)[:, None]
    quarter = dim // 4
    inv = ROPE_THETA ** (-jnp.arange(quarter, dtype=F32) / quarter)[None, :]
    ang = jnp.concatenate([row * inv, row * inv, col * inv, col * inv], axis=1)
    cos, sin = jnp.cos(ang), jnp.sin(ang)
    first = (jnp.arange(dim) // quarter) % 2 == 0
    sin_lo = jnp.where(first[None, :], -sin, 0.0)
    sin_hi = jnp.where(first[None, :], 0.0, sin)
    if pad:
        z = jnp.zeros_like(cos)
        tabs = [jnp.concatenate([t, z], axis=1) for t in (cos, sin_lo, sin_hi)]
    else:
        tabs = [jnp.tile(t, (1, LANE // dim)) for t in (cos, sin_lo, sin_hi)]
    return tuple(tabs)


def kernel(x_prompt, x_sample, c, cache_attn_k, cache_attn_v, state_ret_fwd, state_ret_bwd, cache_mla_ckv, cache_mla_krope, c_ctx, w_mod, b_mod, g_pre_mix, g_post_mix, g_pre_mlp, g_post_mlp, w_in, attn_q_norm, attn_k_norm, ret_decay_fwd, ret_decay_bwd, ret_gn, mla_q_norm, mla_kv_norm, w_mla_uq, w_mla_ukv, w_branch_a, w_branch_b, w_branch_c, w_out, w_mlp_up, w_mlp_down):
    bp, n_p, d = x_prompt.shape
    bs, n_s, _ = x_sample.shape
    depth = w_in.shape[0]
    past = cache_attn_k.shape[2]
    assert n_p % B_CHUNK == 0 and n_s % B_CHUNK == 0

    w_in_t = jnp.transpose(w_in, (0, 2, 1))
    gate_w = w_in.shape[-1] - IN_MAIN
    w_uq = jnp.pad(w_mla_uq.reshape(depth, C_Q_RANK, C_HEADS, C_NOPE_DIM + C_ROPE_DIM),
                   ((0, 0), (0, 0), (0, 0), (0, C_QK_PAD - C_NOPE_DIM - C_ROPE_DIM)))
    w_uq = w_uq.reshape(depth, C_Q_RANK, C_HEADS * C_QK_PAD).astype(BF16)
    w_ukv = w_mla_ukv.astype(BF16)
    wa, wb, wc = w_branch_a.astype(BF16), w_branch_b.astype(BF16), w_branch_c.astype(BF16)
    w_o, w_dn = w_out.astype(BF16), w_mlp_down.astype(BF16)
    vec = lambda g: g.reshape(depth, 1, g.shape[-1])
    g_pre_mix, g_post_mix, g_pre_mlp, g_post_mlp = map(vec, (g_pre_mix, g_post_mix, g_pre_mlp, g_post_mlp))
    gqn, gkn, gcq, gckv, gret = map(vec, (attn_q_norm, attn_k_norm, mla_q_norm, mla_kv_norm, ret_gn))
    lg = jnp.stack([jax.nn.log_sigmoid(ret_decay_fwd.astype(F32)),
                    jax.nn.log_sigmoid(ret_decay_bwd.astype(F32))], axis=1)

    rows = -(-(1 + bs) // 8) * 8
    cvec = jnp.zeros((rows, d), F32).at[0].set(c_ctx).at[1:1 + bs].set(c)
    mod = _modulation(cvec, w_mod, b_mod).reshape(depth, rows, 6, d)
    mod = jnp.pad(mod, ((0, 0), (0, 0), (0, MOD_ROWS - 6), (0, 0)))

    tabs128 = _rope_tables(n_s, HEAD_DIM, pad=False)
    tabs64 = _rope_tables(n_s, C_ROPE_DIM, pad=False)
    tabs64p = _rope_tables(n_s, C_ROPE_DIM, pad=True)

    ctx_k = cache_attn_k.astype(BF16).reshape(bs, depth, past, A_KV_HEADS * HEAD_DIM)
    ctx_v = cache_attn_v.astype(BF16).reshape(bs, depth, past, A_KV_HEADS * HEAD_DIM)
    ctx_kr = jnp.pad(cache_mla_krope, ((0, 0), (0, 0), (0, 0), (0, LANE - C_ROPE_DIM)))
    zero_state = jnp.zeros((bp, B_HEADS, B_QK_DIM, B_V_DIM), F32)

    streams = {
        "p": dict(x=x_prompt.reshape(bp * n_p, d), batch=bp, n=n_p, grp=lambda r: 0, latent=False),
        "s": dict(x=x_sample.reshape(bs * n_s, d), batch=bs, n=n_s, grp=lambda r: 1 + r // n_s, latent=True),
    }
    for st in streams.values():
        st["h"] = _prenorm(st["x"], g_pre_mix, mod, 0, st["grp"])
    caches = [[] for _ in range(6)]

    for l in range(depth):
        for st in streams.values():
            batch, n, grp, latent = st["batch"], st["n"], st["grp"], st["latent"]
            proj = _matmul(st["h"], w_in_t, l, tm=2048, tn=512, w_transposed=True, n=MAIN_W, name="w_in")
            gates = _matmul(st["h"], w_in_t, l, tm=2048, tn=512, w_transposed=True, w_row0=IN_MAIN, n=gate_w,
                            name="w_gates")
            t128 = tabs128 if latent else None

            tabs = (tabs128, tabs64, tabs64p) if latent else None
            pa, qc, pc = _prep_all(proj, gqn, gkn, gcq, gckv, w_uq, w_ukv, l, tabs, n, cache=not latent)

            qa, ka = pa[0], pa[1]
            ctx_a = (ctx_k[:, l].reshape(bs * past, -1), ctx_v[:, l].reshape(bs * past, -1)) if latent else None
            o_a = _attention(qa, ka, 0, proj, OFF_AV // HEAD_DIM, ctx_a, batch, n,
                             groups=A_KV_HEADS, gps=1 if latent else A_KV_HEADS, rep=A_HEADS // A_KV_HEADS,
                             dqk=HEAD_DIM, dv=HEAD_DIM, rows=1024, name="attn_a")

            if latent:
                s0f, s0b = state_ret_fwd[:, l], state_ret_bwd[:, l]
            else:
                s0f = s0b = zero_state
            rb = _retention(proj, lg[l], gret, s0f, s0b, l, t128, batch, n,
                            emit_state=not latent, all_heads=not latent)
            o_b = rb[0]

            kc, vc = pc[0], pc[1]
            ctx_c = None
            if latent:
                ctx_c = _prep_ckv(cache_mla_ckv[:, l].reshape(bs * past, C_KV_RANK), 0,
                                  ctx_kr[:, l].reshape(bs * past, LANE), 0, gckv, w_ukv, l, None, past,
                                  norm=False, cache=False)
            o_c = _attention(qc, kc, 0, vc, 0, ctx_c, batch, n,
                             groups=C_HEADS, gps=1 if latent else C_HEADS, rep=1,
                             dqk=C_QK_PAD, dv=C_V_DIM, rows=1024, name="attn_c")

            if not latent:
                for dst, val in zip(caches, (pa[2], pa[3], rb[1], rb[2], pc[2], pc[3])):
                    dst.append(val)

            merged = _merge(o_a, o_b, o_c, wa, wb, wc, gates, l, d)
            x, h2 = _matmul_row(merged, w_o, st["x"], g_post_mix, g_pre_mlp, mod, l, l, grp,
                                gate_row=2, next_rows=(3, 4), tm=512, tk=2048, name="w_out")
            hid = _matmul(h2, w_mlp_up, l, tm=2048, tn=512, relu2=True, name="mlp_up")
            last = l == depth - 1
            res = _matmul_row(hid, w_dn, x, g_post_mlp, g_pre_mix, mod, l, min(l + 1, depth - 1), grp,
                              gate_row=5, next_rows=None if last else (0, 1), tm=512, tk=2048, name="mlp_down")
            st["x"] = res[0]
            st["h"] = None if last else res[1]

    y_p = streams["p"]["x"].reshape(bp, n_p, d)
    y_s = streams["s"]["x"].reshape(bs, n_s, d)
    nk, nv, sf, sb, ckv, kr = caches
    new_attn_k = jnp.stack(nk, axis=1).reshape(bp, n_p, depth, A_KV_HEADS, HEAD_DIM).transpose(0, 2, 1, 3, 4)
    new_attn_v = jnp.stack(nv, axis=1).reshape(bp, n_p, depth, A_KV_HEADS, HEAD_DIM).transpose(0, 2, 1, 3, 4)
    new_ret_fwd = jnp.stack(sf, axis=1)
    new_ret_bwd = jnp.stack(sb, axis=1)
    new_mla_ckv = jnp.stack(ckv, axis=1).reshape(bp, n_p, depth, C_KV_RANK).transpose(0, 2, 1, 3)
    new_mla_krope = jnp.stack(kr, axis=1).reshape(bp, n_p, depth, C_ROPE_DIM).transpose(0, 2, 1, 3)
    return (y_p, y_s, new_attn_k, new_attn_v, new_ret_fwd, new_ret_bwd, new_mla_ckv, new_mla_krope)
```

```python
import functools

import jax
import jax.numpy as jnp
from jax import lax
from jax.experimental import pallas as pl
from jax.experimental.pallas import tpu as pltpu

F32 = jnp.float32
BF16 = jnp.bfloat16

EPS = 1e-6
LOG2E = 1.4426950408889634
ROPE_THETA = 10000.0
GRID_W = 64
LANE = 128
HEAD_DIM = 128
A_HEADS = 8
A_KV_HEADS = 2
B_HEADS = 4
B_QK_DIM = 128
B_V_DIM = 256
B_CHUNK = 128
C_HEADS = 8
C_Q_RANK = 512
C_KV_RANK = 256
C_NOPE_DIM = 128
C_ROPE_DIM = 64
C_V_DIM = 128
C_QK_PAD = 256

OFF_AQ = 0
OFF_AK = OFF_AQ + A_HEADS * HEAD_DIM
OFF_AV = OFF_AK + A_KV_HEADS * HEAD_DIM
OFF_BQ = OFF_AV + A_KV_HEADS * HEAD_DIM
OFF_BK = OFF_BQ + B_HEADS * B_QK_DIM
OFF_BV = OFF_BK + B_HEADS * B_QK_DIM
OFF_BG = OFF_BV + B_HEADS * B_V_DIM
OFF_CQ = OFF_BG + B_HEADS * B_V_DIM
OFF_CKV = OFF_CQ + C_Q_RANK
OFF_CKR = OFF_CKV + C_KV_RANK
IN_MAIN = OFF_CKR + C_ROPE_DIM
MAIN_W = 5632
MOD_ROWS = 8
EPILOGUE_ROWS = 16

VMEM_LIMIT = 56 * 1024 * 1024


def _params(*sem):
    return pltpu.CompilerParams(dimension_semantics=sem, vmem_limit_bytes=VMEM_LIMIT)


def _rms(x, g):
    return x * lax.rsqrt(jnp.mean(x * x, axis=-1, keepdims=True) + EPS) * g


def _unroll(trips, want):
    while trips % want:
        want //= 2
    return want


def _rope(x, cos, sin_lo, sin_hi, shift):
    return (x * cos + pltpu.roll(x, LANE - shift, 1) * sin_lo + pltpu.roll(x, shift, 1) * sin_hi)


def _mod_kernel(c_ref, w_ref, b_ref, o_ref):
    c = c_ref[...]
    a = (c * jax.nn.sigmoid(c)).astype(BF16)
    o_ref[...] = jnp.dot(a, w_ref[...].astype(BF16), preferred_element_type=F32) + b_ref[...]


def _modulation(cvec, w_mod, b_mod):
    depth, d, n = w_mod.shape
    rows = cvec.shape[0]
    tn = min(1024, n)
    return pl.pallas_call(
        _mod_kernel,
        grid=(depth, n // tn),
        in_specs=[
            pl.BlockSpec((rows, d), lambda l, j: (0, 0)),
            pl.BlockSpec((None, d, tn), lambda l, j: (l, 0, j)),
            pl.BlockSpec((None, 1, tn), lambda l, j: (l, 0, j)),
        ],
        out_specs=pl.BlockSpec((None, rows, tn), lambda l, j: (l, 0, j)),
        out_shape=jax.ShapeDtypeStruct((depth, rows, n), F32),
        compiler_params=_params("arbitrary", "arbitrary"),
        name="modulation",
    )(cvec, w_mod, b_mod.reshape(depth, 1, n))


def _prenorm_kernel(x_ref, g_ref, mod_ref, h_ref):
    y = _rms(x_ref[...], g_ref[...])
    h_ref[...] = (y * (1.0 + mod_ref[1:2, :]) + mod_ref[0:1, :]).astype(BF16)


def _prenorm(x, g, mod, layer, grp):
    m, d = x.shape
    tm = min(512, m)
    return pl.pallas_call(
        _prenorm_kernel,
        grid=(m // tm,),
        in_specs=[
            pl.BlockSpec((tm, d), lambda i: (i, 0)),
            pl.BlockSpec((None, 1, d), lambda i: (layer, 0, 0)),
            pl.BlockSpec((None, None, MOD_ROWS, d), lambda i: (layer, grp(i * tm), 0, 0)),
        ],
        out_specs=pl.BlockSpec((tm, d), lambda i: (i, 0)),
        out_shape=jax.ShapeDtypeStruct((m, d), BF16),
        compiler_params=_params("arbitrary"),
        name="prenorm",
    )(x, g, mod)


def _mm_kernel(a_ref, w_ref, o_ref, *, relu2, w_transposed):
    w = (w_ref[0] if w_transposed else w_ref[...]).astype(BF16)
    contract = (((1,), (1 if w_transposed else 0,)), ((), ()))
    y = lax.dot_general(a_ref[...], w, contract, preferred_element_type=F32)
    if relu2:
        y = jnp.square(jnp.maximum(y, 0.0))
    o_ref[...] = y.astype(o_ref.dtype)


def _matmul(a, w, layer, *, tm, tn, relu2=False, w_transposed=False, w_row0=0, n=None, name):
    m, k = a.shape
    n = n if w_transposed else w.shape[2]
    tm, tn = min(tm, m), min(tn, n)
    if w_transposed:
        w_spec = pl.BlockSpec((pl.Element(1), pl.Element(tn), pl.Element(k)),
                              lambda i, j: (layer, pl.multiple_of(w_row0 + j * tn, 64), 0))
    else:
        w_spec = pl.BlockSpec((None, k, tn), lambda i, j: (layer, 0, j))
    return pl.pallas_call(
        functools.partial(_mm_kernel, relu2=relu2, w_transposed=w_transposed),
        grid=(m // tm, n // tn),
        in_specs=[
            pl.BlockSpec((tm, k), lambda i, j: (i, 0)),
            w_spec,
        ],
        out_specs=pl.BlockSpec((tm, tn), lambda i, j: (i, j)),
        out_shape=jax.ShapeDtypeStruct((m, n), BF16),
        compiler_params=_params("arbitrary", "arbitrary"),
        name=name,
    )(a, w)


def _mm_row_kernel(a_ref, w_ref, x_ref, gpost_ref, mod_ref, gnext_ref, modn_ref, *rest,
                   gate_row, next_rows, nk, nt):
    if next_rows is None:
        xo_ref, acc0_ref, acc1_ref = rest
    else:
        xo_ref, ho_ref, acc0_ref, acc1_ref = rest
    i = pl.program_id(0)
    k = pl.program_id(1)
    tm, d = acc0_ref.shape
    sub = tm // nk

    @pl.when((i == 0) & (k == 0))
    def _():
        acc0_ref[...] = jnp.zeros_like(acc0_ref)
        acc1_ref[...] = jnp.zeros_like(acc1_ref)

    def epilogue(acc_ref):
        gain = gpost_ref[...] * mod_ref[gate_row:gate_row + 1, :]
        if next_rows is not None:
            shift_row, scale_row = next_rows
            gain_next = gnext_ref[...] * (1.0 + modn_ref[scale_row:scale_row + 1, :])
            shift_next = modn_ref[shift_row:shift_row + 1, :]
        base = 0 if nk == 1 else pl.multiple_of(k * sub, sub)
        for c in range(sub // EPILOGUE_ROWS):
            rows = pl.ds(base + c * EPILOGUE_ROWS, EPILOGUE_ROWS)
            f = acc_ref[rows, :]
            acc_ref[rows, :] = jnp.zeros((EPILOGUE_ROWS, d), F32)
            xn = x_ref[rows, :] + _rms(f, gain)
            xo_ref[rows, :] = xn
            if next_rows is not None:
                ho_ref[rows, :] = (_rms(xn, gain_next) + shift_next).astype(BF16)

    for parity, (acc_mm, acc_ep) in enumerate(((acc0_ref, acc1_ref), (acc1_ref, acc0_ref))):
        @pl.when((i < nt) & (i % 2 == parity))
        def _():
            epilogue(acc_ep)
            acc_mm[...] += jnp.dot(a_ref[...], w_ref[...], preferred_element_type=F32)

    @pl.when(i == nt)
    def _():
        epilogue(acc0_ref if (nt - 1) % 2 == 0 else acc1_ref)


def _matmul_row(a, w, x, gpost, gnext, mod, layer, next_layer, grp, *, gate_row, next_rows, tm, tk, name):
    m, kdim = a.shape
    d = w.shape[-1]
    tm, tk = min(tm, m), min(tk, kdim)
    nk, nt = kdim // tk, m // tm
    assert tm % nk == 0 and (tm // nk) % 16 == 0
    cur = lambda i: jnp.minimum(i, nt - 1)
    kcur = lambda i, k: jnp.where(i < nt, k, nk - 1)
    prev = lambda i: jnp.maximum(i - 1, 0)
    vec = pl.BlockSpec((None, 1, d), lambda i, k: (layer, 0, 0))
    vecn = pl.BlockSpec((None, 1, d), lambda i, k: (next_layer, 0, 0))
    out_specs = [pl.BlockSpec((tm, d), lambda i, k: (prev(i), 0))]
    out_shape = [jax.ShapeDtypeStruct((m, d), F32)]
    if next_rows is not None:
        out_specs.append(pl.BlockSpec((tm, d), lambda i, k: (prev(i), 0)))
        out_shape.append(jax.ShapeDtypeStruct((m, d), BF16))
    return pl.pallas_call(
        functools.partial(_mm_row_kernel, gate_row=gate_row, next_rows=next_rows, nk=nk, nt=nt),
        grid=(nt + 1, nk),
        in_specs=[
            pl.BlockSpec((tm, tk), lambda i, k: (cur(i), kcur(i, k))),
            pl.BlockSpec((None, tk, d), lambda i, k: (layer, kcur(i, k), 0)),
            pl.BlockSpec((tm, d), lambda i, k: (prev(i), 0)),
            vec,
            pl.BlockSpec((None, None, MOD_ROWS, d), lambda i, k: (layer, grp(prev(i) * tm), 0, 0)),
            vecn,
            pl.BlockSpec((None, None, MOD_ROWS, d), lambda i, k: (next_layer, grp(prev(i) * tm), 0, 0)),
        ],
        out_specs=out_specs,
        out_shape=out_shape,
        scratch_shapes=[pltpu.VMEM((tm, d), F32), pltpu.VMEM((tm, d), F32)],
        compiler_params=_params("arbitrary", "arbitrary"),
        name=name,
    )(a, w, x, gpost, mod, gnext, mod)


def _prep_a_kernel(p_ref, gq_ref, gk_ref, *rest, rope, cache):
    rest = list(rest)
    tabs = [rest.pop(0) for _ in range(3)] if rope else None
    q_ref, k_ref = rest[:2]
    scale = HEAD_DIM ** -0.5 * LOG2E
    for h in range(A_HEADS):
        y = _rms(p_ref[:, h * HEAD_DIM:(h + 1) * HEAD_DIM].astype(F32), gq_ref[...])
        if rope:
            y = _rope(y, tabs[0][...], tabs[1][...], tabs[2][...], HEAD_DIM // 4)
        q_ref[:, h * HEAD_DIM:(h + 1) * HEAD_DIM] = (y * scale).astype(BF16)
    for g in range(A_KV_HEADS):
        lo = OFF_AK + g * HEAD_DIM
        y = _rms(p_ref[:, lo:lo + HEAD_DIM].astype(F32), gk_ref[...])
        if cache:
            rest[2][:, g * HEAD_DIM:(g + 1) * HEAD_DIM] = y
        if rope:
            y = _rope(y, tabs[0][...], tabs[1][...], tabs[2][...], HEAD_DIM // 4)
        k_ref[:, g * HEAD_DIM:(g + 1) * HEAD_DIM] = y.astype(BF16)
    if cache:
        rest[3][...] = p_ref[:, OFF_AV:OFF_BQ].astype(F32)


def _prep_cq_kernel(p_ref, g_ref, w_ref, *rest, rope):
    rest = list(rest)
    tabs = [rest.pop(0) for _ in range(3)] if rope else None
    q_ref = rest[0]
    scale = (C_NOPE_DIM + C_ROPE_DIM) ** -0.5 * LOG2E
    y = _rms(p_ref[...].astype(F32), g_ref[...]).astype(BF16)
    z = jnp.dot(y, w_ref[...], preferred_element_type=F32)
    for h in range(C_HEADS):
        lo = h * C_QK_PAD
        q_ref[:, lo:lo + C_NOPE_DIM] = (z[:, lo:lo + C_NOPE_DIM] * scale).astype(BF16)
        r = z[:, lo + C_NOPE_DIM:lo + C_QK_PAD]
        if rope:
            r = _rope(r, tabs[0][...], tabs[1][...], tabs[2][...], C_ROPE_DIM // 4)
        q_ref[:, lo + C_NOPE_DIM:lo + C_QK_PAD] = (r * scale).astype(BF16)


def _prep_ckv_kernel(ckv_ref, kr_ref, g_ref, w_ref, *rest, norm, rope, cache):
    rest = list(rest)
    tabs = [rest.pop(0) for _ in range(3)] if rope else None
    kc_ref, vc_ref = rest[:2]
    x = ckv_ref[...].astype(F32)
    if norm:
        x = _rms(x, g_ref[...])
    kr = kr_ref[...].astype(F32)
    if cache:
        rest[2][...] = x
        rest[3][...] = kr[:, :C_ROPE_DIM]
    if rope:
        kr = _rope(kr, tabs[0][...], tabs[1][...], tabs[2][...], C_ROPE_DIM // 4)
    krb = kr.astype(BF16)
    z = jnp.dot(x.astype(BF16), w_ref[...], preferred_element_type=F32)
    up = C_NOPE_DIM + C_V_DIM
    for h in range(C_HEADS):
        kc_ref[:, h * C_QK_PAD:h * C_QK_PAD + C_NOPE_DIM] = z[:, h * up:h * up + C_NOPE_DIM].astype(BF16)
        kc_ref[:, h * C_QK_PAD + C_NOPE_DIM:(h + 1) * C_QK_PAD] = krb
        vc_ref[:, h * C_V_DIM:(h + 1) * C_V_DIM] = z[:, h * up + C_NOPE_DIM:(h + 1) * up].astype(BF16)


def _prep_ctx_ckv(ckv, kr, g, w_ukv):
    depth, m, _ = ckv.shape
    tr = min(512, m)
    row = lambda w: pl.BlockSpec((None, tr, w), lambda l, i: (l, i, 0))
    return pl.pallas_call(
        functools.partial(_prep_ckv_kernel, norm=False, rope=False, cache=False),
        grid=(depth, m // tr),
        in_specs=[row(C_KV_RANK), row(LANE),
                  pl.BlockSpec((None, 1, C_KV_RANK), lambda l, i: (l, 0, 0)),
                  pl.BlockSpec((None,) + w_ukv.shape[1:], lambda l, i: (l, 0, 0))],
        out_specs=[row(C_HEADS * C_QK_PAD), row(C_HEADS * C_V_DIM)],
        out_shape=[jax.ShapeDtypeStruct((depth, m, C_HEADS * C_QK_PAD), BF16),
                   jax.ShapeDtypeStruct((depth, m, C_HEADS * C_V_DIM), BF16)],
        compiler_params=_params("arbitrary", "arbitrary"),
        name="prep_ctx_ckv",
    )(ckv, kr, g, w_ukv)


def _prep_all_kernel(pa_ref, pcq_ref, ckv_ref, kr_ref, gq_ref, gk_ref, gcq_ref, gckv_ref, wuq_ref, wukv_ref,
                     *rest, rope, cache, n_alias):
    rest = list(rest)
    t128, t64, t64p = ([rest.pop(0) for _ in range(3)] if rope else [] for _ in range(3))
    del rest[:n_alias]
    n_out = 4 if cache else 2
    a_out = [rest.pop(0) for _ in range(n_out)]
    cq_out = [rest.pop(0)]
    ckv_out = [rest.pop(0) for _ in range(n_out)]
    _prep_a_kernel(pa_ref, gq_ref, gk_ref, *t128, *a_out, rope=rope, cache=cache)
    _prep_cq_kernel(pcq_ref, gcq_ref, wuq_ref, *t64p, *cq_out, rope=rope)
    _prep_ckv_kernel(ckv_ref, kr_ref, gckv_ref, wukv_ref, *t64, *ckv_out, norm=True, rope=rope, cache=cache)


def _prep_all(proj, gq, gk, gcq, gckv, w_uq, w_ukv, layer, tabs, n_per, *, cache, depth=None, cache_bufs=None):
    m = proj.shape[0]
    tr = min(512, n_per)
    nt = n_per // tr
    rope = tabs is not None
    kvw = A_KV_HEADS * HEAD_DIM
    vec = lambda w: pl.BlockSpec((None, 1, w), lambda i: (layer, 0, 0))
    mat = lambda w: pl.BlockSpec((None,) + w.shape[1:], lambda i: (layer, 0, 0))
    row = lambda w, blk=0: pl.BlockSpec((tr, w), lambda i: (i, blk))
    in_specs = [row(OFF_BQ), row(C_Q_RANK, OFF_CQ // C_Q_RANK), row(C_KV_RANK, OFF_CKV // C_KV_RANK),
                row(LANE, OFF_CKR // LANE), vec(HEAD_DIM), vec(HEAD_DIM), vec(C_Q_RANK), vec(C_KV_RANK),
                mat(w_uq), mat(w_ukv)]
    args = [proj, proj, proj, proj, gq, gk, gcq, gckv, w_uq, w_ukv]
    if rope:
        in_specs += [pl.BlockSpec((tr, LANE), lambda i: (i % nt, 0))] * 9
        args += [t for group in tabs for t in group]
    out = lambda w, dt: (row(w), jax.ShapeDtypeStruct((m, w), dt))
    stacked = lambda w: (pl.BlockSpec((None, None, tr, w), lambda i: (i // nt, layer, i % nt, 0)),
                         jax.ShapeDtypeStruct((m // n_per, depth, n_per, w), F32))
    outs = [out(A_HEADS * HEAD_DIM, BF16), out(kvw, BF16)]
    outs += [stacked(kvw), stacked(kvw)] if cache else []
    outs += [out(C_HEADS * C_QK_PAD, BF16), out(C_HEADS * C_QK_PAD, BF16), out(C_HEADS * C_V_DIM, BF16)]
    outs += [stacked(C_KV_RANK), stacked(C_ROPE_DIM)] if cache else []
    aliases = {}
    if cache_bufs is not None:
        for buf, out_idx in zip(cache_bufs, (2, 3, 7, 8)):
            aliases[len(args)] = out_idx
            in_specs.append(pl.BlockSpec(memory_space=pl.ANY))
            args.append(buf)
    res = pl.pallas_call(
        functools.partial(_prep_all_kernel, rope=rope, cache=cache, n_alias=len(aliases)),
        grid=(m // tr,),
        in_specs=in_specs,
        out_specs=[o[0] for o in outs],
        out_shape=[o[1] for o in outs],
        input_output_aliases=aliases,
        compiler_params=_params("arbitrary"),
        name="prep_all",
    )(*args)
    n_out = 4 if cache else 2
    return res[:n_out], res[n_out], res[n_out + 1:]


def _attn_kernel(q_ref, k_ref, v_ref, *rest, gps, rep, dqk, dv, tq, tk, ctx):
    if ctx:
        k2_ref, v2_ref, o_ref, acc_ref, s0_ref, s1_ref = rest
    else:
        o_ref, acc_ref, s0_ref, s1_ref = rest
    s_refs = (s0_ref, s1_ref)

    items = []
    for g in range(gps):
        chunks = [(k_ref, v_ref, c * tk, tk) for c in range(k_ref.shape[0] // tk)]
        if ctx:
            nctx = k2_ref.shape[0]
            chunks += [(k2_ref, v2_ref, lo, min(tk, nctx - lo)) for lo in range(0, nctx, tk)]
        items += [(g, c == 0, c == len(chunks) - 1) + ch for c, ch in enumerate(chunks)]

    def queries(g):
        cols = [q_ref[:, (g * rep + r) * dqk:(g * rep + r + 1) * dqk] for r in range(rep)]
        return cols[0] if rep == 1 else jnp.concatenate(cols, axis=0)

    def scores(t):
        g, _, _, kr, _, lo, size = items[t]
        s_refs[t % 2][:, :size] = lax.dot_general(
            queries(g), kr[lo:lo + size, g * dqk:(g + 1) * dqk], (((1,), (1,)), ((), ())),
            preferred_element_type=F32)

    scores(0)
    m = None
    for t, (g, first, last, _, vr, lo, size) in enumerate(items):
        if t + 1 < len(items):
            scores(t + 1)
        s = s_refs[t % 2][:, :size]
        m_blk = jnp.max(s, axis=-1, keepdims=True)
        m_new = m_blk if first else jnp.maximum(m, m_blk)
        p = jnp.exp2(s - m_new).astype(BF16)
        v_ext = jnp.concatenate([vr[lo:lo + size, g * dv:(g + 1) * dv], jnp.ones((size, dv), BF16)], axis=1)
        pv = jnp.dot(p, v_ext, preferred_element_type=F32)
        if first:
            acc_ref[g] = pv
        else:
            acc_ref[g] = jnp.exp2(m - m_new) * acc_ref[g] + pv
        m = m_new
        if last:
            o = acc_ref[g, :, :dv] / acc_ref[g, :, dv:]
            for r in range(rep):
                o_ref[:, (g * rep + r) * dv:(g * rep + r + 1) * dv] = o[r * tq:(r + 1) * tq].astype(o_ref.dtype)


def _attention(q, k, kblk0, v, vblk0, ctx_kv, batch, n, *, groups, gps, rep, dqk, dv, rows, name, ctx_layer=0):
    tq = min(rows // rep, n)
    tk = min(512, n)
    nq = n // tq
    assert groups % gps == 0 and kblk0 % gps == 0 and vblk0 % gps == 0 and (ctx_kv is None or gps == 1)
    in_specs = [
        pl.BlockSpec((tq, gps * rep * dqk), lambda b, g, i: (b * nq + i, g)),
        pl.BlockSpec((n, gps * dqk), lambda b, g, i: (b, kblk0 // gps + g)),
        pl.BlockSpec((n, gps * dv), lambda b, g, i: (b, vblk0 // gps + g)),
    ]
    args = [q, k, v]
    if ctx_kv is not None:
        k2, v2 = ctx_kv
        nc = k2.shape[-2] // batch
        if k2.ndim == 3:
            in_specs += [pl.BlockSpec((None, nc, dqk), lambda b, g, i: (ctx_layer, b, g)),
                         pl.BlockSpec((None, nc, dv), lambda b, g, i: (ctx_layer, b, g))]
        else:
            in_specs += [pl.BlockSpec((nc, dqk), lambda b, g, i: (b, g)),
                         pl.BlockSpec((nc, dv), lambda b, g, i: (b, g))]
        args += [k2, v2]
    return pl.pallas_call(
        functools.partial(_attn_kernel, gps=gps, rep=rep, dqk=dqk, dv=dv, tq=tq, tk=tk, ctx=ctx_kv is not None),
        grid=(batch, groups // gps, nq),
        in_specs=in_specs,
        out_specs=pl.BlockSpec((tq, gps * rep * dv), lambda b, g, i: (b * nq + i, g)),
        out_shape=jax.ShapeDtypeStruct((batch * n, groups * rep * dv), BF16),
        scratch_shapes=[pltpu.VMEM((gps, rep * tq, 2 * dv), F32),
                        pltpu.VMEM((rep * tq, tk), F32),
                        pltpu.VMEM((rep * tq, tk), F32)],
        compiler_params=_params("arbitrary", "arbitrary", "arbitrary"),
        name=name,
    )(*args)


def _ret_head(lgf, lgb, io, tabs, nc, kvf_ref, kvb_ref, state_ref):
    c_len = B_CHUNK
    rope = tabs is not None
    ii = lax.broadcasted_iota(jnp.int32, (c_len, c_len), 0)
    jj = lax.broadcasted_iota(jnp.int32, (c_len, c_len), 1)
    diff = (ii - jj).astype(F32)
    dmat = (jnp.where(diff >= 0, jnp.exp(jnp.maximum(diff, 0.0) * lgf), 0.0)
            + jnp.where(diff <= 0, jnp.exp(jnp.maximum(-diff, 0.0) * lgb), 0.0))
    ri = lax.broadcasted_iota(jnp.int32, (c_len, 1), 0).astype(F32)
    qdec_f = jnp.exp((ri + 1.0) * lgf)
    qdec_b = jnp.exp((c_len - ri) * lgb)
    kdec_f = jnp.exp((c_len - 1.0 - ri) * lgf)
    kdec_b = jnp.exp(ri * lgb)
    one = jnp.ones((1, 1), F32)
    cdec_f = jnp.exp(one * (c_len * lgf))
    cdec_b = jnp.exp(one * (c_len * lgb))
    kscale = B_QK_DIM ** -0.5

    def roped(load, lo):
        x = load(lo).astype(F32)
        if rope:
            t = [tab[pl.ds(lo, c_len), :] for tab in tabs]
            x = _rope(x, t[0], t[1], t[2], B_QK_DIM // 4)
        return x

    def increments(c, carry):
        lo = pl.multiple_of(c * c_len, c_len)
        kc = roped(io["k"], lo) * kscale
        kk = jnp.concatenate([kc * kdec_f, kc * kdec_b], axis=1).T.astype(BF16)
        kv = jnp.dot(kk, io["v"](lo), preferred_element_type=F32)
        kvf_ref[c] = kv[:B_QK_DIM]
        kvb_ref[c] = kv[B_QK_DIM:]
        return carry

    lax.fori_loop(0, nc, increments, 0, unroll=_unroll(nc, 8))

    def scan_f(c, s):
        state_ref[c, :B_QK_DIM, :] = s.astype(BF16)
        return cdec_f * s + kvf_ref[c]

    def scan_b(t, s):
        c = nc - 1 - t
        state_ref[c, B_QK_DIM:, :] = s.astype(BF16)
        return cdec_b * s + kvb_ref[c]

    s_f = lax.fori_loop(0, nc, scan_f, io["s0f"], unroll=_unroll(nc, 2))
    s_b = lax.fori_loop(0, nc, scan_b, io["s0b"], unroll=_unroll(nc, 2))
    io["store_state"](s_f, s_b)

    def outputs(c, carry):
        lo = pl.multiple_of(c * c_len, c_len)
        qc = roped(io["q"], lo)
        kc = (roped(io["k"], lo) * kscale).astype(BF16)
        s = lax.dot_general(qc.astype(BF16), kc, (((1,), (1,)), ((), ())), preferred_element_type=F32) * dmat
        o = jnp.dot(s.astype(BF16), io["v"](lo), preferred_element_type=F32)
        qq = jnp.concatenate([qc * qdec_f, qc * qdec_b], axis=1).astype(BF16)
        o = o + jnp.dot(qq, state_ref[c], preferred_element_type=F32)
        gate = io["gate"](lo).astype(F32)
        io["store_o"](lo, (gate * jax.nn.sigmoid(gate) * _rms(o, io["gn"])).astype(BF16))
        return carry

    lax.fori_loop(0, nc, outputs, 0, unroll=_unroll(nc, 4))


def _ret_kernel(lg_ref, *refs, n, rope, emit_state, all_heads, n_alias):
    refs = list(refs)
    data = [refs.pop(0) for _ in range(2 if all_heads else 4)]
    gn_ref, s0f_ref, s0b_ref = refs.pop(0), refs.pop(0), refs.pop(0)
    tabs = [refs.pop(0) for _ in range(3)] if rope else None
    del refs[:n_alias]
    o_ref = refs.pop(0)
    sfo_ref, sbo_ref = (refs.pop(0), refs.pop(0)) if emit_state else (None, None)
    kvf_ref, kvb_ref, state_ref = refs
    nc = n // B_CHUNK
    rows = lambda lo: pl.ds(lo, B_CHUNK)
    qw, vw = B_HEADS * B_QK_DIM, B_V_DIM

    for hh in range(B_HEADS if all_heads else 1):
        if all_heads:
            h = hh
            blk1, blk2 = data
            v_src = (blk1, 2 * qw + hh * vw) if hh < 2 else (blk2, (hh - 2) * vw)
            col = lambda ref, c0, w: (lambda lo: ref[rows(lo), c0:c0 + w])
            io = dict(
                q=col(blk1, hh * B_QK_DIM, B_QK_DIM), k=col(blk1, qw + hh * B_QK_DIM, B_QK_DIM),
                v=col(v_src[0], v_src[1], vw), gate=col(blk2, 2 * vw + hh * vw, vw),
                gn=gn_ref[:, hh * vw:(hh + 1) * vw], s0f=s0f_ref[hh], s0b=s0b_ref[hh])

            def store_o(lo, val, hh=hh):
                o_ref[rows(lo), hh * vw:(hh + 1) * vw] = val

            def store_state(sf, sb, hh=hh):
                if emit_state:
                    sfo_ref[hh] = sf
                    sbo_ref[hh] = sb
        else:
            h = pl.program_id(1)
            whole = lambda ref: (lambda lo: ref[rows(lo), :])
            io = dict(q=whole(data[0]), k=whole(data[1]), v=whole(data[2]), gate=whole(data[3]),
                      gn=gn_ref[...], s0f=s0f_ref[...], s0b=s0b_ref[...])

            def store_o(lo, val):
                o_ref[rows(lo), :] = val

            def store_state(sf, sb):
                if emit_state:
                    sfo_ref[...] = sf
                    sbo_ref[...] = sb
        io.update(store_o=store_o, store_state=store_state)
        _ret_head(lg_ref[0, h], lg_ref[1, h], io, tabs, nc, kvf_ref.at[hh], kvb_ref.at[hh], state_ref.at[hh])


def _retention(proj, lg, gn, s0f, s0b, layer, tabs, batch, n, *, emit_state, all_heads, depth=None,
               state_bufs=None):
    rope = tabs is not None
    assert all_heads or not emit_state
    if all_heads:
        wide = (OFF_CQ - OFF_BQ) // 2
        assert OFF_BQ % wide == 0 and wide == 2 * B_HEADS * B_QK_DIM + 2 * B_V_DIM
        st_spec = pl.BlockSpec((None, B_HEADS, B_QK_DIM, B_V_DIM), lambda b, h, lg_: (b, 0, 0, 0))
        in_specs = [
            pl.BlockSpec((n, wide), lambda b, h, lg_: (b, OFF_BQ // wide)),
            pl.BlockSpec((n, wide), lambda b, h, lg_: (b, OFF_BQ // wide + 1)),
            pl.BlockSpec((None, 1, B_HEADS * B_V_DIM), lambda b, h, lg_: (layer, 0, 0)),
        ]
        args = [proj, proj, gn]
        out_specs = [pl.BlockSpec((n, B_HEADS * B_V_DIM), lambda b, h, lg_: (b, 0))]
        heads_per_step = B_HEADS
    else:
        qb, kb = OFF_BQ // B_QK_DIM, OFF_BK // B_QK_DIM
        vb, gb = OFF_BV // B_V_DIM, OFF_BG // B_V_DIM
        st_spec = pl.BlockSpec((None, None, B_QK_DIM, B_V_DIM), lambda b, h, lg_: (b, h, 0, 0))
        in_specs = [
            pl.BlockSpec((n, B_QK_DIM), lambda b, h, lg_: (b, qb + h)),
            pl.BlockSpec((n, B_QK_DIM), lambda b, h, lg_: (b, kb + h)),
            pl.BlockSpec((n, B_V_DIM), lambda b, h, lg_: (b, vb + h)),
            pl.BlockSpec((n, B_V_DIM), lambda b, h, lg_: (b, gb + h)),
            pl.BlockSpec((None, 1, B_V_DIM), lambda b, h, lg_: (layer, 0, h)),
        ]
        args = [proj, proj, proj, proj, gn]
        out_specs = [pl.BlockSpec((n, B_V_DIM), lambda b, h, lg_: (b, h))]
        heads_per_step = 1
    in_specs += [st_spec, st_spec]
    args += [s0f, s0b]
    if rope:
        in_specs += [pl.BlockSpec((n, LANE), lambda b, h, lg_: (0, 0))] * 3
        args += list(tabs)
    out_shape = [jax.ShapeDtypeStruct((batch * n, B_HEADS * B_V_DIM), BF16)]
    aliases = {}
    if emit_state:
        out_specs += [pl.BlockSpec((None, None, B_HEADS, B_QK_DIM, B_V_DIM),
                                   lambda b, h, lg_: (b, layer, 0, 0, 0))] * 2
        out_shape += [jax.ShapeDtypeStruct((batch, depth, B_HEADS, B_QK_DIM, B_V_DIM), F32)] * 2
        for j, buf in enumerate(state_bufs or ()):
            aliases[1 + len(args)] = 1 + j
            in_specs.append(pl.BlockSpec(memory_space=pl.ANY))
            args.append(buf)
    nc = n // B_CHUNK
    return pl.pallas_call(
        functools.partial(_ret_kernel, n=n, rope=rope, emit_state=emit_state, all_heads=all_heads,
                          n_alias=len(aliases)),
        grid_spec=pltpu.PrefetchScalarGridSpec(
            num_scalar_prefetch=1,
            grid=(batch, B_HEADS // heads_per_step),
            in_specs=in_specs,
            out_specs=out_specs,
            scratch_shapes=[pltpu.VMEM((heads_per_step, nc, B_QK_DIM, B_V_DIM), F32),
                            pltpu.VMEM((heads_per_step, nc, B_QK_DIM, B_V_DIM), F32),
                            pltpu.VMEM((heads_per_step, nc, 2 * B_QK_DIM, B_V_DIM), BF16)],
        ),
        out_shape=out_shape,
        input_output_aliases=aliases,
        compiler_params=_params("arbitrary", "arbitrary"),
        name="retention",
    )(lg, *args)


def _merge_kernel(oa_ref, ob_ref, oc_ref, wa_ref, wb_ref, wc_ref, ga_ref, gb_ref, gc_ref, o_ref):
    def branch(o, w, g):
        return jax.nn.sigmoid(g[...].astype(F32)) * jnp.dot(o[...], w[...], preferred_element_type=F32)

    o_ref[...] = (branch(oa_ref, wa_ref, ga_ref) + branch(ob_ref, wb_ref, gb_ref)
                  + branch(oc_ref, wc_ref, gc_ref)).astype(BF16)


def _merge(oa, ob, oc, wa, wb, wc, gates, layer, d):
    m = oa.shape[0]
    tm = min(1024, m)
    tn = min(1024, d)
    nb = d // tn
    o_spec = lambda o: pl.BlockSpec((tm, o.shape[1]), lambda i, j: (i, 0))
    w_spec = lambda w: pl.BlockSpec((None, w.shape[1], tn), lambda i, j: (layer, 0, j))
    g_spec = lambda br: pl.BlockSpec((tm, tn), lambda i, j: (i, br * nb + j))
    return pl.pallas_call(
        _merge_kernel,
        grid=(m // tm, nb),
        in_specs=[o_spec(oa), o_spec(ob), o_spec(oc), w_spec(wa), w_spec(wb), w_spec(wc),
                  g_spec(0), g_spec(1), g_spec(2)],
        out_specs=pl.BlockSpec((tm, tn), lambda i, j: (i, j)),
        out_shape=jax.ShapeDtypeStruct((m, d), BF16),
        compiler_params=_params("arbitrary", "arbitrary"),
        name="merge",
    )(oa, ob, oc, wa, wb, wc, gates, gates, gates)


def _rope_tables(n_tokens, dim, pad):
    pos = jnp.arange(n_tokens, dtype=jnp.int32)
    row = (pos // GRID_W).astype(F32)[:, None]
    col = (pos % GRID_W).astype(F32)[:, None]
    quarter = dim // 4
    inv = ROPE_THETA ** (-jnp.arange(quarter, dtype=F32) / quarter)[None, :]
    ang = jnp.concatenate([row * inv, row * inv, col * inv, col * inv], axis=1)
    cos, sin = jnp.cos(ang), jnp.sin(ang)
    first = (jnp.arange(dim) // quarter) % 2 == 0
    sin_lo = jnp.where(first[None, :], -sin, 0.0)
    sin_hi = jnp.where(first[None, :], 0.0, sin)
    if pad:
        z = jnp.zeros_like(cos)
        tabs = [jnp.concatenate([t, z], axis=1) for t in (cos, sin_lo, sin_hi)]
    else:
        tabs = [jnp.tile(t, (1, LANE // dim)) for t in (cos, sin_lo, sin_hi)]
    return tuple(tabs)


def kernel(x_prompt, x_sample, c, cache_attn_k, cache_attn_v, state_ret_fwd, state_ret_bwd, cache_mla_ckv, cache_mla_krope, c_ctx, w_mod, b_mod, g_pre_mix, g_post_mix, g_pre_mlp, g_post_mlp, w_in, attn_q_norm, attn_k_norm, ret_decay_fwd, ret_decay_bwd, ret_gn, mla_q_norm, mla_kv_norm, w_mla_uq, w_mla_ukv, w_branch_a, w_branch_b, w_branch_c, w_out, w_mlp_up, w_mlp_down):
    bp, n_p, d = x_prompt.shape
    bs, n_s, _ = x_sample.shape
    depth = w_in.shape[0]
    past = cache_attn_k.shape[2]
    assert n_p % B_CHUNK == 0 and n_s % B_CHUNK == 0

    w_in_t = jnp.transpose(w_in, (0, 2, 1))
    gate_w = w_in.shape[-1] - IN_MAIN
    w_uq = jnp.pad(w_mla_uq.reshape(depth, C_Q_RANK, C_HEADS, C_NOPE_DIM + C_ROPE_DIM),
                   ((0, 0), (0, 0), (0, 0), (0, C_QK_PAD - C_NOPE_DIM - C_ROPE_DIM)))
    w_uq = w_uq.reshape(depth, C_Q_RANK, C_HEADS * C_QK_PAD).astype(BF16)
    w_ukv = w_mla_ukv.astype(BF16)
    wa, wb, wc = w_branch_a.astype(BF16), w_branch_b.astype(BF16), w_branch_c.astype(BF16)
    w_o, w_dn = w_out.astype(BF16), w_mlp_down.astype(BF16)
    vec = lambda g: g.reshape(depth, 1, g.shape[-1])
    g_pre_mix, g_post_mix, g_pre_mlp, g_post_mlp = map(vec, (g_pre_mix, g_post_mix, g_pre_mlp, g_post_mlp))
    gqn, gkn, gcq, gckv, gret = map(vec, (attn_q_norm, attn_k_norm, mla_q_norm, mla_kv_norm, ret_gn))
    lg = jnp.stack([jax.nn.log_sigmoid(ret_decay_fwd.astype(F32)),
                    jax.nn.log_sigmoid(ret_decay_bwd.astype(F32))], axis=1)

    rows = -(-(1 + bs) // 8) * 8
    cvec = jnp.zeros((rows, d), F32).at[0].set(c_ctx).at[1:1 + bs].set(c)
    mod = _modulation(cvec, w_mod, b_mod).reshape(depth, rows, 6, d)
    mod = jnp.pad(mod, ((0, 0), (0, 0), (0, MOD_ROWS - 6), (0, 0)))

    tabs128 = _rope_tables(n_s, HEAD_DIM, pad=False)
    tabs64 = _rope_tables(n_s, C_ROPE_DIM, pad=False)
    tabs64p = _rope_tables(n_s, C_ROPE_DIM, pad=True)

    to_layers = lambda t: jnp.swapaxes(t.reshape(bs, depth, past, -1), 0, 1).reshape(depth, bs * past, -1)
    ctx_k, ctx_v = to_layers(cache_attn_k.astype(BF16)), to_layers(cache_attn_v.astype(BF16))
    ctx_kr = jnp.pad(cache_mla_krope, ((0, 0), (0, 0), (0, 0), (0, LANE - C_ROPE_DIM)))
    ctx_c = _prep_ctx_ckv(to_layers(cache_mla_ckv), to_layers(ctx_kr), gckv, w_ukv)
    zero_state = jnp.zeros((bp, B_HEADS, B_QK_DIM, B_V_DIM), F32)

    streams = {
        "p": dict(x=x_prompt.reshape(bp * n_p, d), batch=bp, n=n_p, grp=lambda r: 0, latent=False),
        "s": dict(x=x_sample.reshape(bs * n_s, d), batch=bs, n=n_s, grp=lambda r: 1 + r // n_s, latent=True),
    }
    for st in streams.values():
        st["h"] = _prenorm(st["x"], g_pre_mix, mod, 0, st["grp"])
    cache_bufs = state_bufs = None

    for l in range(depth):
        for st in streams.values():
            batch, n, grp, latent = st["batch"], st["n"], st["grp"], st["latent"]
            proj = _matmul(st["h"], w_in_t, l, tm=2048, tn=512, w_transposed=True, n=MAIN_W, name="w_in")
            gates = _matmul(st["h"], w_in_t, l, tm=2048, tn=512, w_transposed=True, w_row0=IN_MAIN, n=gate_w,
                            name="w_gates")
            t128 = tabs128 if latent else None

            tabs = (tabs128, tabs64, tabs64p) if latent else None
            pa, qc, pc = _prep_all(proj, gqn, gkn, gcq, gckv, w_uq, w_ukv, l, tabs, n, cache=not latent,
                                   depth=depth, cache_bufs=None if latent else cache_bufs)
            if not latent:
                cache_bufs = (pa[2], pa[3], pc[2], pc[3])

            qa, ka = pa[0], pa[1]
            ctx_a = (ctx_k, ctx_v) if latent else None
            o_a = _attention(qa, ka, 0, proj, OFF_AV // HEAD_DIM, ctx_a, batch, n,
                             groups=A_KV_HEADS, gps=1 if latent else A_KV_HEADS, rep=A_HEADS // A_KV_HEADS,
                             dqk=HEAD_DIM, dv=HEAD_DIM, rows=1024, name="attn_a", ctx_layer=l)

            if latent:
                s0f, s0b = state_ret_fwd[:, l], state_ret_bwd[:, l]
            else:
                s0f = s0b = zero_state
            rb = _retention(proj, lg[l], gret, s0f, s0b, l, t128, batch, n,
                            emit_state=not latent, all_heads=not latent, depth=depth,
                            state_bufs=None if latent else state_bufs)
            o_b = rb[0]
            if not latent:
                state_bufs = (rb[1], rb[2])

            kc, vc = pc[0], pc[1]
            o_c = _attention(qc, kc, 0, vc, 0, ctx_c if latent else None, batch, n,
                             groups=C_HEADS, gps=1 if latent else C_HEADS, rep=1,
                             dqk=C_QK_PAD, dv=C_V_DIM, rows=1024, name="attn_c", ctx_layer=l)

            merged = _merge(o_a, o_b, o_c, wa, wb, wc, gates, l, d)
            x, h2 = _matmul_row(merged, w_o, st["x"], g_post_mix, g_pre_mlp, mod, l, l, grp,
                                gate_row=2, next_rows=(3, 4), tm=512, tk=2048, name="w_out")
            hid = _matmul(h2, w_mlp_up, l, tm=2048, tn=512, relu2=True, name="mlp_up")
            last = l == depth - 1
            res = _matmul_row(hid, w_dn, x, g_post_mlp, g_pre_mix, mod, l, min(l + 1, depth - 1), grp,
                              gate_row=5, next_rows=None if last else (0, 1), tm=512, tk=2048, name="mlp_down")
            st["x"] = res[0]
            st["h"] = None if last else res[1]

    y_p = streams["p"]["x"].reshape(bp, n_p, d)
    y_s = streams["s"]["x"].reshape(bs, n_s, d)
    nk, nv, new_mla_ckv, new_mla_krope = cache_bufs
    new_attn_k = nk.reshape(bp, depth, n_p, A_KV_HEADS, HEAD_DIM)
    new_attn_v = nv.reshape(bp, depth, n_p, A_KV_HEADS, HEAD_DIM)
    new_ret_fwd, new_ret_bwd = state_bufs
    return (y_p, y_s, new_attn_k, new_attn_v, new_ret_fwd, new_ret_bwd, new_mla_ckv, new_mla_krope)
```

```python
import functools

import jax
import jax.numpy as jnp
from jax import lax
from jax.experimental import pallas as pl
from jax.experimental.pallas import tpu as pltpu

F32 = jnp.float32
BF16 = jnp.bfloat16

EPS = 1e-6
LOG2E = 1.4426950408889634
ROPE_THETA = 10000.0
GRID_W = 64
LANE = 128
HEAD_DIM = 128
A_HEADS = 8
A_KV_HEADS = 2
B_HEADS = 4
B_QK_DIM = 128
B_V_DIM = 256
B_CHUNK = 128
C_HEADS = 8
C_Q_RANK = 512
C_KV_RANK = 256
C_NOPE_DIM = 128
C_ROPE_DIM = 64
C_V_DIM = 128
C_QK_PAD = 256

OFF_AQ = 0
OFF_AK = OFF_AQ + A_HEADS * HEAD_DIM
OFF_AV = OFF_AK + A_KV_HEADS * HEAD_DIM
OFF_BQ = OFF_AV + A_KV_HEADS * HEAD_DIM
OFF_BK = OFF_BQ + B_HEADS * B_QK_DIM
OFF_BV = OFF_BK + B_HEADS * B_QK_DIM
OFF_BG = OFF_BV + B_HEADS * B_V_DIM
OFF_CQ = OFF_BG + B_HEADS * B_V_DIM
OFF_CKV = OFF_CQ + C_Q_RANK
OFF_CKR = OFF_CKV + C_KV_RANK
IN_MAIN = OFF_CKR + C_ROPE_DIM
MAIN_W = 5632
MOD_ROWS = 8
EPILOGUE_ROWS = 16

VMEM_LIMIT = 56 * 1024 * 1024


def _params(*sem):
    return pltpu.CompilerParams(dimension_semantics=sem, vmem_limit_bytes=VMEM_LIMIT)


def _rms(x, g):
    return x * lax.rsqrt(jnp.mean(x * x, axis=-1, keepdims=True) + EPS) * g


def _unroll(trips, want):
    while trips % want:
        want //= 2
    return want


def _rope(x, cos, sin_lo, sin_hi, shift):
    return (x * cos + pltpu.roll(x, LANE - shift, 1) * sin_lo + pltpu.roll(x, shift, 1) * sin_hi)


def _mod_kernel(c_ref, w_ref, b_ref, o_ref):
    c = c_ref[...]
    a = (c * jax.nn.sigmoid(c)).astype(BF16)
    o_ref[...] = jnp.dot(a, w_ref[...].astype(BF16), preferred_element_type=F32) + b_ref[...]


def _modulation(cvec, w_mod, b_mod):
    depth, d, n = w_mod.shape
    rows = cvec.shape[0]
    tn = min(1024, n)
    return pl.pallas_call(
        _mod_kernel,
        grid=(depth, n // tn),
        in_specs=[
            pl.BlockSpec((rows, d), lambda l, j: (0, 0)),
            pl.BlockSpec((None, d, tn), lambda l, j: (l, 0, j)),
            pl.BlockSpec((None, 1, tn), lambda l, j: (l, 0, j)),
        ],
        out_specs=pl.BlockSpec((None, rows, tn), lambda l, j: (l, 0, j)),
        out_shape=jax.ShapeDtypeStruct((depth, rows, n), F32),
        compiler_params=_params("arbitrary", "arbitrary"),
        name="modulation",
    )(cvec, w_mod, b_mod.reshape(depth, 1, n))


def _prenorm_kernel(x_ref, g_ref, mod_ref, h_ref):
    y = _rms(x_ref[...], g_ref[...])
    h_ref[...] = (y * (1.0 + mod_ref[1:2, :]) + mod_ref[0:1, :]).astype(BF16)


def _prenorm(x, g, mod, layer, grp):
    m, d = x.shape
    tm = min(512, m)
    return pl.pallas_call(
        _prenorm_kernel,
        grid=(m // tm,),
        in_specs=[
            pl.BlockSpec((tm, d), lambda i: (i, 0)),
            pl.BlockSpec((None, 1, d), lambda i: (layer, 0, 0)),
            pl.BlockSpec((None, None, MOD_ROWS, d), lambda i: (layer, grp(i * tm), 0, 0)),
        ],
        out_specs=pl.BlockSpec((tm, d), lambda i: (i, 0)),
        out_shape=jax.ShapeDtypeStruct((m, d), BF16),
        compiler_params=_params("arbitrary"),
        name="prenorm",
    )(x, g, mod)


def _mm_kernel(a_ref, w_ref, o_ref, *, relu2, w_transposed):
    w = (w_ref[0] if w_transposed else w_ref[...]).astype(BF16)
    contract = (((1,), (1 if w_transposed else 0,)), ((), ()))
    y = lax.dot_general(a_ref[...], w, contract, preferred_element_type=F32)
    if relu2:
        y = jnp.square(jnp.maximum(y, 0.0))
    o_ref[...] = y.astype(o_ref.dtype)


def _matmul(a, w, layer, *, tm, tn, relu2=False, w_transposed=False, w_row0=0, n=None, name):
    m, k = a.shape
    n = n if w_transposed else w.shape[2]
    tm, tn = min(tm, m), min(tn, n)
    if w_transposed:
        w_spec = pl.BlockSpec((pl.Element(1), pl.Element(tn), pl.Element(k)),
                              lambda i, j: (layer, pl.multiple_of(w_row0 + j * tn, 64), 0))
    else:
        w_spec = pl.BlockSpec((None, k, tn), lambda i, j: (layer, 0, j))
    return pl.pallas_call(
        functools.partial(_mm_kernel, relu2=relu2, w_transposed=w_transposed),
        grid=(m // tm, n // tn),
        in_specs=[
            pl.BlockSpec((tm, k), lambda i, j: (i, 0)),
            w_spec,
        ],
        out_specs=pl.BlockSpec((tm, tn), lambda i, j: (i, j)),
        out_shape=jax.ShapeDtypeStruct((m, n), BF16),
        compiler_params=_params("arbitrary", "arbitrary"),
        name=name,
    )(a, w)


def _mm_row_kernel(a_ref, w_ref, x_ref, gpost_ref, mod_ref, gnext_ref, modn_ref, *rest,
                   gate_row, next_rows, nk, nt):
    if next_rows is None:
        xo_ref, acc0_ref, acc1_ref = rest
    else:
        xo_ref, ho_ref, acc0_ref, acc1_ref = rest
    i = pl.program_id(0)
    k = pl.program_id(1)
    tm, d = acc0_ref.shape
    sub = tm // nk

    @pl.when((i == 0) & (k == 0))
    def _():
        acc0_ref[...] = jnp.zeros_like(acc0_ref)
        acc1_ref[...] = jnp.zeros_like(acc1_ref)

    def epilogue(acc_ref):
        gain = gpost_ref[...] * mod_ref[gate_row:gate_row + 1, :]
        if next_rows is not None:
            shift_row, scale_row = next_rows
            gain_next = gnext_ref[...] * (1.0 + modn_ref[scale_row:scale_row + 1, :])
            shift_next = modn_ref[shift_row:shift_row + 1, :]
        base = 0 if nk == 1 else pl.multiple_of(k * sub, sub)
        for c in range(sub // EPILOGUE_ROWS):
            rows = pl.ds(base + c * EPILOGUE_ROWS, EPILOGUE_ROWS)
            f = acc_ref[rows, :]
            acc_ref[rows, :] = jnp.zeros((EPILOGUE_ROWS, d), F32)
            blk = pl.ds(c * EPILOGUE_ROWS, EPILOGUE_ROWS)
            xn = x_ref[blk, :] + _rms(f, gain)
            xo_ref[blk, :] = xn
            if next_rows is not None:
                ho_ref[blk, :] = (_rms(xn, gain_next) + shift_next).astype(BF16)

    @pl.when(i == 0)
    def _():
        acc0_ref[...] += jnp.dot(a_ref[...], w_ref[...], preferred_element_type=F32)

    for parity, (acc_mm, acc_ep) in enumerate(((acc0_ref, acc1_ref), (acc1_ref, acc0_ref))):
        @pl.when((i > 0) & (i < nt) & (i % 2 == parity))
        def _():
            epilogue(acc_ep)
            acc_mm[...] += jnp.dot(a_ref[...], w_ref[...], preferred_element_type=F32)

    @pl.when(i == nt)
    def _():
        epilogue(acc0_ref if (nt - 1) % 2 == 0 else acc1_ref)


def _matmul_row(a, w, x, gpost, gnext, mod, layer, next_layer, grp, *, gate_row, next_rows, tm, tk, name):
    m, kdim = a.shape
    d = w.shape[-1]
    tm, tk = min(tm, m), min(tk, kdim)
    nk, nt = kdim // tk, m // tm
    assert tm % nk == 0 and (tm // nk) % 16 == 0
    cur = lambda i: jnp.minimum(i, nt - 1)
    kcur = lambda i, k: jnp.where(i < nt, k, nk - 1)
    prev = lambda i: jnp.maximum(i - 1, 0)
    sub = tm // nk
    slab = lambda i, k: (prev(i) * nk + jnp.where(i == 0, 0, k), 0)
    vec = pl.BlockSpec((None, 1, d), lambda i, k: (layer, 0, 0))
    vecn = pl.BlockSpec((None, 1, d), lambda i, k: (next_layer, 0, 0))
    out_specs = [pl.BlockSpec((sub, d), slab)]
    out_shape = [jax.ShapeDtypeStruct((m, d), F32)]
    if next_rows is not None:
        out_specs.append(pl.BlockSpec((sub, d), slab))
        out_shape.append(jax.ShapeDtypeStruct((m, d), BF16))
    return pl.pallas_call(
        functools.partial(_mm_row_kernel, gate_row=gate_row, next_rows=next_rows, nk=nk, nt=nt),
        grid=(nt + 1, nk),
        in_specs=[
            pl.BlockSpec((tm, tk), lambda i, k: (cur(i), kcur(i, k))),
            pl.BlockSpec((None, tk, d), lambda i, k: (layer, kcur(i, k), 0)),
            pl.BlockSpec((sub, d), slab),
            vec,
            pl.BlockSpec((None, None, MOD_ROWS, d), lambda i, k: (layer, grp(prev(i) * tm), 0, 0)),
            vecn,
            pl.BlockSpec((None, None, MOD_ROWS, d), lambda i, k: (next_layer, grp(prev(i) * tm), 0, 0)),
        ],
        out_specs=out_specs,
        out_shape=out_shape,
        scratch_shapes=[pltpu.VMEM((tm, d), F32), pltpu.VMEM((tm, d), F32)],
        compiler_params=_params("arbitrary", "arbitrary"),
        name=name,
    )(a, w, x, gpost, mod, gnext, mod)


def _prep_a_kernel(p_ref, gq_ref, gk_ref, *rest, rope, cache):
    rest = list(rest)
    tabs = [rest.pop(0) for _ in range(3)] if rope else None
    q_ref, k_ref = rest[:2]
    scale = HEAD_DIM ** -0.5 * LOG2E
    for h in range(A_HEADS):
        y = _rms(p_ref[:, h * HEAD_DIM:(h + 1) * HEAD_DIM].astype(F32), gq_ref[...])
        if rope:
            y = _rope(y, tabs[0][...], tabs[1][...], tabs[2][...], HEAD_DIM // 4)
        q_ref[:, h * HEAD_DIM:(h + 1) * HEAD_DIM] = (y * scale).astype(BF16)
    for g in range(A_KV_HEADS):
        lo = OFF_AK + g * HEAD_DIM
        y = _rms(p_ref[:, lo:lo + HEAD_DIM].astype(F32), gk_ref[...])
        if cache:
            rest[2][:, g * HEAD_DIM:(g + 1) * HEAD_DIM] = y
        if rope:
            y = _rope(y, tabs[0][...], tabs[1][...], tabs[2][...], HEAD_DIM // 4)
        k_ref[:, g * HEAD_DIM:(g + 1) * HEAD_DIM] = y.astype(BF16)
    if cache:
        rest[3][...] = p_ref[:, OFF_AV:OFF_BQ].astype(F32)


def _prep_cq_kernel(p_ref, g_ref, w_ref, *rest, rope):
    rest = list(rest)
    tabs = [rest.pop(0) for _ in range(3)] if rope else None
    q_ref = rest[0]
    scale = (C_NOPE_DIM + C_ROPE_DIM) ** -0.5 * LOG2E
    y = _rms(p_ref[...].astype(F32), g_ref[...]).astype(BF16)
    z = jnp.dot(y, w_ref[...], preferred_element_type=F32)
    for h in range(C_HEADS):
        lo = h * C_QK_PAD
        q_ref[:, lo:lo + C_NOPE_DIM] = (z[:, lo:lo + C_NOPE_DIM] * scale).astype(BF16)
        r = z[:, lo + C_NOPE_DIM:lo + C_QK_PAD]
        if rope:
            r = _rope(r, tabs[0][...], tabs[1][...], tabs[2][...], C_ROPE_DIM // 4)
        q_ref[:, lo + C_NOPE_DIM:lo + C_QK_PAD] = (r * scale).astype(BF16)


def _prep_ckv_kernel(ckv_ref, kr_ref, g_ref, w_ref, *rest, norm, rope, cache):
    rest = list(rest)
    tabs = [rest.pop(0) for _ in range(3)] if rope else None
    kc_ref, vc_ref = rest[:2]
    x = ckv_ref[...].astype(F32)
    if norm:
        x = _rms(x, g_ref[...])
    kr = kr_ref[...].astype(F32)
    if cache:
        rest[2][...] = x
        rest[3][...] = kr[:, :C_ROPE_DIM]
    if rope:
        kr = _rope(kr, tabs[0][...], tabs[1][...], tabs[2][...], C_ROPE_DIM // 4)
    krb = kr.astype(BF16)
    z = jnp.dot(x.astype(BF16), w_ref[...], preferred_element_type=F32)
    up = C_NOPE_DIM + C_V_DIM
    for h in range(C_HEADS):
        kc_ref[:, h * C_QK_PAD:h * C_QK_PAD + C_NOPE_DIM] = z[:, h * up:h * up + C_NOPE_DIM].astype(BF16)
        kc_ref[:, h * C_QK_PAD + C_NOPE_DIM:(h + 1) * C_QK_PAD] = krb
        vc_ref[:, h * C_V_DIM:(h + 1) * C_V_DIM] = z[:, h * up + C_NOPE_DIM:(h + 1) * up].astype(BF16)


def _prep_ctx_ckv(ckv, kr, g, w_ukv):
    depth, m, _ = ckv.shape
    tr = min(512, m)
    row = lambda w: pl.BlockSpec((None, tr, w), lambda l, i: (l, i, 0))
    return pl.pallas_call(
        functools.partial(_prep_ckv_kernel, norm=False, rope=False, cache=False),
        grid=(depth, m // tr),
        in_specs=[row(C_KV_RANK), row(LANE),
                  pl.BlockSpec((None, 1, C_KV_RANK), lambda l, i: (l, 0, 0)),
                  pl.BlockSpec((None,) + w_ukv.shape[1:], lambda l, i: (l, 0, 0))],
        out_specs=[row(C_HEADS * C_QK_PAD), row(C_HEADS * C_V_DIM)],
        out_shape=[jax.ShapeDtypeStruct((depth, m, C_HEADS * C_QK_PAD), BF16),
                   jax.ShapeDtypeStruct((depth, m, C_HEADS * C_V_DIM), BF16)],
        compiler_params=_params("arbitrary", "arbitrary"),
        name="prep_ctx_ckv",
    )(ckv, kr, g, w_ukv)


def _prep_all_kernel(pa_ref, pcq_ref, ckv_ref, kr_ref, gq_ref, gk_ref, gcq_ref, gckv_ref, wuq_ref, wukv_ref,
                     *rest, rope, cache, n_alias):
    rest = list(rest)
    t128, t64, t64p = ([rest.pop(0) for _ in range(3)] if rope else [] for _ in range(3))
    del rest[:n_alias]
    n_out = 4 if cache else 2
    a_out = [rest.pop(0) for _ in range(n_out)]
    cq_out = [rest.pop(0)]
    ckv_out = [rest.pop(0) for _ in range(n_out)]
    _prep_a_kernel(pa_ref, gq_ref, gk_ref, *t128, *a_out, rope=rope, cache=cache)
    _prep_cq_kernel(pcq_ref, gcq_ref, wuq_ref, *t64p, *cq_out, rope=rope)
    _prep_ckv_kernel(ckv_ref, kr_ref, gckv_ref, wukv_ref, *t64, *ckv_out, norm=True, rope=rope, cache=cache)


def _prep_all(proj, gq, gk, gcq, gckv, w_uq, w_ukv, layer, tabs, n_per, *, cache, depth=None, cache_bufs=None):
    m = proj.shape[0]
    tr = min(512, n_per)
    nt = n_per // tr
    rope = tabs is not None
    kvw = A_KV_HEADS * HEAD_DIM
    vec = lambda w: pl.BlockSpec((None, 1, w), lambda i: (layer, 0, 0))
    mat = lambda w: pl.BlockSpec((None,) + w.shape[1:], lambda i: (layer, 0, 0))
    row = lambda w, blk=0: pl.BlockSpec((tr, w), lambda i: (i, blk))
    in_specs = [row(OFF_BQ), row(C_Q_RANK, OFF_CQ // C_Q_RANK), row(C_KV_RANK, OFF_CKV // C_KV_RANK),
                row(LANE, OFF_CKR // LANE), vec(HEAD_DIM), vec(HEAD_DIM), vec(C_Q_RANK), vec(C_KV_RANK),
                mat(w_uq), mat(w_ukv)]
    args = [proj, proj, proj, proj, gq, gk, gcq, gckv, w_uq, w_ukv]
    if rope:
        in_specs += [pl.BlockSpec((tr, LANE), lambda i: (i % nt, 0))] * 9
        args += [t for group in tabs for t in group]
    out = lambda w, dt: (row(w), jax.ShapeDtypeStruct((m, w), dt))
    stacked = lambda w: (pl.BlockSpec((None, None, tr, w), lambda i: (i // nt, layer, i % nt, 0)),
                         jax.ShapeDtypeStruct((m // n_per, depth, n_per, w), F32))
    outs = [out(A_HEADS * HEAD_DIM, BF16), out(kvw, BF16)]
    outs += [stacked(kvw), stacked(kvw)] if cache else []
    outs += [out(C_HEADS * C_QK_PAD, BF16), out(C_HEADS * C_QK_PAD, BF16), out(C_HEADS * C_V_DIM, BF16)]
    outs += [stacked(C_KV_RANK), stacked(C_ROPE_DIM)] if cache else []
    aliases = {}
    if cache_bufs is not None:
        for buf, out_idx in zip(cache_bufs, (2, 3, 7, 8)):
            aliases[len(args)] = out_idx
            in_specs.append(pl.BlockSpec(memory_space=pl.ANY))
            args.append(buf)
    res = pl.pallas_call(
        functools.partial(_prep_all_kernel, rope=rope, cache=cache, n_alias=len(aliases)),
        grid=(m // tr,),
        in_specs=in_specs,
        out_specs=[o[0] for o in outs],
        out_shape=[o[1] for o in outs],
        input_output_aliases=aliases,
        compiler_params=_params("arbitrary"),
        name="prep_all",
    )(*args)
    n_out = 4 if cache else 2
    return res[:n_out], res[n_out], res[n_out + 1:]


def _attn_kernel(q_ref, k_ref, v_ref, *rest, gps, rep, dqk, dv, tq, tk, ctx):
    if ctx:
        k2_ref, v2_ref, o_ref, acc_ref, s0_ref, s1_ref = rest
    else:
        o_ref, acc_ref, s0_ref, s1_ref = rest
    s_refs = (s0_ref, s1_ref)

    items = []
    for g in range(gps):
        chunks = [(k_ref, v_ref, c * tk, tk) for c in range(k_ref.shape[0] // tk)]
        if ctx:
            nctx = k2_ref.shape[0]
            chunks += [(k2_ref, v2_ref, lo, min(tk, nctx - lo)) for lo in range(0, nctx, tk)]
        items += [(g, c == 0, c == len(chunks) - 1) + ch for c, ch in enumerate(chunks)]

    def queries(g):
        cols = [q_ref[:, (g * rep + r) * dqk:(g * rep + r + 1) * dqk] for r in range(rep)]
        return cols[0] if rep == 1 else jnp.concatenate(cols, axis=0)

    def scores(t):
        g, _, _, kr, _, lo, size = items[t]
        s_refs[t % 2][:, :size] = lax.dot_general(
            queries(g), kr[lo:lo + size, g * dqk:(g + 1) * dqk], (((1,), (1,)), ((), ())),
            preferred_element_type=F32)

    scores(0)
    m = None
    for t, (g, first, last, _, vr, lo, size) in enumerate(items):
        if t + 1 < len(items):
            scores(t + 1)
        s = s_refs[t % 2][:, :size]
        m_blk = jnp.max(s, axis=-1, keepdims=True)
        m_new = m_blk if first else jnp.maximum(m, m_blk)
        p = jnp.exp2(s - m_new).astype(BF16)
        v_ext = jnp.concatenate([vr[lo:lo + size, g * dv:(g + 1) * dv], jnp.ones((size, dv), BF16)], axis=1)
        pv = jnp.dot(p, v_ext, preferred_element_type=F32)
        if first:
            acc_ref[g] = pv
        else:
            acc_ref[g] = jnp.exp2(m - m_new) * acc_ref[g] + pv
        m = m_new
        if last:
            o = acc_ref[g, :, :dv] / acc_ref[g, :, dv:]
            for r in range(rep):
                o_ref[:, (g * rep + r) * dv:(g * rep + r + 1) * dv] = o[r * tq:(r + 1) * tq].astype(o_ref.dtype)


def _attention(q, k, kblk0, v, vblk0, ctx_kv, batch, n, *, groups, gps, rep, dqk, dv, rows, name, ctx_layer=0):
    tq = min(rows // rep, n)
    tk = min(512, n)
    nq = n // tq
    assert groups % gps == 0 and kblk0 % gps == 0 and vblk0 % gps == 0 and (ctx_kv is None or gps == 1)
    in_specs = [
        pl.BlockSpec((tq, gps * rep * dqk), lambda b, g, i: (b * nq + i, g)),
        pl.BlockSpec((n, gps * dqk), lambda b, g, i: (b, kblk0 // gps + g)),
        pl.BlockSpec((n, gps * dv), lambda b, g, i: (b, vblk0 // gps + g)),
    ]
    args = [q, k, v]
    if ctx_kv is not None:
        k2, v2 = ctx_kv
        nc = k2.shape[-2] // batch
        if k2.ndim == 3:
            in_specs += [pl.BlockSpec((None, nc, dqk), lambda b, g, i: (ctx_layer, b, g)),
                         pl.BlockSpec((None, nc, dv), lambda b, g, i: (ctx_layer, b, g))]
        else:
            in_specs += [pl.BlockSpec((nc, dqk), lambda b, g, i: (b, g)),
                         pl.BlockSpec((nc, dv), lambda b, g, i: (b, g))]
        args += [k2, v2]
    return pl.pallas_call(
        functools.partial(_attn_kernel, gps=gps, rep=rep, dqk=dqk, dv=dv, tq=tq, tk=tk, ctx=ctx_kv is not None),
        grid=(batch, groups // gps, nq),
        in_specs=in_specs,
        out_specs=pl.BlockSpec((tq, gps * rep * dv), lambda b, g, i: (b * nq + i, g)),
        out_shape=jax.ShapeDtypeStruct((batch * n, groups * rep * dv), BF16),
        scratch_shapes=[pltpu.VMEM((gps, rep * tq, 2 * dv), F32),
                        pltpu.VMEM((rep * tq, tk), F32),
                        pltpu.VMEM((rep * tq, tk), F32)],
        compiler_params=_params("arbitrary", "arbitrary", "arbitrary"),
        name=name,
    )(*args)


def _ret_head(lgf, lgb, io, tabs, nc, kvf_ref, kvb_ref, state_ref):
    c_len = B_CHUNK
    rope = tabs is not None
    ii = lax.broadcasted_iota(jnp.int32, (c_len, c_len), 0)
    jj = lax.broadcasted_iota(jnp.int32, (c_len, c_len), 1)
    diff = (ii - jj).astype(F32)
    dmat = (jnp.where(diff >= 0, jnp.exp(jnp.maximum(diff, 0.0) * lgf), 0.0)
            + jnp.where(diff <= 0, jnp.exp(jnp.maximum(-diff, 0.0) * lgb), 0.0))
    ri = lax.broadcasted_iota(jnp.int32, (c_len, 1), 0).astype(F32)
    qdec_f = jnp.exp((ri + 1.0) * lgf)
    qdec_b = jnp.exp((c_len - ri) * lgb)
    kdec_f = jnp.exp((c_len - 1.0 - ri) * lgf)
    kdec_b = jnp.exp(ri * lgb)
    one = jnp.ones((1, 1), F32)
    cdec_f = jnp.exp(one * (c_len * lgf))
    cdec_b = jnp.exp(one * (c_len * lgb))
    kscale = B_QK_DIM ** -0.5

    def roped(load, lo):
        x = load(lo).astype(F32)
        if rope:
            t = [tab[pl.ds(lo, c_len), :] for tab in tabs]
            x = _rope(x, t[0], t[1], t[2], B_QK_DIM // 4)
        return x

    def increments(c, carry):
        lo = pl.multiple_of(c * c_len, c_len)
        kc = roped(io["k"], lo) * kscale
        kk = jnp.concatenate([kc * kdec_f, kc * kdec_b], axis=1).T.astype(BF16)
        kv = jnp.dot(kk, io["v"](lo), preferred_element_type=F32)
        kvf_ref[c] = kv[:B_QK_DIM]
        kvb_ref[c] = kv[B_QK_DIM:]
        return carry

    lax.fori_loop(0, nc, increments, 0, unroll=_unroll(nc, 8))

    def scan_f(c, s):
        state_ref[c, :B_QK_DIM, :] = s.astype(BF16)
        return cdec_f * s + kvf_ref[c]

    def scan_b(t, s):
        c = nc - 1 - t
        state_ref[c, B_QK_DIM:, :] = s.astype(BF16)
        return cdec_b * s + kvb_ref[c]

    s_f = lax.fori_loop(0, nc, scan_f, io["s0f"], unroll=_unroll(nc, 2))
    s_b = lax.fori_loop(0, nc, scan_b, io["s0b"], unroll=_unroll(nc, 2))
    io["store_state"](s_f, s_b)

    def outputs(c, carry):
        lo = pl.multiple_of(c * c_len, c_len)
        qc = roped(io["q"], lo)
        kc = (roped(io["k"], lo) * kscale).astype(BF16)
        s = lax.dot_general(qc.astype(BF16), kc, (((1,), (1,)), ((), ())), preferred_element_type=F32) * dmat
        o = jnp.dot(s.astype(BF16), io["v"](lo), preferred_element_type=F32)
        qq = jnp.concatenate([qc * qdec_f, qc * qdec_b], axis=1).astype(BF16)
        o = o + jnp.dot(qq, state_ref[c], preferred_element_type=F32)
        gate = io["gate"](lo).astype(F32)
        io["store_o"](lo, (gate * jax.nn.sigmoid(gate) * _rms(o, io["gn"])).astype(BF16))
        return carry

    lax.fori_loop(0, nc, outputs, 0, unroll=_unroll(nc, 4))


def _ret_kernel(lg_ref, *refs, n, rope, emit_state, all_heads, n_alias):
    refs = list(refs)
    data = [refs.pop(0) for _ in range(2 if all_heads else 4)]
    gn_ref, s0f_ref, s0b_ref = refs.pop(0), refs.pop(0), refs.pop(0)
    tabs = [refs.pop(0) for _ in range(3)] if rope else None
    del refs[:n_alias]
    o_ref = refs.pop(0)
    sfo_ref, sbo_ref = (refs.pop(0), refs.pop(0)) if emit_state else (None, None)
    kvf_ref, kvb_ref, state_ref = refs
    nc = n // B_CHUNK
    rows = lambda lo: pl.ds(lo, B_CHUNK)
    qw, vw = B_HEADS * B_QK_DIM, B_V_DIM

    for hh in range(B_HEADS if all_heads else 1):
        if all_heads:
            h = hh
            blk1, blk2 = data
            v_src = (blk1, 2 * qw + hh * vw) if hh < 2 else (blk2, (hh - 2) * vw)
            col = lambda ref, c0, w: (lambda lo: ref[rows(lo), c0:c0 + w])
            io = dict(
                q=col(blk1, hh * B_QK_DIM, B_QK_DIM), k=col(blk1, qw + hh * B_QK_DIM, B_QK_DIM),
                v=col(v_src[0], v_src[1], vw), gate=col(blk2, 2 * vw + hh * vw, vw),
                gn=gn_ref[:, hh * vw:(hh + 1) * vw], s0f=s0f_ref[hh], s0b=s0b_ref[hh])

            def store_o(lo, val, hh=hh):
                o_ref[rows(lo), hh * vw:(hh + 1) * vw] = val

            def store_state(sf, sb, hh=hh):
                if emit_state:
                    sfo_ref[hh] = sf
                    sbo_ref[hh] = sb
        else:
            h = pl.program_id(1)
            whole = lambda ref: (lambda lo: ref[rows(lo), :])
            io = dict(q=whole(data[0]), k=whole(data[1]), v=whole(data[2]), gate=whole(data[3]),
                      gn=gn_ref[...], s0f=s0f_ref[...], s0b=s0b_ref[...])

            def store_o(lo, val):
                o_ref[rows(lo), :] = val

            def store_state(sf, sb):
                if emit_state:
                    sfo_ref[...] = sf
                    sbo_ref[...] = sb
        io.update(store_o=store_o, store_state=store_state)
        _ret_head(lg_ref[0, h], lg_ref[1, h], io, tabs, nc, kvf_ref.at[hh], kvb_ref.at[hh], state_ref.at[hh])


def _retention(proj, lg, gn, s0f, s0b, layer, tabs, batch, n, *, emit_state, all_heads, depth=None,
               state_bufs=None):
    rope = tabs is not None
    assert all_heads or not emit_state
    if all_heads:
        wide = (OFF_CQ - OFF_BQ) // 2
        assert OFF_BQ % wide == 0 and wide == 2 * B_HEADS * B_QK_DIM + 2 * B_V_DIM
        st_spec = pl.BlockSpec((None, B_HEADS, B_QK_DIM, B_V_DIM), lambda b, h, lg_: (b, 0, 0, 0))
        in_specs = [
            pl.BlockSpec((n, wide), lambda b, h, lg_: (b, OFF_BQ // wide)),
            pl.BlockSpec((n, wide), lambda b, h, lg_: (b, OFF_BQ // wide + 1)),
            pl.BlockSpec((None, 1, B_HEADS * B_V_DIM), lambda b, h, lg_: (layer, 0, 0)),
        ]
        args = [proj, proj, gn]
        out_specs = [pl.BlockSpec((n, B_HEADS * B_V_DIM), lambda b, h, lg_: (b, 0))]
        heads_per_step = B_HEADS
    else:
        qb, kb = OFF_BQ // B_QK_DIM, OFF_BK // B_QK_DIM
        vb, gb = OFF_BV // B_V_DIM, OFF_BG // B_V_DIM
        st_spec = pl.BlockSpec((None, None, B_QK_DIM, B_V_DIM), lambda b, h, lg_: (b, h, 0, 0))
        in_specs = [
            pl.BlockSpec((n, B_QK_DIM), lambda b, h, lg_: (b, qb + h)),
            pl.BlockSpec((n, B_QK_DIM), lambda b, h, lg_: (b, kb + h)),
            pl.BlockSpec((n, B_V_DIM), lambda b, h, lg_: (b, vb + h)),
            pl.BlockSpec((n, B_V_DIM), lambda b, h, lg_: (b, gb + h)),
            pl.BlockSpec((None, 1, B_V_DIM), lambda b, h, lg_: (layer, 0, h)),
        ]
        args = [proj, proj, proj, proj, gn]
        out_specs = [pl.BlockSpec((n, B_V_DIM), lambda b, h, lg_: (b, h))]
        heads_per_step = 1
    in_specs += [st_spec, st_spec]
    args += [s0f, s0b]
    if rope:
        in_specs += [pl.BlockSpec((n, LANE), lambda b, h, lg_: (0, 0))] * 3
        args += list(tabs)
    out_shape = [jax.ShapeDtypeStruct((batch * n, B_HEADS * B_V_DIM), BF16)]
    aliases = {}
    if emit_state:
        out_specs += [pl.BlockSpec((None, None, B_HEADS, B_QK_DIM, B_V_DIM),
                                   lambda b, h, lg_: (b, layer, 0, 0, 0))] * 2
        out_shape += [jax.ShapeDtypeStruct((batch, depth, B_HEADS, B_QK_DIM, B_V_DIM), F32)] * 2
        for j, buf in enumerate(state_bufs or ()):
            aliases[1 + len(args)] = 1 + j
            in_specs.append(pl.BlockSpec(memory_space=pl.ANY))
            args.append(buf)
    nc = n // B_CHUNK
    return pl.pallas_call(
        functools.partial(_ret_kernel, n=n, rope=rope, emit_state=emit_state, all_heads=all_heads,
                          n_alias=len(aliases)),
        grid_spec=pltpu.PrefetchScalarGridSpec(
            num_scalar_prefetch=1,
            grid=(batch, B_HEADS // heads_per_step),
            in_specs=in_specs,
            out_specs=out_specs,
            scratch_shapes=[pltpu.VMEM((heads_per_step, nc, B_QK_DIM, B_V_DIM), F32),
                            pltpu.VMEM((heads_per_step, nc, B_QK_DIM, B_V_DIM), F32),
                            pltpu.VMEM((heads_per_step, nc, 2 * B_QK_DIM, B_V_DIM), BF16)],
        ),
        out_shape=out_shape,
        input_output_aliases=aliases,
        compiler_params=_params("arbitrary", "arbitrary"),
        name="retention",
    )(lg, *args)


def _merge_kernel(oa_ref, ob_ref, oc_ref, wa_ref, wb_ref, wc_ref, ga_ref, gb_ref, gc_ref, o_ref):
    def branch(o, w, g):
        return jax.nn.sigmoid(g[...].astype(F32)) * jnp.dot(o[...], w[...], preferred_element_type=F32)

    o_ref[...] = (branch(oa_ref, wa_ref, ga_ref) + branch(ob_ref, wb_ref, gb_ref)
                  + branch(oc_ref, wc_ref, gc_ref)).astype(BF16)


def _merge(oa, ob, oc, wa, wb, wc, gates, layer, d):
    m = oa.shape[0]
    tm = min(1024, m)
    tn = min(1024, d)
    nb = d // tn
    o_spec = lambda o: pl.BlockSpec((tm, o.shape[1]), lambda i, j: (i, 0))
    w_spec = lambda w: pl.BlockSpec((None, w.shape[1], tn), lambda i, j: (layer, 0, j))
    g_spec = lambda br: pl.BlockSpec((tm, tn), lambda i, j: (i, br * nb + j))
    return pl.pallas_call(
        _merge_kernel,
        grid=(m // tm, nb),
        in_specs=[o_spec(oa), o_spec(ob), o_spec(oc), w_spec(wa), w_spec(wb), w_spec(wc),
                  g_spec(0), g_spec(1), g_spec(2)],
        out_specs=pl.BlockSpec((tm, tn), lambda i, j: (i, j)),
        out_shape=jax.ShapeDtypeStruct((m, d), BF16),
        compiler_params=_params("arbitrary", "arbitrary"),
        name="merge",
    )(oa, ob, oc, wa, wb, wc, gates, gates, gates)


def _rope_tables(n_tokens, dim, pad):
    pos = jnp.arange(n_tokens, dtype=jnp.int32)
    row = (pos // GRID_W).astype(F32)[:, None]
    col = (pos % GRID_W).astype(F32)[:, None]
    quarter = dim // 4
    inv = ROPE_THETA ** (-jnp.arange(quarter, dtype=F32) / quarter)[None, :]
    ang = jnp.concatenate([row * inv, row * inv, col * inv, col * inv], axis=1)
    cos, sin = jnp.cos(ang), jnp.sin(ang)
    first = (jnp.arange(dim) // quarter) % 2 == 0
    sin_lo = jnp.where(first[None, :], -sin, 0.0)
    sin_hi = jnp.where(first[None, :], 0.0, sin)
    if pad:
        z = jnp.zeros_like(cos)
        tabs = [jnp.concatenate([t, z], axis=1) for t in (cos, sin_lo, sin_hi)]
    else:
        tabs = [jnp.tile(t, (1, LANE // dim)) for t in (cos, sin_lo, sin_hi)]
    return tuple(tabs)


def kernel(x_prompt, x_sample, c, cache_attn_k, cache_attn_v, state_ret_fwd, state_ret_bwd, cache_mla_ckv, cache_mla_krope, c_ctx, w_mod, b_mod, g_pre_mix, g_post_mix, g_pre_mlp, g_post_mlp, w_in, attn_q_norm, attn_k_norm, ret_decay_fwd, ret_decay_bwd, ret_gn, mla_q_norm, mla_kv_norm, w_mla_uq, w_mla_ukv, w_branch_a, w_branch_b, w_branch_c, w_out, w_mlp_up, w_mlp_down):
    bp, n_p, d = x_prompt.shape
    bs, n_s, _ = x_sample.shape
    depth = w_in.shape[0]
    past = cache_attn_k.shape[2]
    assert n_p % B_CHUNK == 0 and n_s % B_CHUNK == 0

    w_in_t = jnp.transpose(w_in, (0, 2, 1))
    gate_w = w_in.shape[-1] - IN_MAIN
    w_uq = jnp.pad(w_mla_uq.reshape(depth, C_Q_RANK, C_HEADS, C_NOPE_DIM + C_ROPE_DIM),
                   ((0, 0), (0, 0), (0, 0), (0, C_QK_PAD - C_NOPE_DIM - C_ROPE_DIM)))
    w_uq = w_uq.reshape(depth, C_Q_RANK, C_HEADS * C_QK_PAD).astype(BF16)
    w_ukv = w_mla_ukv.astype(BF16)
    wa, wb, wc = w_branch_a.astype(BF16), w_branch_b.astype(BF16), w_branch_c.astype(BF16)
    w_o, w_dn = w_out.astype(BF16), w_mlp_down.astype(BF16)
    vec = lambda g: g.reshape(depth, 1, g.shape[-1])
    g_pre_mix, g_post_mix, g_pre_mlp, g_post_mlp = map(vec, (g_pre_mix, g_post_mix, g_pre_mlp, g_post_mlp))
    gqn, gkn, gcq, gckv, gret = map(vec, (attn_q_norm, attn_k_norm, mla_q_norm, mla_kv_norm, ret_gn))
    lg = jnp.stack([jax.nn.log_sigmoid(ret_decay_fwd.astype(F32)),
                    jax.nn.log_sigmoid(ret_decay_bwd.astype(F32))], axis=1)

    rows = -(-(1 + bs) // 8) * 8
    cvec = jnp.zeros((rows, d), F32).at[0].set(c_ctx).at[1:1 + bs].set(c)
    mod = _modulation(cvec, w_mod, b_mod).reshape(depth, rows, 6, d)
    mod = jnp.pad(mod, ((0, 0), (0, 0), (0, MOD_ROWS - 6), (0, 0)))

    tabs128 = _rope_tables(n_s, HEAD_DIM, pad=False)
    tabs64 = _rope_tables(n_s, C_ROPE_DIM, pad=False)
    tabs64p = _rope_tables(n_s, C_ROPE_DIM, pad=True)

    to_layers = lambda t: jnp.swapaxes(t.reshape(bs, depth, past, -1), 0, 1).reshape(depth, bs * past, -1)
    ctx_k, ctx_v = to_layers(cache_attn_k.astype(BF16)), to_layers(cache_attn_v.astype(BF16))
    ctx_kr = jnp.pad(cache_mla_krope, ((0, 0), (0, 0), (0, 0), (0, LANE - C_ROPE_DIM)))
    ctx_c = _prep_ctx_ckv(to_layers(cache_mla_ckv), to_layers(ctx_kr), gckv, w_ukv)
    zero_state = jnp.zeros((bp, B_HEADS, B_QK_DIM, B_V_DIM), F32)

    streams = {
        "p": dict(x=x_prompt.reshape(bp * n_p, d), batch=bp, n=n_p, grp=lambda r: 0, latent=False),
        "s": dict(x=x_sample.reshape(bs * n_s, d), batch=bs, n=n_s, grp=lambda r: 1 + r // n_s, latent=True),
    }
    for st in streams.values():
        st["h"] = _prenorm(st["x"], g_pre_mix, mod, 0, st["grp"])
    cache_bufs = state_bufs = None

    for l in range(depth):
        for st in streams.values():
            batch, n, grp, latent = st["batch"], st["n"], st["grp"], st["latent"]
            proj = _matmul(st["h"], w_in_t, l, tm=2048, tn=512, w_transposed=True, n=MAIN_W, name="w_in")
            gates = _matmul(st["h"], w_in_t, l, tm=2048, tn=512, w_transposed=True, w_row0=IN_MAIN, n=gate_w,
                            name="w_gates")
            t128 = tabs128 if latent else None

            tabs = (tabs128, tabs64, tabs64p) if latent else None
            pa, qc, pc = _prep_all(proj, gqn, gkn, gcq, gckv, w_uq, w_ukv, l, tabs, n, cache=not latent,
                                   depth=depth, cache_bufs=None if latent else cache_bufs)
            if not latent:
                cache_bufs = (pa[2], pa[3], pc[2], pc[3])

            qa, ka = pa[0], pa[1]
            ctx_a = (ctx_k, ctx_v) if latent else None
            o_a = _attention(qa, ka, 0, proj, OFF_AV // HEAD_DIM, ctx_a, batch, n,
                             groups=A_KV_HEADS, gps=1 if latent else A_KV_HEADS, rep=A_HEADS // A_KV_HEADS,
                             dqk=HEAD_DIM, dv=HEAD_DIM, rows=1024, name="attn_a", ctx_layer=l)

            if latent:
                s0f, s0b = state_ret_fwd[:, l], state_ret_bwd[:, l]
            else:
                s0f = s0b = zero_state
            rb = _retention(proj, lg[l], gret, s0f, s0b, l, t128, batch, n,
                            emit_state=not latent, all_heads=not latent, depth=depth,
                            state_bufs=None if latent else state_bufs)
            o_b = rb[0]
            if not latent:
                state_bufs = (rb[1], rb[2])

            kc, vc = pc[0], pc[1]
            o_c = _attention(qc, kc, 0, vc, 0, ctx_c if latent else None, batch, n,
                             groups=C_HEADS, gps=1 if latent else C_HEADS, rep=1,
                             dqk=C_QK_PAD, dv=C_V_DIM, rows=1024, name="attn_c", ctx_layer=l)

            merged = _merge(o_a, o_b, o_c, wa, wb, wc, gates, l, d)
            group_rows = n if latent else batch * n
            x, h2 = _matmul_row(merged, w_o, st["x"], g_post_mix, g_pre_mlp, mod, l, l, grp,
                                gate_row=2, next_rows=(3, 4), tm=min(512, group_rows), tk=2048, name="w_out")
            hid = _matmul(h2, w_mlp_up, l, tm=2048, tn=512, relu2=True, name="mlp_up")
            last = l == depth - 1
            res = _matmul_row(hid, w_dn, x, g_post_mlp, g_pre_mix, mod, l, min(l + 1, depth - 1), grp,
                              gate_row=5, next_rows=None if last else (0, 1), tm=min(1024, group_rows), tk=1024,
                              name="mlp_down")
            st["x"] = res[0]
            st["h"] = None if last else res[1]

    y_p = streams["p"]["x"].reshape(bp, n_p, d)
    y_s = streams["s"]["x"].reshape(bs, n_s, d)
    nk, nv, new_mla_ckv, new_mla_krope = cache_bufs
    new_attn_k = nk.reshape(bp, depth, n_p, A_KV_HEADS, HEAD_DIM)
    new_attn_v = nv.reshape(bp, depth, n_p, A_KV_HEADS, HEAD_DIM)
    new_ret_fwd, new_ret_bwd = state_bufs
    return (y_p, y_s, new_attn_k, new_attn_v, new_ret_fwd, new_ret_bwd, new_mla_ckv, new_mla_krope)
```

```python
import functools

import jax
import jax.numpy as jnp
from jax import lax
from jax.experimental import pallas as pl
from jax.experimental.pallas import tpu as pltpu

F32 = jnp.float32
BF16 = jnp.bfloat16

EPS = 1e-6
LOG2E = 1.4426950408889634
ROPE_THETA = 10000.0
GRID_W = 64
LANE = 128
HEAD_DIM = 128
A_HEADS = 8
A_KV_HEADS = 2
B_HEADS = 4
B_QK_DIM = 128
B_V_DIM = 256
B_CHUNK = 128
C_HEADS = 8
C_Q_RANK = 512
C_KV_RANK = 256
C_NOPE_DIM = 128
C_ROPE_DIM = 64
C_V_DIM = 128
C_QK_PAD = 256

OFF_AQ = 0
OFF_AK = OFF_AQ + A_HEADS * HEAD_DIM
OFF_AV = OFF_AK + A_KV_HEADS * HEAD_DIM
OFF_BQ = OFF_AV + A_KV_HEADS * HEAD_DIM
OFF_BK = OFF_BQ + B_HEADS * B_QK_DIM
OFF_BV = OFF_BK + B_HEADS * B_QK_DIM
OFF_BG = OFF_BV + B_HEADS * B_V_DIM
OFF_CQ = OFF_BG + B_HEADS * B_V_DIM
OFF_CKV = OFF_CQ + C_Q_RANK
OFF_CKR = OFF_CKV + C_KV_RANK
IN_MAIN = OFF_CKR + C_ROPE_DIM
MAIN_W = 5632
MOD_ROWS = 8
EPILOGUE_ROWS = 16

VMEM_LIMIT = 56 * 1024 * 1024


def _params(*sem):
    return pltpu.CompilerParams(dimension_semantics=sem, vmem_limit_bytes=VMEM_LIMIT)


def _rms(x, g):
    return x * lax.rsqrt(jnp.mean(x * x, axis=-1, keepdims=True) + EPS) * g


def _fit(total, want):
    while total % want:
        want //= 2
    return want


def _rope(x, cos, sin_lo, sin_hi, shift):
    return (x * cos + pltpu.roll(x, LANE - shift, 1) * sin_lo + pltpu.roll(x, shift, 1) * sin_hi)


def _mod_kernel(c_ref, w_ref, b_ref, o_ref):
    c = c_ref[...]
    a = (c * jax.nn.sigmoid(c)).astype(BF16)
    o_ref[...] = jnp.dot(a, w_ref[...].astype(BF16), preferred_element_type=F32) + b_ref[...]


def _modulation(cvec, w_mod, b_mod):
    depth, d, n = w_mod.shape
    rows = cvec.shape[0]
    tn = min(1024, n)
    return pl.pallas_call(
        _mod_kernel,
        grid=(depth, n // tn),
        in_specs=[
            pl.BlockSpec((rows, d), lambda l, j: (0, 0)),
            pl.BlockSpec((None, d, tn), lambda l, j: (l, 0, j)),
            pl.BlockSpec((None, 1, tn), lambda l, j: (l, 0, j)),
        ],
        out_specs=pl.BlockSpec((None, rows, tn), lambda l, j: (l, 0, j)),
        out_shape=jax.ShapeDtypeStruct((depth, rows, n), F32),
        compiler_params=_params("arbitrary", "arbitrary"),
        name="modulation",
    )(cvec, w_mod, b_mod.reshape(depth, 1, n))


def _prenorm_kernel(x_ref, g_ref, mod_ref, h_ref):
    y = _rms(x_ref[...], g_ref[...])
    h_ref[...] = (y * (1.0 + mod_ref[1:2, :]) + mod_ref[0:1, :]).astype(BF16)


def _prenorm(x, g, mod, layer, grp):
    m, d = x.shape
    tm = min(512, m)
    return pl.pallas_call(
        _prenorm_kernel,
        grid=(m // tm,),
        in_specs=[
            pl.BlockSpec((tm, d), lambda i: (i, 0)),
            pl.BlockSpec((None, 1, d), lambda i: (layer, 0, 0)),
            pl.BlockSpec((None, None, MOD_ROWS, d), lambda i: (layer, grp(i * tm), 0, 0)),
        ],
        out_specs=pl.BlockSpec((tm, d), lambda i: (i, 0)),
        out_shape=jax.ShapeDtypeStruct((m, d), BF16),
        compiler_params=_params("arbitrary"),
        name="prenorm",
    )(x, g, mod)


def _mm_kernel(a_ref, w_ref, o_ref, *, relu2, w_transposed):
    w = (w_ref[0] if w_transposed else w_ref[...]).astype(BF16)
    contract = (((1,), (1 if w_transposed else 0,)), ((), ()))
    y = lax.dot_general(a_ref[...], w, contract, preferred_element_type=F32)
    if relu2:
        y = jnp.square(jnp.maximum(y, 0.0))
    o_ref[...] = y.astype(o_ref.dtype)


def _matmul(a, w, layer, *, tm, tn, relu2=False, w_transposed=False, w_row0=0, n=None, name):
    m, k = a.shape
    n = n if w_transposed else w.shape[2]
    tm, tn = _fit(m, min(tm, m)), _fit(n, min(tn, n))
    if w_transposed:
        w_spec = pl.BlockSpec((pl.Element(1), pl.Element(tn), pl.Element(k)),
                              lambda i, j: (layer, pl.multiple_of(w_row0 + j * tn, 64), 0))
    else:
        w_spec = pl.BlockSpec((None, k, tn), lambda i, j: (layer, 0, j))
    return pl.pallas_call(
        functools.partial(_mm_kernel, relu2=relu2, w_transposed=w_transposed),
        grid=(m // tm, n // tn),
        in_specs=[
            pl.BlockSpec((tm, k), lambda i, j: (i, 0)),
            w_spec,
        ],
        out_specs=pl.BlockSpec((tm, tn), lambda i, j: (i, j)),
        out_shape=jax.ShapeDtypeStruct((m, n), BF16),
        compiler_params=_params("arbitrary", "arbitrary"),
        name=name,
    )(a, w)


def _mm_row_kernel(a_ref, w_ref, x_ref, gpost_ref, mod_ref, gnext_ref, modn_ref, *rest,
                   gate_row, next_rows, nk, nt):
    if next_rows is None:
        xo_ref, acc0_ref, acc1_ref = rest
    else:
        xo_ref, ho_ref, acc0_ref, acc1_ref = rest
    i = pl.program_id(0)
    k = pl.program_id(1)
    tm, d = acc0_ref.shape
    sub = tm // nk

    @pl.when((i == 0) & (k == 0))
    def _():
        acc0_ref[...] = jnp.zeros_like(acc0_ref)
        acc1_ref[...] = jnp.zeros_like(acc1_ref)

    def epilogue(acc_ref):
        gain = gpost_ref[...] * mod_ref[gate_row:gate_row + 1, :]
        if next_rows is not None:
            shift_row, scale_row = next_rows
            gain_next = gnext_ref[...] * (1.0 + modn_ref[scale_row:scale_row + 1, :])
            shift_next = modn_ref[shift_row:shift_row + 1, :]
        base = 0 if nk == 1 else pl.multiple_of(k * sub, sub)
        for c in range(sub // EPILOGUE_ROWS):
            rows = pl.ds(base + c * EPILOGUE_ROWS, EPILOGUE_ROWS)
            f = acc_ref[rows, :]
            acc_ref[rows, :] = jnp.zeros((EPILOGUE_ROWS, d), F32)
            blk = pl.ds(c * EPILOGUE_ROWS, EPILOGUE_ROWS)
            xn = x_ref[blk, :] + _rms(f, gain)
            xo_ref[blk, :] = xn
            if next_rows is not None:
                ho_ref[blk, :] = (_rms(xn, gain_next) + shift_next).astype(BF16)

    @pl.when(i == 0)
    def _():
        acc0_ref[...] += jnp.dot(a_ref[...], w_ref[...], preferred_element_type=F32)

    for parity, (acc_mm, acc_ep) in enumerate(((acc0_ref, acc1_ref), (acc1_ref, acc0_ref))):
        @pl.when((i > 0) & (i < nt) & (i % 2 == parity))
        def _():
            epilogue(acc_ep)
            acc_mm[...] += jnp.dot(a_ref[...], w_ref[...], preferred_element_type=F32)

    @pl.when(i == nt)
    def _():
        epilogue(acc0_ref if (nt - 1) % 2 == 0 else acc1_ref)


def _matmul_row(a, w, x, gpost, gnext, mod, layer, next_layer, grp, *, gate_row, next_rows, tm, tk, name):
    m, kdim = a.shape
    d = w.shape[-1]
    tm, tk = min(tm, m), min(tk, kdim)
    nk, nt = kdim // tk, m // tm
    assert tm % nk == 0 and (tm // nk) % 16 == 0
    cur = lambda i: jnp.minimum(i, nt - 1)
    kcur = lambda i, k: jnp.where(i < nt, k, nk - 1)
    prev = lambda i: jnp.maximum(i - 1, 0)
    sub = tm // nk
    slab = lambda i, k: (prev(i) * nk + jnp.where(i == 0, 0, k), 0)
    vec = pl.BlockSpec((None, 1, d), lambda i, k: (layer, 0, 0))
    vecn = pl.BlockSpec((None, 1, d), lambda i, k: (next_layer, 0, 0))
    out_specs = [pl.BlockSpec((sub, d), slab)]
    out_shape = [jax.ShapeDtypeStruct((m, d), F32)]
    if next_rows is not None:
        out_specs.append(pl.BlockSpec((sub, d), slab))
        out_shape.append(jax.ShapeDtypeStruct((m, d), BF16))
    return pl.pallas_call(
        functools.partial(_mm_row_kernel, gate_row=gate_row, next_rows=next_rows, nk=nk, nt=nt),
        grid=(nt + 1, nk),
        in_specs=[
            pl.BlockSpec((tm, tk), lambda i, k: (cur(i), kcur(i, k))),
            pl.BlockSpec((None, tk, d), lambda i, k: (layer, kcur(i, k), 0)),
            pl.BlockSpec((sub, d), slab),
            vec,
            pl.BlockSpec((None, None, MOD_ROWS, d), lambda i, k: (layer, grp(prev(i) * tm), 0, 0)),
            vecn,
            pl.BlockSpec((None, None, MOD_ROWS, d), lambda i, k: (next_layer, grp(prev(i) * tm), 0, 0)),
        ],
        out_specs=out_specs,
        out_shape=out_shape,
        scratch_shapes=[pltpu.VMEM((tm, d), F32), pltpu.VMEM((tm, d), F32)],
        compiler_params=_params("arbitrary", "arbitrary"),
        name=name,
    )(a, w, x, gpost, mod, gnext, mod)


def _prep_a_kernel(p_ref, gq_ref, gk_ref, *rest, rope, cache):
    rest = list(rest)
    tabs = [rest.pop(0) for _ in range(3)] if rope else None
    q_ref, k_ref = rest[:2]
    scale = HEAD_DIM ** -0.5 * LOG2E
    for h in range(A_HEADS):
        y = _rms(p_ref[:, h * HEAD_DIM:(h + 1) * HEAD_DIM].astype(F32), gq_ref[...])
        if rope:
            y = _rope(y, tabs[0][...], tabs[1][...], tabs[2][...], HEAD_DIM // 4)
        q_ref[:, h * HEAD_DIM:(h + 1) * HEAD_DIM] = (y * scale).astype(BF16)
    for g in range(A_KV_HEADS):
        lo = OFF_AK + g * HEAD_DIM
        y = _rms(p_ref[:, lo:lo + HEAD_DIM].astype(F32), gk_ref[...])
        if cache:
            rest[2][:, g * HEAD_DIM:(g + 1) * HEAD_DIM] = y
        if rope:
            y = _rope(y, tabs[0][...], tabs[1][...], tabs[2][...], HEAD_DIM // 4)
        k_ref[:, g * HEAD_DIM:(g + 1) * HEAD_DIM] = y.astype(BF16)
    if cache:
        rest[3][...] = p_ref[:, OFF_AV:OFF_BQ].astype(F32)


def _prep_cq_kernel(p_ref, g_ref, w_ref, *rest, rope):
    rest = list(rest)
    tabs = [rest.pop(0) for _ in range(3)] if rope else None
    q_ref = rest[0]
    scale = (C_NOPE_DIM + C_ROPE_DIM) ** -0.5 * LOG2E
    y = _rms(p_ref[...].astype(F32), g_ref[...]).astype(BF16)
    z = jnp.dot(y, w_ref[...], preferred_element_type=F32)
    for h in range(C_HEADS):
        lo = h * C_QK_PAD
        q_ref[:, lo:lo + C_NOPE_DIM] = (z[:, lo:lo + C_NOPE_DIM] * scale).astype(BF16)
        r = z[:, lo + C_NOPE_DIM:lo + C_QK_PAD]
        if rope:
            r = _rope(r, tabs[0][...], tabs[1][...], tabs[2][...], C_ROPE_DIM // 4)
        q_ref[:, lo + C_NOPE_DIM:lo + C_QK_PAD] = (r * scale).astype(BF16)


def _prep_ckv_kernel(ckv_ref, kr_ref, g_ref, w_ref, *rest, norm, rope, cache):
    rest = list(rest)
    tabs = [rest.pop(0) for _ in range(3)] if rope else None
    kc_ref, vc_ref = rest[:2]
    x = ckv_ref[...].astype(F32)
    if norm:
        x = _rms(x, g_ref[...])
    kr = kr_ref[...].astype(F32)
    if cache:
        rest[2][...] = x
        rest[3][...] = kr[:, :C_ROPE_DIM]
    if rope:
        kr = _rope(kr, tabs[0][...], tabs[1][...], tabs[2][...], C_ROPE_DIM // 4)
    krb = kr.astype(BF16)
    z = jnp.dot(x.astype(BF16), w_ref[...], preferred_element_type=F32)
    up = C_NOPE_DIM + C_V_DIM
    for h in range(C_HEADS):
        kc_ref[:, h * C_QK_PAD:h * C_QK_PAD + C_NOPE_DIM] = z[:, h * up:h * up + C_NOPE_DIM].astype(BF16)
        kc_ref[:, h * C_QK_PAD + C_NOPE_DIM:(h + 1) * C_QK_PAD] = krb
        vc_ref[:, h * C_V_DIM:(h + 1) * C_V_DIM] = z[:, h * up + C_NOPE_DIM:(h + 1) * up].astype(BF16)


def _prep_ctx_ckv(ckv, kr, g, w_ukv):
    depth, m, _ = ckv.shape
    tr = min(512, m)
    row = lambda w: pl.BlockSpec((None, tr, w), lambda l, i: (l, i, 0))
    return pl.pallas_call(
        functools.partial(_prep_ckv_kernel, norm=False, rope=False, cache=False),
        grid=(depth, m // tr),
        in_specs=[row(C_KV_RANK), row(LANE),
                  pl.BlockSpec((None, 1, C_KV_RANK), lambda l, i: (l, 0, 0)),
                  pl.BlockSpec((None,) + w_ukv.shape[1:], lambda l, i: (l, 0, 0))],
        out_specs=[row(C_HEADS * C_QK_PAD), row(C_HEADS * C_V_DIM)],
        out_shape=[jax.ShapeDtypeStruct((depth, m, C_HEADS * C_QK_PAD), BF16),
                   jax.ShapeDtypeStruct((depth, m, C_HEADS * C_V_DIM), BF16)],
        compiler_params=_params("arbitrary", "arbitrary"),
        name="prep_ctx_ckv",
    )(ckv, kr, g, w_ukv)


def _prep_all_kernel(pa_ref, pcq_ref, ckv_ref, kr_ref, gq_ref, gk_ref, gcq_ref, gckv_ref, wuq_ref, wukv_ref,
                     *rest, rope, cache, n_alias):
    rest = list(rest)
    t128, t64, t64p = ([rest.pop(0) for _ in range(3)] if rope else [] for _ in range(3))
    del rest[:n_alias]
    n_out = 4 if cache else 2
    a_out = [rest.pop(0) for _ in range(n_out)]
    cq_out = [rest.pop(0)]
    ckv_out = [rest.pop(0) for _ in range(n_out)]
    _prep_a_kernel(pa_ref, gq_ref, gk_ref, *t128, *a_out, rope=rope, cache=cache)
    _prep_cq_kernel(pcq_ref, gcq_ref, wuq_ref, *t64p, *cq_out, rope=rope)
    _prep_ckv_kernel(ckv_ref, kr_ref, gckv_ref, wukv_ref, *t64, *ckv_out, norm=True, rope=rope, cache=cache)


def _prep_all(proj, gq, gk, gcq, gckv, w_uq, w_ukv, layer, tabs, n_per, *, cache, depth=None, cache_bufs=None):
    m = proj.shape[0]
    tr = min(512, n_per)
    nt = n_per // tr
    rope = tabs is not None
    kvw = A_KV_HEADS * HEAD_DIM
    vec = lambda w: pl.BlockSpec((None, 1, w), lambda i: (layer, 0, 0))
    mat = lambda w: pl.BlockSpec((None,) + w.shape[1:], lambda i: (layer, 0, 0))
    row = lambda w, blk=0: pl.BlockSpec((tr, w), lambda i: (i, blk))
    in_specs = [row(OFF_BQ), row(C_Q_RANK, OFF_CQ // C_Q_RANK), row(C_KV_RANK, OFF_CKV // C_KV_RANK),
                row(LANE, OFF_CKR // LANE), vec(HEAD_DIM), vec(HEAD_DIM), vec(C_Q_RANK), vec(C_KV_RANK),
                mat(w_uq), mat(w_ukv)]
    args = [proj, proj, proj, proj, gq, gk, gcq, gckv, w_uq, w_ukv]
    if rope:
        in_specs += [pl.BlockSpec((tr, LANE), lambda i: (i % nt, 0))] * 9
        args += [t for group in tabs for t in group]
    out = lambda w, dt: (row(w), jax.ShapeDtypeStruct((m, w), dt))
    stacked = lambda w: (pl.BlockSpec((None, None, tr, w), lambda i: (i // nt, layer, i % nt, 0)),
                         jax.ShapeDtypeStruct((m // n_per, depth, n_per, w), F32))
    outs = [out(A_HEADS * HEAD_DIM, BF16), out(kvw, BF16)]
    outs += [stacked(kvw), stacked(kvw)] if cache else []
    outs += [out(C_HEADS * C_QK_PAD, BF16), out(C_HEADS * C_QK_PAD, BF16), out(C_HEADS * C_V_DIM, BF16)]
    outs += [stacked(C_KV_RANK), stacked(C_ROPE_DIM)] if cache else []
    aliases = {}
    if cache_bufs is not None:
        for buf, out_idx in zip(cache_bufs, (2, 3, 7, 8)):
            aliases[len(args)] = out_idx
            in_specs.append(pl.BlockSpec(memory_space=pl.ANY))
            args.append(buf)
    res = pl.pallas_call(
        functools.partial(_prep_all_kernel, rope=rope, cache=cache, n_alias=len(aliases)),
        grid=(m // tr,),
        in_specs=in_specs,
        out_specs=[o[0] for o in outs],
        out_shape=[o[1] for o in outs],
        input_output_aliases=aliases,
        compiler_params=_params("arbitrary"),
        name="prep_all",
    )(*args)
    n_out = 4 if cache else 2
    return res[:n_out], res[n_out], res[n_out + 1:]


def _attn_kernel(q_ref, k_ref, v_ref, *rest, gps, rep, dqk, dv, tq, tk, ctx):
    if ctx:
        k2_ref, v2_ref, o_ref, acc_ref, s0_ref, s1_ref = rest
    else:
        o_ref, acc_ref, s0_ref, s1_ref = rest
    s_refs = (s0_ref, s1_ref)

    items = []
    for g in range(gps):
        chunks = [(k_ref, v_ref, c * tk, tk) for c in range(k_ref.shape[0] // tk)]
        if ctx:
            nctx = k2_ref.shape[0]
            chunks += [(k2_ref, v2_ref, lo, min(tk, nctx - lo)) for lo in range(0, nctx, tk)]
        items += [(g, c == 0, c == len(chunks) - 1) + ch for c, ch in enumerate(chunks)]

    def queries(g):
        cols = [q_ref[:, (g * rep + r) * dqk:(g * rep + r + 1) * dqk] for r in range(rep)]
        return cols[0] if rep == 1 else jnp.concatenate(cols, axis=0)

    def scores(t):
        g, _, _, kr, _, lo, size = items[t]
        s_refs[t % 2][:, :size] = lax.dot_general(
            queries(g), kr[lo:lo + size, g * dqk:(g + 1) * dqk], (((1,), (1,)), ((), ())),
            preferred_element_type=F32)

    scores(0)
    m = None
    for t, (g, first, last, _, vr, lo, size) in enumerate(items):
        if t + 1 < len(items):
            scores(t + 1)
        s = s_refs[t % 2][:, :size]
        m_blk = jnp.max(s, axis=-1, keepdims=True)
        m_new = m_blk if first else jnp.maximum(m, m_blk)
        p = jnp.exp2(s - m_new).astype(BF16)
        v_ext = jnp.concatenate([vr[lo:lo + size, g * dv:(g + 1) * dv], jnp.ones((size, dv), BF16)], axis=1)
        pv = jnp.dot(p, v_ext, preferred_element_type=F32)
        if first:
            acc_ref[g] = pv
        else:
            acc_ref[g] = jnp.exp2(m - m_new) * acc_ref[g] + pv
        m = m_new
        if last:
            o = acc_ref[g, :, :dv] / acc_ref[g, :, dv:]
            for r in range(rep):
                o_ref[:, (g * rep + r) * dv:(g * rep + r + 1) * dv] = o[r * tq:(r + 1) * tq].astype(o_ref.dtype)


def _attention(q, k, kblk0, v, vblk0, ctx_kv, batch, n, *, groups, gps, rep, dqk, dv, rows, name, ctx_layer=0):
    tq = min(rows // rep, n)
    tk = min(512, n)
    nq = n // tq
    assert groups % gps == 0 and kblk0 % gps == 0 and vblk0 % gps == 0 and (ctx_kv is None or gps == 1)
    in_specs = [
        pl.BlockSpec((tq, gps * rep * dqk), lambda b, g, i: (b * nq + i, g)),
        pl.BlockSpec((n, gps * dqk), lambda b, g, i: (b, kblk0 // gps + g)),
        pl.BlockSpec((n, gps * dv), lambda b, g, i: (b, vblk0 // gps + g)),
    ]
    args = [q, k, v]
    if ctx_kv is not None:
        k2, v2 = ctx_kv
        nc = k2.shape[-2] // batch
        if k2.ndim == 3:
            in_specs += [pl.BlockSpec((None, nc, dqk), lambda b, g, i: (ctx_layer, b, g)),
                         pl.BlockSpec((None, nc, dv), lambda b, g, i: (ctx_layer, b, g))]
        else:
            in_specs += [pl.BlockSpec((nc, dqk), lambda b, g, i: (b, g)),
                         pl.BlockSpec((nc, dv), lambda b, g, i: (b, g))]
        args += [k2, v2]
    return pl.pallas_call(
        functools.partial(_attn_kernel, gps=gps, rep=rep, dqk=dqk, dv=dv, tq=tq, tk=tk, ctx=ctx_kv is not None),
        grid=(batch, groups // gps, nq),
        in_specs=in_specs,
        out_specs=pl.BlockSpec((tq, gps * rep * dv), lambda b, g, i: (b * nq + i, g)),
        out_shape=jax.ShapeDtypeStruct((batch * n, groups * rep * dv), BF16),
        scratch_shapes=[pltpu.VMEM((gps, rep * tq, 2 * dv), F32),
                        pltpu.VMEM((rep * tq, tk), F32),
                        pltpu.VMEM((rep * tq, tk), F32)],
        compiler_params=_params("arbitrary", "arbitrary", "arbitrary"),
        name=name,
    )(*args)


def _ret_head(lgf, lgb, io, tabs, nc, kvf_ref, kvb_ref, state_ref):
    c_len = B_CHUNK
    rope = tabs is not None
    ii = lax.broadcasted_iota(jnp.int32, (c_len, c_len), 0)
    jj = lax.broadcasted_iota(jnp.int32, (c_len, c_len), 1)
    diff = (ii - jj).astype(F32)
    dmat = (jnp.where(diff >= 0, jnp.exp(jnp.maximum(diff, 0.0) * lgf), 0.0)
            + jnp.where(diff <= 0, jnp.exp(jnp.maximum(-diff, 0.0) * lgb), 0.0))
    ri = lax.broadcasted_iota(jnp.int32, (c_len, 1), 0).astype(F32)
    qdec_f = jnp.exp((ri + 1.0) * lgf)
    qdec_b = jnp.exp((c_len - ri) * lgb)
    kdec_f = jnp.exp((c_len - 1.0 - ri) * lgf)
    kdec_b = jnp.exp(ri * lgb)
    one = jnp.ones((1, 1), F32)
    cdec_f = jnp.exp(one * (c_len * lgf))
    cdec_b = jnp.exp(one * (c_len * lgb))
    kscale = B_QK_DIM ** -0.5

    def roped(load, lo):
        x = load(lo).astype(F32)
        if rope:
            t = [tab[pl.ds(lo, c_len), :] for tab in tabs]
            x = _rope(x, t[0], t[1], t[2], B_QK_DIM // 4)
        return x

    def increments(c, carry):
        lo = pl.multiple_of(c * c_len, c_len)
        kc = roped(io["k"], lo) * kscale
        kk = jnp.concatenate([kc * kdec_f, kc * kdec_b], axis=1).T.astype(BF16)
        kv = jnp.dot(kk, io["v"](lo), preferred_element_type=F32)
        kvf_ref[c] = kv[:B_QK_DIM]
        kvb_ref[c] = kv[B_QK_DIM:]
        return carry

    lax.fori_loop(0, nc, increments, 0, unroll=_fit(nc, 8))

    def scan_f(c, s):
        state_ref[c, :B_QK_DIM, :] = s.astype(BF16)
        return cdec_f * s + kvf_ref[c]

    def scan_b(t, s):
        c = nc - 1 - t
        state_ref[c, B_QK_DIM:, :] = s.astype(BF16)
        return cdec_b * s + kvb_ref[c]

    s_f = lax.fori_loop(0, nc, scan_f, io["s0f"], unroll=_fit(nc, 2))
    s_b = lax.fori_loop(0, nc, scan_b, io["s0b"], unroll=_fit(nc, 2))
    io["store_state"](s_f, s_b)

    def outputs(c, carry):
        lo = pl.multiple_of(c * c_len, c_len)
        qc = roped(io["q"], lo)
        kc = (roped(io["k"], lo) * kscale).astype(BF16)
        s = lax.dot_general(qc.astype(BF16), kc, (((1,), (1,)), ((), ())), preferred_element_type=F32) * dmat
        o = jnp.dot(s.astype(BF16), io["v"](lo), preferred_element_type=F32)
        qq = jnp.concatenate([qc * qdec_f, qc * qdec_b], axis=1).astype(BF16)
        o = o + jnp.dot(qq, state_ref[c], preferred_element_type=F32)
        gate = io["gate"](lo).astype(F32)
        io["store_o"](lo, (gate * jax.nn.sigmoid(gate) * _rms(o, io["gn"])).astype(BF16))
        return carry

    lax.fori_loop(0, nc, outputs, 0, unroll=_fit(nc, 4))


def _ret_kernel(lg_ref, *refs, n, rope, emit_state, all_heads, n_alias):
    refs = list(refs)
    data = [refs.pop(0) for _ in range(2 if all_heads else 4)]
    gn_ref, s0f_ref, s0b_ref = refs.pop(0), refs.pop(0), refs.pop(0)
    tabs = [refs.pop(0) for _ in range(3)] if rope else None
    del refs[:n_alias]
    o_ref = refs.pop(0)
    sfo_ref, sbo_ref = (refs.pop(0), refs.pop(0)) if emit_state else (None, None)
    kvf_ref, kvb_ref, state_ref = refs
    nc = n // B_CHUNK
    rows = lambda lo: pl.ds(lo, B_CHUNK)
    qw, vw = B_HEADS * B_QK_DIM, B_V_DIM

    for hh in range(B_HEADS if all_heads else 1):
        if all_heads:
            h = hh
            blk1, blk2 = data
            v_src = (blk1, 2 * qw + hh * vw) if hh < 2 else (blk2, (hh - 2) * vw)
            col = lambda ref, c0, w: (lambda lo: ref[rows(lo), c0:c0 + w])
            io = dict(
                q=col(blk1, hh * B_QK_DIM, B_QK_DIM), k=col(blk1, qw + hh * B_QK_DIM, B_QK_DIM),
                v=col(v_src[0], v_src[1], vw), gate=col(blk2, 2 * vw + hh * vw, vw),
                gn=gn_ref[:, hh * vw:(hh + 1) * vw], s0f=s0f_ref[hh], s0b=s0b_ref[hh])

            def store_o(lo, val, hh=hh):
                o_ref[rows(lo), hh * vw:(hh + 1) * vw] = val

            def store_state(sf, sb, hh=hh):
                if emit_state:
                    sfo_ref[hh] = sf
                    sbo_ref[hh] = sb
        else:
            h = pl.program_id(1)
            whole = lambda ref: (lambda lo: ref[rows(lo), :])
            io = dict(q=whole(data[0]), k=whole(data[1]), v=whole(data[2]), gate=whole(data[3]),
                      gn=gn_ref[...], s0f=s0f_ref[...], s0b=s0b_ref[...])

            def store_o(lo, val):
                o_ref[rows(lo), :] = val

            def store_state(sf, sb):
                if emit_state:
                    sfo_ref[...] = sf
                    sbo_ref[...] = sb
        io.update(store_o=store_o, store_state=store_state)
        _ret_head(lg_ref[0, h], lg_ref[1, h], io, tabs, nc, kvf_ref.at[hh], kvb_ref.at[hh], state_ref.at[hh])


def _retention(proj, lg, gn, s0f, s0b, s0_layer, layer, tabs, batch, n, *, emit_state, all_heads, depth=None,
               state_bufs=None):
    rope = tabs is not None
    assert all_heads or not emit_state
    if all_heads:
        wide = (OFF_CQ - OFF_BQ) // 2
        assert OFF_BQ % wide == 0 and wide == 2 * B_HEADS * B_QK_DIM + 2 * B_V_DIM
        st_spec = pl.BlockSpec((None, None, B_HEADS, B_QK_DIM, B_V_DIM), lambda b, h, lg_: (b, s0_layer, 0, 0, 0))
        in_specs = [
            pl.BlockSpec((n, wide), lambda b, h, lg_: (b, OFF_BQ // wide)),
            pl.BlockSpec((n, wide), lambda b, h, lg_: (b, OFF_BQ // wide + 1)),
            pl.BlockSpec((None, 1, B_HEADS * B_V_DIM), lambda b, h, lg_: (layer, 0, 0)),
        ]
        args = [proj, proj, gn]
        out_specs = [pl.BlockSpec((n, B_HEADS * B_V_DIM), lambda b, h, lg_: (b, 0))]
        heads_per_step = B_HEADS
    else:
        qb, kb = OFF_BQ // B_QK_DIM, OFF_BK // B_QK_DIM
        vb, gb = OFF_BV // B_V_DIM, OFF_BG // B_V_DIM
        st_spec = pl.BlockSpec((None, None, None, B_QK_DIM, B_V_DIM), lambda b, h, lg_: (b, s0_layer, h, 0, 0))
        in_specs = [
            pl.BlockSpec((n, B_QK_DIM), lambda b, h, lg_: (b, qb + h)),
            pl.BlockSpec((n, B_QK_DIM), lambda b, h, lg_: (b, kb + h)),
            pl.BlockSpec((n, B_V_DIM), lambda b, h, lg_: (b, vb + h)),
            pl.BlockSpec((n, B_V_DIM), lambda b, h, lg_: (b, gb + h)),
            pl.BlockSpec((None, 1, B_V_DIM), lambda b, h, lg_: (layer, 0, h)),
        ]
        args = [proj, proj, proj, proj, gn]
        out_specs = [pl.BlockSpec((n, B_V_DIM), lambda b, h, lg_: (b, h))]
        heads_per_step = 1
    in_specs += [st_spec, st_spec]
    args += [s0f, s0b]
    if rope:
        in_specs += [pl.BlockSpec((n, LANE), lambda b, h, lg_: (0, 0))] * 3
        args += list(tabs)
    out_shape = [jax.ShapeDtypeStruct((batch * n, B_HEADS * B_V_DIM), BF16)]
    aliases = {}
    if emit_state:
        out_specs += [pl.BlockSpec((None, None, B_HEADS, B_QK_DIM, B_V_DIM),
                                   lambda b, h, lg_: (b, layer, 0, 0, 0))] * 2
        out_shape += [jax.ShapeDtypeStruct((batch, depth, B_HEADS, B_QK_DIM, B_V_DIM), F32)] * 2
        for j, buf in enumerate(state_bufs or ()):
            aliases[1 + len(args)] = 1 + j
            in_specs.append(pl.BlockSpec(memory_space=pl.ANY))
            args.append(buf)
    nc = n // B_CHUNK
    return pl.pallas_call(
        functools.partial(_ret_kernel, n=n, rope=rope, emit_state=emit_state, all_heads=all_heads,
                          n_alias=len(aliases)),
        grid_spec=pltpu.PrefetchScalarGridSpec(
            num_scalar_prefetch=1,
            grid=(batch, B_HEADS // heads_per_step),
            in_specs=in_specs,
            out_specs=out_specs,
            scratch_shapes=[pltpu.VMEM((heads_per_step, nc, B_QK_DIM, B_V_DIM), F32),
                            pltpu.VMEM((heads_per_step, nc, B_QK_DIM, B_V_DIM), F32),
                            pltpu.VMEM((heads_per_step, nc, 2 * B_QK_DIM, B_V_DIM), BF16)],
        ),
        out_shape=out_shape,
        input_output_aliases=aliases,
        compiler_params=_params("arbitrary", "arbitrary"),
        name="retention",
    )(lg, *args)


def _merge_kernel(oa_ref, ob_ref, oc_ref, wa_ref, wb_ref, wc_ref, ga_ref, gb_ref, gc_ref, o_ref):
    def branch(o, w, g):
        return jax.nn.sigmoid(g[...].astype(F32)) * jnp.dot(o[...], w[...], preferred_element_type=F32)

    o_ref[...] = (branch(oa_ref, wa_ref, ga_ref) + branch(ob_ref, wb_ref, gb_ref)
                  + branch(oc_ref, wc_ref, gc_ref)).astype(BF16)


def _merge(oa, ob, oc, wa, wb, wc, gates, layer, d):
    m = oa.shape[0]
    tm = min(1024, m)
    tn = min(1024, d)
    nb = d // tn
    o_spec = lambda o: pl.BlockSpec((tm, o.shape[1]), lambda i, j: (i, 0))
    w_spec = lambda w: pl.BlockSpec((None, w.shape[1], tn), lambda i, j: (layer, 0, j))
    g_spec = lambda br: pl.BlockSpec((tm, tn), lambda i, j: (i, br * nb + j))
    return pl.pallas_call(
        _merge_kernel,
        grid=(m // tm, nb),
        in_specs=[o_spec(oa), o_spec(ob), o_spec(oc), w_spec(wa), w_spec(wb), w_spec(wc),
                  g_spec(0), g_spec(1), g_spec(2)],
        out_specs=pl.BlockSpec((tm, tn), lambda i, j: (i, j)),
        out_shape=jax.ShapeDtypeStruct((m, d), BF16),
        compiler_params=_params("arbitrary", "arbitrary"),
        name="merge",
    )(oa, ob, oc, wa, wb, wc, gates, gates, gates)


def _rope_tables(n_tokens, dim, pad):
    pos = jnp.arange(n_tokens, dtype=jnp.int32)
    row = (pos // GRID_W).astype(F32)[:, None]
    col = (pos % GRID_W).astype(F32)[:, None]
    quarter = dim // 4
    inv = ROPE_THETA ** (-jnp.arange(quarter, dtype=F32) / quarter)[None, :]
    ang = jnp.concatenate([row * inv, row * inv, col * inv, col * inv], axis=1)
    cos, sin = jnp.cos(ang), jnp.sin(ang)
    first = (jnp.arange(dim) // quarter) % 2 == 0
    sin_lo = jnp.where(first[None, :], -sin, 0.0)
    sin_hi = jnp.where(first[None, :], 0.0, sin)
    if pad:
        z = jnp.zeros_like(cos)
        tabs = [jnp.concatenate([t, z], axis=1) for t in (cos, sin_lo, sin_hi)]
    else:
        tabs = [jnp.tile(t, (1, LANE // dim)) for t in (cos, sin_lo, sin_hi)]
    return tuple(tabs)


def kernel(x_prompt, x_sample, c, cache_attn_k, cache_attn_v, state_ret_fwd, state_ret_bwd, cache_mla_ckv, cache_mla_krope, c_ctx, w_mod, b_mod, g_pre_mix, g_post_mix, g_pre_mlp, g_post_mlp, w_in, attn_q_norm, attn_k_norm, ret_decay_fwd, ret_decay_bwd, ret_gn, mla_q_norm, mla_kv_norm, w_mla_uq, w_mla_ukv, w_branch_a, w_branch_b, w_branch_c, w_out, w_mlp_up, w_mlp_down):
    bp, n_p, d = x_prompt.shape
    bs, n_s, _ = x_sample.shape
    depth = w_in.shape[0]
    past = cache_attn_k.shape[2]
    assert n_p % B_CHUNK == 0 and n_s % B_CHUNK == 0

    w_in_t = jnp.transpose(w_in, (0, 2, 1))
    gate_w = w_in.shape[-1] - IN_MAIN
    w_uq = jnp.pad(w_mla_uq.reshape(depth, C_Q_RANK, C_HEADS, C_NOPE_DIM + C_ROPE_DIM),
                   ((0, 0), (0, 0), (0, 0), (0, C_QK_PAD - C_NOPE_DIM - C_ROPE_DIM)))
    w_uq = w_uq.reshape(depth, C_Q_RANK, C_HEADS * C_QK_PAD).astype(BF16)
    w_ukv = w_mla_ukv.astype(BF16)
    wa, wb, wc = w_branch_a.astype(BF16), w_branch_b.astype(BF16), w_branch_c.astype(BF16)
    w_o, w_dn = w_out.astype(BF16), w_mlp_down.astype(BF16)
    vec = lambda g: g.reshape(depth, 1, g.shape[-1])
    g_pre_mix, g_post_mix, g_pre_mlp, g_post_mlp = map(vec, (g_pre_mix, g_post_mix, g_pre_mlp, g_post_mlp))
    gqn, gkn, gcq, gckv, gret = map(vec, (attn_q_norm, attn_k_norm, mla_q_norm, mla_kv_norm, ret_gn))
    lg = jnp.stack([jax.nn.log_sigmoid(ret_decay_fwd.astype(F32)),
                    jax.nn.log_sigmoid(ret_decay_bwd.astype(F32))], axis=1)

    rows = -(-(1 + bs) // 8) * 8
    cvec = jnp.zeros((rows, d), F32).at[0].set(c_ctx).at[1:1 + bs].set(c)
    mod = _modulation(cvec, w_mod, b_mod).reshape(depth, rows, 6, d)
    mod = jnp.pad(mod, ((0, 0), (0, 0), (0, MOD_ROWS - 6), (0, 0)))

    tabs128 = _rope_tables(n_s, HEAD_DIM, pad=False)
    tabs64 = _rope_tables(n_s, C_ROPE_DIM, pad=False)
    tabs64p = _rope_tables(n_s, C_ROPE_DIM, pad=True)

    to_layers = lambda t: jnp.swapaxes(t.reshape(bs, depth, past, -1), 0, 1).reshape(depth, bs * past, -1)
    ctx_k, ctx_v = to_layers(cache_attn_k.astype(BF16)), to_layers(cache_attn_v.astype(BF16))
    ctx_kr = jnp.pad(cache_mla_krope, ((0, 0), (0, 0), (0, 0), (0, LANE - C_ROPE_DIM)))
    ctx_c = _prep_ctx_ckv(to_layers(cache_mla_ckv), to_layers(ctx_kr), gckv, w_ukv)
    zero_state = jnp.zeros((bp, 1, B_HEADS, B_QK_DIM, B_V_DIM), F32)

    streams = {
        "p": dict(x=x_prompt.reshape(bp * n_p, d), batch=bp, n=n_p, grp=lambda r: 0, latent=False),
        "s": dict(x=x_sample.reshape(bs * n_s, d), batch=bs, n=n_s, grp=lambda r: 1 + r // n_s, latent=True),
    }
    for st in streams.values():
        st["h"] = _prenorm(st["x"], g_pre_mix, mod, 0, st["grp"])
    kvw = A_KV_HEADS * HEAD_DIM
    cache_bufs = tuple(jnp.zeros((bp, depth, n_p, w), F32) for w in (kvw, kvw, C_KV_RANK, C_ROPE_DIM))
    state_bufs = tuple(jnp.zeros((bp, depth, B_HEADS, B_QK_DIM, B_V_DIM), F32) for _ in range(2))

    for l in range(depth):
        for st in streams.values():
            batch, n, grp, latent = st["batch"], st["n"], st["grp"], st["latent"]
            proj = _matmul(st["h"], w_in_t, l, tm=2048, tn=512, w_transposed=True, n=MAIN_W, name="w_in")
            gates = _matmul(st["h"], w_in_t, l, tm=2048, tn=1024, w_transposed=True, w_row0=IN_MAIN, n=gate_w,
                            name="w_gates")
            t128 = tabs128 if latent else None

            tabs = (tabs128, tabs64, tabs64p) if latent else None
            pa, qc, pc = _prep_all(proj, gqn, gkn, gcq, gckv, w_uq, w_ukv, l, tabs, n, cache=not latent,
                                   depth=depth, cache_bufs=None if latent else cache_bufs)
            if not latent:
                cache_bufs = (pa[2], pa[3], pc[2], pc[3])

            qa, ka = pa[0], pa[1]
            ctx_a = (ctx_k, ctx_v) if latent else None
            o_a = _attention(qa, ka, 0, proj, OFF_AV // HEAD_DIM, ctx_a, batch, n,
                             groups=A_KV_HEADS, gps=1 if latent else A_KV_HEADS, rep=A_HEADS // A_KV_HEADS,
                             dqk=HEAD_DIM, dv=HEAD_DIM, rows=1024, name="attn_a", ctx_layer=l)

            if latent:
                s0f, s0b, s0_layer = state_ret_fwd, state_ret_bwd, l
            else:
                s0f, s0b, s0_layer = zero_state, zero_state, 0
            rb = _retention(proj, lg[l], gret, s0f, s0b, s0_layer, l, t128, batch, n,
                            emit_state=not latent, all_heads=not latent, depth=depth,
                            state_bufs=None if latent else state_bufs)
            o_b = rb[0]
            if not latent:
                state_bufs = (rb[1], rb[2])

            kc, vc = pc[0], pc[1]
            o_c = _attention(qc, kc, 0, vc, 0, ctx_c if latent else None, batch, n,
                             groups=C_HEADS, gps=1 if latent else C_HEADS, rep=1,
                             dqk=C_QK_PAD, dv=C_V_DIM, rows=1024, name="attn_c", ctx_layer=l)

            merged = _merge(o_a, o_b, o_c, wa, wb, wc, gates, l, d)
            group_rows = n if latent else batch * n
            x, h2 = _matmul_row(merged, w_o, st["x"], g_post_mix, g_pre_mlp, mod, l, l, grp,
                                gate_row=2, next_rows=(3, 4), tm=min(512, group_rows), tk=2048, name="w_out")
            hid = _matmul(h2, w_mlp_up, l, tm=2048, tn=1024, relu2=True, name="mlp_up")
            last = l == depth - 1
            res = _matmul_row(hid, w_dn, x, g_post_mlp, g_pre_mix, mod, l, min(l + 1, depth - 1), grp,
                              gate_row=5, next_rows=None if last else (0, 1), tm=min(1024, group_rows), tk=1024,
                              name="mlp_down")
            st["x"] = res[0]
            st["h"] = None if last else res[1]

    y_p = streams["p"]["x"].reshape(bp, n_p, d)
    y_s = streams["s"]["x"].reshape(bs, n_s, d)
    nk, nv, new_mla_ckv, new_mla_krope = cache_bufs
    new_attn_k = nk.reshape(bp, depth, n_p, A_KV_HEADS, HEAD_DIM)
    new_attn_v = nv.reshape(bp, depth, n_p, A_KV_HEADS, HEAD_DIM)
    new_ret_fwd, new_ret_bwd = state_bufs
    return (y_p, y_s, new_attn_k, new_attn_v, new_ret_fwd, new_ret_bwd, new_mla_ckv, new_mla_krope)
```

```python
import functools

import jax
import jax.numpy as jnp
from jax import lax
from jax.experimental import pallas as pl
from jax.experimental.pallas import tpu as pltpu

F32 = jnp.float32
BF16 = jnp.bfloat16

EPS = 1e-6
LOG2E = 1.4426950408889634
ROPE_THETA = 10000.0
GRID_W = 64
LANE = 128
HEAD_DIM = 128
A_HEADS = 8
A_KV_HEADS = 2
B_HEADS = 4
B_QK_DIM = 128
B_V_DIM = 256
B_CHUNK = 128
C_HEADS = 8
C_Q_RANK = 512
C_KV_RANK = 256
C_NOPE_DIM = 128
C_ROPE_DIM = 64
C_V_DIM = 128
C_QK_PAD = 256

OFF_AQ = 0
OFF_AK = OFF_AQ + A_HEADS * HEAD_DIM
OFF_AV = OFF_AK + A_KV_HEADS * HEAD_DIM
OFF_BQ = OFF_AV + A_KV_HEADS * HEAD_DIM
OFF_BK = OFF_BQ + B_HEADS * B_QK_DIM
OFF_BV = OFF_BK + B_HEADS * B_QK_DIM
OFF_BG = OFF_BV + B_HEADS * B_V_DIM
OFF_CQ = OFF_BG + B_HEADS * B_V_DIM
OFF_CKV = OFF_CQ + C_Q_RANK
OFF_CKR = OFF_CKV + C_KV_RANK
IN_MAIN = OFF_CKR + C_ROPE_DIM
MAIN_W = 5632
MOD_ROWS = 8
EPILOGUE_ROWS = 16

VMEM_LIMIT = 56 * 1024 * 1024

TILE_ROWS_ELEMENTWISE = 512
TILE_MOD_N = 1024
TILE_MATMUL_M = 2048
TILE_W_IN_N = 512
TILE_WIDE_N = 1024
TILE_MERGE_M, TILE_MERGE_N = 1024, 1024
TILE_W_OUT_M, TILE_W_OUT_K = 512, 2048
TILE_MLP_DOWN_M, TILE_MLP_DOWN_K = 1024, 1024
ATTN_ROWS = 1024
ATTN_KEYS = 512


def _params(*sem):
    return pltpu.CompilerParams(dimension_semantics=sem, vmem_limit_bytes=VMEM_LIMIT)


def _rms(x, g):
    return x * lax.rsqrt(jnp.mean(x * x, axis=-1, keepdims=True) + EPS) * g


def _fit(total, want):
    while total % want:
        want //= 2
    return want


def _rope(x, cos, sin_lo, sin_hi, shift):
    return (x * cos + pltpu.roll(x, LANE - shift, 1) * sin_lo + pltpu.roll(x, shift, 1) * sin_hi)


def _mod_kernel(c_ref, w_ref, b_ref, o_ref):
    c = c_ref[...]
    a = (c * jax.nn.sigmoid(c)).astype(BF16)
    o_ref[...] = jnp.dot(a, w_ref[...].astype(BF16), preferred_element_type=F32) + b_ref[...]


def _modulation(cvec, w_mod, b_mod):
    depth, d, n = w_mod.shape
    rows = cvec.shape[0]
    tn = min(TILE_MOD_N, n)
    return pl.pallas_call(
        _mod_kernel,
        grid=(depth, n // tn),
        in_specs=[
            pl.BlockSpec((rows, d), lambda l, j: (0, 0)),
            pl.BlockSpec((None, d, tn), lambda l, j: (l, 0, j)),
            pl.BlockSpec((None, 1, tn), lambda l, j: (l, 0, j)),
        ],
        out_specs=pl.BlockSpec((None, rows, tn), lambda l, j: (l, 0, j)),
        out_shape=jax.ShapeDtypeStruct((depth, rows, n), F32),
        compiler_params=_params("arbitrary", "arbitrary"),
        name="modulation",
    )(cvec, w_mod, b_mod.reshape(depth, 1, n))


def _prenorm_kernel(x_ref, g_ref, mod_ref, h_ref):
    y = _rms(x_ref[...], g_ref[...])
    h_ref[...] = (y * (1.0 + mod_ref[1:2, :]) + mod_ref[0:1, :]).astype(BF16)


def _prenorm(x, g, mod, layer, grp):
    m, d = x.shape
    tm = min(TILE_ROWS_ELEMENTWISE, m)
    return pl.pallas_call(
        _prenorm_kernel,
        grid=(m // tm,),
        in_specs=[
            pl.BlockSpec((tm, d), lambda i: (i, 0)),
            pl.BlockSpec((None, 1, d), lambda i: (layer, 0, 0)),
            pl.BlockSpec((None, None, MOD_ROWS, d), lambda i: (layer, grp(i * tm), 0, 0)),
        ],
        out_specs=pl.BlockSpec((tm, d), lambda i: (i, 0)),
        out_shape=jax.ShapeDtypeStruct((m, d), BF16),
        compiler_params=_params("arbitrary"),
        name="prenorm",
    )(x, g, mod)


def _mm_kernel(a_ref, w_ref, o_ref, *, relu2, w_transposed):
    w = (w_ref[0] if w_transposed else w_ref[...]).astype(BF16)
    contract = (((1,), (1 if w_transposed else 0,)), ((), ()))
    y = lax.dot_general(a_ref[...], w, contract, preferred_element_type=F32)
    if relu2:
        y = jnp.square(jnp.maximum(y, 0.0))
    o_ref[...] = y.astype(o_ref.dtype)


def _matmul(a, w, layer, *, tm, tn, relu2=False, w_transposed=False, w_row0=0, n=None, name):
    m, k = a.shape
    n = n if w_transposed else w.shape[2]
    tm, tn = _fit(m, min(tm, m)), _fit(n, min(tn, n))
    if w_transposed:
        w_spec = pl.BlockSpec((pl.Element(1), pl.Element(tn), pl.Element(k)),
                              lambda i, j: (layer, pl.multiple_of(w_row0 + j * tn, 64), 0))
    else:
        w_spec = pl.BlockSpec((None, k, tn), lambda i, j: (layer, 0, j))
    return pl.pallas_call(
        functools.partial(_mm_kernel, relu2=relu2, w_transposed=w_transposed),
        grid=(m // tm, n // tn),
        in_specs=[
            pl.BlockSpec((tm, k), lambda i, j: (i, 0)),
            w_spec,
        ],
        out_specs=pl.BlockSpec((tm, tn), lambda i, j: (i, j)),
        out_shape=jax.ShapeDtypeStruct((m, n), BF16),
        compiler_params=_params("arbitrary", "arbitrary"),
        name=name,
    )(a, w)


def _mm_row_kernel(a_ref, w_ref, x_ref, gpost_ref, mod_ref, gnext_ref, modn_ref, *rest,
                   gate_row, next_rows, nk, nt):
    if next_rows is None:
        xo_ref, acc0_ref, acc1_ref = rest
    else:
        xo_ref, ho_ref, acc0_ref, acc1_ref = rest
    i = pl.program_id(0)
    k = pl.program_id(1)
    tm, d = acc0_ref.shape
    sub = tm // nk

    @pl.when((i == 0) & (k == 0))
    def _():
        acc0_ref[...] = jnp.zeros_like(acc0_ref)
        acc1_ref[...] = jnp.zeros_like(acc1_ref)

    def epilogue(acc_ref):
        gain = gpost_ref[...] * mod_ref[gate_row:gate_row + 1, :]
        if next_rows is not None:
            shift_row, scale_row = next_rows
            gain_next = gnext_ref[...] * (1.0 + modn_ref[scale_row:scale_row + 1, :])
            shift_next = modn_ref[shift_row:shift_row + 1, :]
        base = 0 if nk == 1 else pl.multiple_of(k * sub, sub)
        for c in range(sub // EPILOGUE_ROWS):
            rows = pl.ds(base + c * EPILOGUE_ROWS, EPILOGUE_ROWS)
            f = acc_ref[rows, :]
            acc_ref[rows, :] = jnp.zeros((EPILOGUE_ROWS, d), F32)
            blk = pl.ds(c * EPILOGUE_ROWS, EPILOGUE_ROWS)
            xn = x_ref[blk, :] + _rms(f, gain)
            xo_ref[blk, :] = xn
            if next_rows is not None:
                ho_ref[blk, :] = (_rms(xn, gain_next) + shift_next).astype(BF16)

    @pl.when(i == 0)
    def _():
        acc0_ref[...] += jnp.dot(a_ref[...], w_ref[...].astype(BF16), preferred_element_type=F32)

    for parity, (acc_mm, acc_ep) in enumerate(((acc0_ref, acc1_ref), (acc1_ref, acc0_ref))):
        @pl.when((i > 0) & (i < nt) & (i % 2 == parity))
        def _():
            epilogue(acc_ep)
            acc_mm[...] += jnp.dot(a_ref[...], w_ref[...].astype(BF16), preferred_element_type=F32)

    @pl.when(i == nt)
    def _():
        epilogue(acc0_ref if (nt - 1) % 2 == 0 else acc1_ref)


def _matmul_row(a, w, x, gpost, gnext, mod, layer, next_layer, grp, *, gate_row, next_rows, tm, tk, name):
    m, kdim = a.shape
    d = w.shape[-1]
    tm, tk = min(tm, m), min(tk, kdim)
    nk, nt = kdim // tk, m // tm
    assert tm % nk == 0 and (tm // nk) % 16 == 0
    cur = lambda i: jnp.minimum(i, nt - 1)
    kcur = lambda i, k: jnp.where(i < nt, k, nk - 1)
    prev = lambda i: jnp.maximum(i - 1, 0)
    sub = tm // nk
    slab = lambda i, k: (prev(i) * nk + jnp.where(i == 0, 0, k), 0)
    vec = pl.BlockSpec((None, 1, d), lambda i, k: (layer, 0, 0))
    vecn = pl.BlockSpec((None, 1, d), lambda i, k: (next_layer, 0, 0))
    out_specs = [pl.BlockSpec((sub, d), slab)]
    out_shape = [jax.ShapeDtypeStruct((m, d), F32)]
    if next_rows is not None:
        out_specs.append(pl.BlockSpec((sub, d), slab))
        out_shape.append(jax.ShapeDtypeStruct((m, d), BF16))
    return pl.pallas_call(
        functools.partial(_mm_row_kernel, gate_row=gate_row, next_rows=next_rows, nk=nk, nt=nt),
        grid=(nt + 1, nk),
        in_specs=[
            pl.BlockSpec((tm, tk), lambda i, k: (cur(i), kcur(i, k))),
            pl.BlockSpec((None, tk, d), lambda i, k: (layer, kcur(i, k), 0)),
            pl.BlockSpec((sub, d), slab),
            vec,
            pl.BlockSpec((None, None, MOD_ROWS, d), lambda i, k: (layer, grp(prev(i) * tm), 0, 0)),
            vecn,
            pl.BlockSpec((None, None, MOD_ROWS, d), lambda i, k: (next_layer, grp(prev(i) * tm), 0, 0)),
        ],
        out_specs=out_specs,
        out_shape=out_shape,
        scratch_shapes=[pltpu.VMEM((tm, d), F32), pltpu.VMEM((tm, d), F32)],
        compiler_params=_params("arbitrary", "arbitrary"),
        name=name,
    )(a, w, x, gpost, mod, gnext, mod)


def _prep_a_kernel(p_ref, gq_ref, gk_ref, *rest, rope, cache):
    rest = list(rest)
    tabs = [rest.pop(0) for _ in range(3)] if rope else None
    q_ref, k_ref = rest[:2]
    scale = HEAD_DIM ** -0.5 * LOG2E
    for h in range(A_HEADS):
        y = _rms(p_ref[:, h * HEAD_DIM:(h + 1) * HEAD_DIM].astype(F32), gq_ref[...])
        if rope:
            y = _rope(y, tabs[0][...], tabs[1][...], tabs[2][...], HEAD_DIM // 4)
        q_ref[:, h * HEAD_DIM:(h + 1) * HEAD_DIM] = (y * scale).astype(BF16)
    for g in range(A_KV_HEADS):
        lo = OFF_AK + g * HEAD_DIM
        y = _rms(p_ref[:, lo:lo + HEAD_DIM].astype(F32), gk_ref[...])
        if cache:
            rest[2][:, g * HEAD_DIM:(g + 1) * HEAD_DIM] = y
        if rope:
            y = _rope(y, tabs[0][...], tabs[1][...], tabs[2][...], HEAD_DIM // 4)
        k_ref[:, g * HEAD_DIM:(g + 1) * HEAD_DIM] = y.astype(BF16)
    if cache:
        rest[3][...] = p_ref[:, OFF_AV:OFF_BQ].astype(F32)


def _prep_cq_kernel(p_ref, g_ref, w_ref, *rest, rope):
    rest = list(rest)
    tabs = [rest.pop(0) for _ in range(3)] if rope else None
    q_ref = rest[0]
    scale = (C_NOPE_DIM + C_ROPE_DIM) ** -0.5 * LOG2E
    y = _rms(p_ref[...].astype(F32), g_ref[...]).astype(BF16)
    z = jnp.dot(y, w_ref[...], preferred_element_type=F32)
    for h in range(C_HEADS):
        lo = h * C_QK_PAD
        q_ref[:, lo:lo + C_NOPE_DIM] = (z[:, lo:lo + C_NOPE_DIM] * scale).astype(BF16)
        r = z[:, lo + C_NOPE_DIM:lo + C_QK_PAD]
        if rope:
            r = _rope(r, tabs[0][...], tabs[1][...], tabs[2][...], C_ROPE_DIM // 4)
        q_ref[:, lo + C_NOPE_DIM:lo + C_QK_PAD] = (r * scale).astype(BF16)


def _prep_ckv_kernel(ckv_ref, kr_ref, g_ref, w_ref, *rest, norm, rope, cache):
    rest = list(rest)
    tabs = [rest.pop(0) for _ in range(3)] if rope else None
    kc_ref, vc_ref = rest[:2]
    x = ckv_ref[...].astype(F32)
    if norm:
        x = _rms(x, g_ref[...])
    kr = kr_ref[...].astype(F32)
    if cache:
        rest[2][...] = x
        rest[3][...] = kr[:, :C_ROPE_DIM]
    if rope:
        kr = _rope(kr, tabs[0][...], tabs[1][...], tabs[2][...], C_ROPE_DIM // 4)
    krb = kr.astype(BF16)
    z = jnp.dot(x.astype(BF16), w_ref[...], preferred_element_type=F32)
    up = C_NOPE_DIM + C_V_DIM
    for h in range(C_HEADS):
        kc_ref[:, h * C_QK_PAD:h * C_QK_PAD + C_NOPE_DIM] = z[:, h * up:h * up + C_NOPE_DIM].astype(BF16)
        kc_ref[:, h * C_QK_PAD + C_NOPE_DIM:(h + 1) * C_QK_PAD] = krb
        vc_ref[:, h * C_V_DIM:(h + 1) * C_V_DIM] = z[:, h * up + C_NOPE_DIM:(h + 1) * up].astype(BF16)


def _prep_ctx_ckv(ckv, kr, g, w_ukv):
    depth, m, _ = ckv.shape
    tr = min(TILE_ROWS_ELEMENTWISE, m)
    row = lambda w: pl.BlockSpec((None, tr, w), lambda l, i: (l, i, 0))
    return pl.pallas_call(
        functools.partial(_prep_ckv_kernel, norm=False, rope=False, cache=False),
        grid=(depth, m // tr),
        in_specs=[row(C_KV_RANK), row(LANE),
                  pl.BlockSpec((None, 1, C_KV_RANK), lambda l, i: (l, 0, 0)),
                  pl.BlockSpec((None,) + w_ukv.shape[1:], lambda l, i: (l, 0, 0))],
        out_specs=[row(C_HEADS * C_QK_PAD), row(C_HEADS * C_V_DIM)],
        out_shape=[jax.ShapeDtypeStruct((depth, m, C_HEADS * C_QK_PAD), BF16),
                   jax.ShapeDtypeStruct((depth, m, C_HEADS * C_V_DIM), BF16)],
        compiler_params=_params("arbitrary", "arbitrary"),
        name="prep_ctx_ckv",
    )(ckv, kr, g, w_ukv)


def _prep_all_kernel(pa_ref, pcq_ref, ckv_ref, kr_ref, gq_ref, gk_ref, gcq_ref, gckv_ref, wuq_ref, wukv_ref,
                     *rest, rope, cache, n_alias):
    rest = list(rest)
    t128, t64, t64p = ([rest.pop(0) for _ in range(3)] if rope else [] for _ in range(3))
    del rest[:n_alias]
    n_out = 4 if cache else 2
    a_out = [rest.pop(0) for _ in range(n_out)]
    cq_out = [rest.pop(0)]
    ckv_out = [rest.pop(0) for _ in range(n_out)]
    _prep_a_kernel(pa_ref, gq_ref, gk_ref, *t128, *a_out, rope=rope, cache=cache)
    _prep_cq_kernel(pcq_ref, gcq_ref, wuq_ref, *t64p, *cq_out, rope=rope)
    _prep_ckv_kernel(ckv_ref, kr_ref, gckv_ref, wukv_ref, *t64, *ckv_out, norm=True, rope=rope, cache=cache)


def _prep_all(proj, gq, gk, gcq, gckv, w_uq, w_ukv, layer, tabs, n_per, *, cache, depth=None, cache_bufs=None):
    m = proj.shape[0]
    tr = min(TILE_ROWS_ELEMENTWISE, n_per)
    nt = n_per // tr
    rope = tabs is not None
    kvw = A_KV_HEADS * HEAD_DIM
    vec = lambda w: pl.BlockSpec((None, 1, w), lambda i: (layer, 0, 0))
    mat = lambda w: pl.BlockSpec((None,) + w.shape[1:], lambda i: (layer, 0, 0))
    row = lambda w, blk=0: pl.BlockSpec((tr, w), lambda i: (i, blk))
    in_specs = [row(OFF_BQ), row(C_Q_RANK, OFF_CQ // C_Q_RANK), row(C_KV_RANK, OFF_CKV // C_KV_RANK),
                row(LANE, OFF_CKR // LANE), vec(HEAD_DIM), vec(HEAD_DIM), vec(C_Q_RANK), vec(C_KV_RANK),
                mat(w_uq), mat(w_ukv)]
    args = [proj, proj, proj, proj, gq, gk, gcq, gckv, w_uq, w_ukv]
    if rope:
        in_specs += [pl.BlockSpec((tr, LANE), lambda i: (i % nt, 0))] * 9
        args += [t for group in tabs for t in group]
    out = lambda w, dt: (row(w), jax.ShapeDtypeStruct((m, w), dt))
    stacked = lambda w: (pl.BlockSpec((None, None, tr, w), lambda i: (i // nt, layer, i % nt, 0)),
                         jax.ShapeDtypeStruct((m // n_per, depth, n_per, w), F32))
    outs = [out(A_HEADS * HEAD_DIM, BF16), out(kvw, BF16)]
    outs += [stacked(kvw), stacked(kvw)] if cache else []
    outs += [out(C_HEADS * C_QK_PAD, BF16), out(C_HEADS * C_QK_PAD, BF16), out(C_HEADS * C_V_DIM, BF16)]
    outs += [stacked(C_KV_RANK), stacked(C_ROPE_DIM)] if cache else []
    aliases = {}
    if cache_bufs is not None:
        for buf, out_idx in zip(cache_bufs, (2, 3, 7, 8)):
            aliases[len(args)] = out_idx
            in_specs.append(pl.BlockSpec(memory_space=pl.ANY))
            args.append(buf)
    res = pl.pallas_call(
        functools.partial(_prep_all_kernel, rope=rope, cache=cache, n_alias=len(aliases)),
        grid=(m // tr,),
        in_specs=in_specs,
        out_specs=[o[0] for o in outs],
        out_shape=[o[1] for o in outs],
        input_output_aliases=aliases,
        compiler_params=_params("arbitrary"),
        name="prep_all",
    )(*args)
    n_out = 4 if cache else 2
    return res[:n_out], res[n_out], res[n_out + 1:]


def _attn_kernel(q_ref, k_ref, v_ref, *rest, gps, rep, dqk, dv, tq, tk, ctx):
    if ctx:
        k2_ref, v2_ref, o_ref, acc_ref, s0_ref, s1_ref = rest
    else:
        o_ref, acc_ref, s0_ref, s1_ref = rest
    s_refs = (s0_ref, s1_ref)

    items = []
    for g in range(gps):
        chunks = [(k_ref, v_ref, c * tk, tk) for c in range(k_ref.shape[0] // tk)]
        if ctx:
            nctx = k2_ref.shape[0]
            chunks += [(k2_ref, v2_ref, lo, min(tk, nctx - lo)) for lo in range(0, nctx, tk)]
        items += [(g, c == 0, c == len(chunks) - 1) + ch for c, ch in enumerate(chunks)]

    def queries(g):
        cols = [q_ref[:, (g * rep + r) * dqk:(g * rep + r + 1) * dqk] for r in range(rep)]
        return cols[0] if rep == 1 else jnp.concatenate(cols, axis=0)

    def scores(t):
        g, _, _, kr, _, lo, size = items[t]
        s_refs[t % 2][:, :size] = lax.dot_general(
            queries(g), kr[lo:lo + size, g * dqk:(g + 1) * dqk], (((1,), (1,)), ((), ())),
            preferred_element_type=F32)

    scores(0)
    m = None
    for t, (g, first, last, _, vr, lo, size) in enumerate(items):
        if t + 1 < len(items):
            scores(t + 1)
        s = s_refs[t % 2][:, :size]
        m_blk = jnp.max(s, axis=-1, keepdims=True)
        m_new = m_blk if first else jnp.maximum(m, m_blk)
        p = jnp.exp2(s - m_new).astype(BF16)
        v_ext = jnp.concatenate([vr[lo:lo + size, g * dv:(g + 1) * dv], jnp.ones((size, dv), BF16)], axis=1)
        pv = jnp.dot(p, v_ext, preferred_element_type=F32)
        if first:
            acc_ref[g] = pv
        else:
            acc_ref[g] = jnp.exp2(m - m_new) * acc_ref[g] + pv
        m = m_new
        if last:
            o = acc_ref[g, :, :dv] / acc_ref[g, :, dv:]
            for r in range(rep):
                o_ref[:, (g * rep + r) * dv:(g * rep + r + 1) * dv] = o[r * tq:(r + 1) * tq].astype(o_ref.dtype)


def _attention(q, k, kblk0, v, vblk0, ctx_kv, batch, n, *, groups, gps, rep, dqk, dv, rows, name, ctx_layer=0):
    tq = min(rows // rep, n)
    tk = min(ATTN_KEYS, n)
    nq = n // tq
    assert groups % gps == 0 and kblk0 % gps == 0 and vblk0 % gps == 0 and (ctx_kv is None or gps == 1)
    in_specs = [
        pl.BlockSpec((tq, gps * rep * dqk), lambda b, g, i: (b * nq + i, g)),
        pl.BlockSpec((n, gps * dqk), lambda b, g, i: (b, kblk0 // gps + g)),
        pl.BlockSpec((n, gps * dv), lambda b, g, i: (b, vblk0 // gps + g)),
    ]
    args = [q, k, v]
    if ctx_kv is not None:
        k2, v2 = ctx_kv
        nc = k2.shape[-2] // batch
        if k2.ndim == 3:
            in_specs += [pl.BlockSpec((None, nc, dqk), lambda b, g, i: (ctx_layer, b, g)),
                         pl.BlockSpec((None, nc, dv), lambda b, g, i: (ctx_layer, b, g))]
        else:
            in_specs += [pl.BlockSpec((nc, dqk), lambda b, g, i: (b, g)),
                         pl.BlockSpec((nc, dv), lambda b, g, i: (b, g))]
        args += [k2, v2]
    return pl.pallas_call(
        functools.partial(_attn_kernel, gps=gps, rep=rep, dqk=dqk, dv=dv, tq=tq, tk=tk, ctx=ctx_kv is not None),
        grid=(batch, groups // gps, nq),
        in_specs=in_specs,
        out_specs=pl.BlockSpec((tq, gps * rep * dv), lambda b, g, i: (b * nq + i, g)),
        out_shape=jax.ShapeDtypeStruct((batch * n, groups * rep * dv), BF16),
        scratch_shapes=[pltpu.VMEM((gps, rep * tq, 2 * dv), F32),
                        pltpu.VMEM((rep * tq, tk), F32),
                        pltpu.VMEM((rep * tq, tk), F32)],
        compiler_params=_params("arbitrary", "arbitrary", "arbitrary"),
        name=name,
    )(*args)


def _ret_head(lgf, lgb, io, tabs, nc, kvf_ref, kvb_ref, state_ref):
    c_len = B_CHUNK
    rope = tabs is not None
    ii = lax.broadcasted_iota(jnp.int32, (c_len, c_len), 0)
    jj = lax.broadcasted_iota(jnp.int32, (c_len, c_len), 1)
    diff = (ii - jj).astype(F32)
    dmat = (jnp.where(diff >= 0, jnp.exp(jnp.maximum(diff, 0.0) * lgf), 0.0)
            + jnp.where(diff <= 0, jnp.exp(jnp.maximum(-diff, 0.0) * lgb), 0.0))
    ri = lax.broadcasted_iota(jnp.int32, (c_len, 1), 0).astype(F32)
    qdec_f = jnp.exp((ri + 1.0) * lgf)
    qdec_b = jnp.exp((c_len - ri) * lgb)
    kdec_f = jnp.exp((c_len - 1.0 - ri) * lgf)
    kdec_b = jnp.exp(ri * lgb)
    one = jnp.ones((1, 1), F32)
    cdec_f = jnp.exp(one * (c_len * lgf))
    cdec_b = jnp.exp(one * (c_len * lgb))
    kscale = B_QK_DIM ** -0.5

    def roped(load, lo):
        x = load(lo).astype(F32)
        if rope:
            t = [tab[pl.ds(lo, c_len), :] for tab in tabs]
            x = _rope(x, t[0], t[1], t[2], B_QK_DIM // 4)
        return x

    def increments(c, carry):
        lo = pl.multiple_of(c * c_len, c_len)
        kc = roped(io["k"], lo) * kscale
        kk = jnp.concatenate([kc * kdec_f, kc * kdec_b], axis=1).T.astype(BF16)
        kv = jnp.dot(kk, io["v"](lo), preferred_element_type=F32)
        kvf_ref[c] = kv[:B_QK_DIM]
        kvb_ref[c] = kv[B_QK_DIM:]
        return carry

    lax.fori_loop(0, nc, increments, 0, unroll=_fit(nc, 8))

    def scan_f(c, s):
        state_ref[c, :B_QK_DIM, :] = s.astype(BF16)
        return cdec_f * s + kvf_ref[c]

    def scan_b(t, s):
        c = nc - 1 - t
        state_ref[c, B_QK_DIM:, :] = s.astype(BF16)
        return cdec_b * s + kvb_ref[c]

    s_f = lax.fori_loop(0, nc, scan_f, io["s0f"], unroll=_fit(nc, 2))
    s_b = lax.fori_loop(0, nc, scan_b, io["s0b"], unroll=_fit(nc, 2))
    io["store_state"](s_f, s_b)

    def outputs(c, carry):
        lo = pl.multiple_of(c * c_len, c_len)
        qc = roped(io["q"], lo)
        kc = (roped(io["k"], lo) * kscale).astype(BF16)
        s = lax.dot_general(qc.astype(BF16), kc, (((1,), (1,)), ((), ())), preferred_element_type=F32) * dmat
        o = jnp.dot(s.astype(BF16), io["v"](lo), preferred_element_type=F32)
        qq = jnp.concatenate([qc * qdec_f, qc * qdec_b], axis=1).astype(BF16)
        o = o + jnp.dot(qq, state_ref[c], preferred_element_type=F32)
        gate = io["gate"](lo).astype(F32)
        io["store_o"](lo, (gate * jax.nn.sigmoid(gate) * _rms(o, io["gn"])).astype(BF16))
        return carry

    lax.fori_loop(0, nc, outputs, 0, unroll=_fit(nc, 4))


def _ret_kernel(lg_ref, *refs, n, rope, emit_state, all_heads, n_alias):
    refs = list(refs)
    data = [refs.pop(0) for _ in range(2 if all_heads else 4)]
    gn_ref, s0f_ref, s0b_ref = refs.pop(0), refs.pop(0), refs.pop(0)
    tabs = [refs.pop(0) for _ in range(3)] if rope else None
    del refs[:n_alias]
    o_ref = refs.pop(0)
    sfo_ref, sbo_ref = (refs.pop(0), refs.pop(0)) if emit_state else (None, None)
    kvf_ref, kvb_ref, state_ref = refs
    nc = n // B_CHUNK
    rows = lambda lo: pl.ds(lo, B_CHUNK)
    qw, vw = B_HEADS * B_QK_DIM, B_V_DIM

    for hh in range(B_HEADS if all_heads else 1):
        if all_heads:
            h = hh
            blk1, blk2 = data
            v_src = (blk1, 2 * qw + hh * vw) if hh < 2 else (blk2, (hh - 2) * vw)
            col = lambda ref, c0, w: (lambda lo: ref[rows(lo), c0:c0 + w])
            io = dict(
                q=col(blk1, hh * B_QK_DIM, B_QK_DIM), k=col(blk1, qw + hh * B_QK_DIM, B_QK_DIM),
                v=col(v_src[0], v_src[1], vw), gate=col(blk2, 2 * vw + hh * vw, vw),
                gn=gn_ref[:, hh * vw:(hh + 1) * vw], s0f=s0f_ref[hh], s0b=s0b_ref[hh])

            def store_o(lo, val, hh=hh):
                o_ref[rows(lo), hh * vw:(hh + 1) * vw] = val

            def store_state(sf, sb, hh=hh):
                if emit_state:
                    sfo_ref[hh] = sf
                    sbo_ref[hh] = sb
        else:
            h = pl.program_id(1)
            whole = lambda ref: (lambda lo: ref[rows(lo), :])
            io = dict(q=whole(data[0]), k=whole(data[1]), v=whole(data[2]), gate=whole(data[3]),
                      gn=gn_ref[...], s0f=s0f_ref[...], s0b=s0b_ref[...])

            def store_o(lo, val):
                o_ref[rows(lo), :] = val

            def store_state(sf, sb):
                if emit_state:
                    sfo_ref[...] = sf
                    sbo_ref[...] = sb
        io.update(store_o=store_o, store_state=store_state)
        _ret_head(lg_ref[0, h], lg_ref[1, h], io, tabs, nc, kvf_ref.at[hh], kvb_ref.at[hh], state_ref.at[hh])


def _retention(proj, lg, gn, s0f, s0b, s0_layer, layer, tabs, batch, n, *, emit_state, all_heads, depth=None,
               state_bufs=None):
    rope = tabs is not None
    assert all_heads or not emit_state
    if all_heads:
        wide = (OFF_CQ - OFF_BQ) // 2
        assert OFF_BQ % wide == 0 and wide == 2 * B_HEADS * B_QK_DIM + 2 * B_V_DIM
        st_spec = pl.BlockSpec((None, None, B_HEADS, B_QK_DIM, B_V_DIM), lambda b, h, lg_: (b, s0_layer, 0, 0, 0))
        in_specs = [
            pl.BlockSpec((n, wide), lambda b, h, lg_: (b, OFF_BQ // wide)),
            pl.BlockSpec((n, wide), lambda b, h, lg_: (b, OFF_BQ // wide + 1)),
            pl.BlockSpec((None, 1, B_HEADS * B_V_DIM), lambda b, h, lg_: (layer, 0, 0)),
        ]
        args = [proj, proj, gn]
        out_specs = [pl.BlockSpec((n, B_HEADS * B_V_DIM), lambda b, h, lg_: (b, 0))]
        heads_per_step = B_HEADS
    else:
        qb, kb = OFF_BQ // B_QK_DIM, OFF_BK // B_QK_DIM
        vb, gb = OFF_BV // B_V_DIM, OFF_BG // B_V_DIM
        st_spec = pl.BlockSpec((None, None, None, B_QK_DIM, B_V_DIM), lambda b, h, lg_: (b, s0_layer, h, 0, 0))
        in_specs = [
            pl.BlockSpec((n, B_QK_DIM), lambda b, h, lg_: (b, qb + h)),
            pl.BlockSpec((n, B_QK_DIM), lambda b, h, lg_: (b, kb + h)),
            pl.BlockSpec((n, B_V_DIM), lambda b, h, lg_: (b, vb + h)),
            pl.BlockSpec((n, B_V_DIM), lambda b, h, lg_: (b, gb + h)),
            pl.BlockSpec((None, 1, B_V_DIM), lambda b, h, lg_: (layer, 0, h)),
        ]
        args = [proj, proj, proj, proj, gn]
        out_specs = [pl.BlockSpec((n, B_V_DIM), lambda b, h, lg_: (b, h))]
        heads_per_step = 1
    in_specs += [st_spec, st_spec]
    args += [s0f, s0b]
    if rope:
        in_specs += [pl.BlockSpec((n, LANE), lambda b, h, lg_: (0, 0))] * 3
        args += list(tabs)
    out_shape = [jax.ShapeDtypeStruct((batch * n, B_HEADS * B_V_DIM), BF16)]
    aliases = {}
    if emit_state:
        out_specs += [pl.BlockSpec((None, None, B_HEADS, B_QK_DIM, B_V_DIM),
                                   lambda b, h, lg_: (b, layer, 0, 0, 0))] * 2
        out_shape += [jax.ShapeDtypeStruct((batch, depth, B_HEADS, B_QK_DIM, B_V_DIM), F32)] * 2
        for j, buf in enumerate(state_bufs or ()):
            aliases[1 + len(args)] = 1 + j
            in_specs.append(pl.BlockSpec(memory_space=pl.ANY))
            args.append(buf)
    nc = n // B_CHUNK
    return pl.pallas_call(
        functools.partial(_ret_kernel, n=n, rope=rope, emit_state=emit_state, all_heads=all_heads,
                          n_alias=len(aliases)),
        grid_spec=pltpu.PrefetchScalarGridSpec(
            num_scalar_prefetch=1,
            grid=(batch, B_HEADS // heads_per_step),
            in_specs=in_specs,
            out_specs=out_specs,
            scratch_shapes=[pltpu.VMEM((heads_per_step, nc, B_QK_DIM, B_V_DIM), F32),
                            pltpu.VMEM((heads_per_step, nc, B_QK_DIM, B_V_DIM), F32),
                            pltpu.VMEM((heads_per_step, nc, 2 * B_QK_DIM, B_V_DIM), BF16)],
        ),
        out_shape=out_shape,
        input_output_aliases=aliases,
        compiler_params=_params("arbitrary", "arbitrary"),
        name="retention",
    )(lg, *args)


def _merge_kernel(oa_ref, ob_ref, oc_ref, wa_ref, wb_ref, wc_ref, ga_ref, gb_ref, gc_ref, o_ref):
    def branch(o, w, g):
        return jax.nn.sigmoid(g[...].astype(F32)) * jnp.dot(o[...], w[...], preferred_element_type=F32)

    o_ref[...] = (branch(oa_ref, wa_ref, ga_ref) + branch(ob_ref, wb_ref, gb_ref)
                  + branch(oc_ref, wc_ref, gc_ref)).astype(BF16)


def _merge(oa, ob, oc, wa, wb, wc, gates, layer, d):
    m = oa.shape[0]
    tm = min(TILE_MERGE_M, m)
    tn = min(TILE_MERGE_N, d)
    nb = d // tn
    o_spec = lambda o: pl.BlockSpec((tm, o.shape[1]), lambda i, j: (i, 0))
    w_spec = lambda w: pl.BlockSpec((None, w.shape[1], tn), lambda i, j: (layer, 0, j))
    g_spec = lambda br: pl.BlockSpec((tm, tn), lambda i, j: (i, br * nb + j))
    return pl.pallas_call(
        _merge_kernel,
        grid=(m // tm, nb),
        in_specs=[o_spec(oa), o_spec(ob), o_spec(oc), w_spec(wa), w_spec(wb), w_spec(wc),
                  g_spec(0), g_spec(1), g_spec(2)],
        out_specs=pl.BlockSpec((tm, tn), lambda i, j: (i, j)),
        out_shape=jax.ShapeDtypeStruct((m, d), BF16),
        compiler_params=_params("arbitrary", "arbitrary"),
        name="merge",
    )(oa, ob, oc, wa, wb, wc, gates, gates, gates)


def _rope_tables(n_tokens, dim, pad):
    pos = jnp.arange(n_tokens, dtype=jnp.int32)
    row = (pos // GRID_W).astype(F32)[:, None]
    col = (pos % GRID_W).astype(F32)[:, None]
    quarter = dim // 4
    inv = ROPE_THETA ** (-jnp.arange(quarter, dtype=F32) / quarter)[None, :]
    ang = jnp.concatenate([row * inv, row * inv, col * inv, col * inv], axis=1)
    cos, sin = jnp.cos(ang), jnp.sin(ang)
    first = (jnp.arange(dim) // quarter) % 2 == 0
    sin_lo = jnp.where(first[None, :], -sin, 0.0)
    sin_hi = jnp.where(first[None, :], 0.0, sin)
    if pad:
        z = jnp.zeros_like(cos)
        tabs = [jnp.concatenate([t, z], axis=1) for t in (cos, sin_lo, sin_hi)]
    else:
        tabs = [jnp.tile(t, (1, LANE // dim)) for t in (cos, sin_lo, sin_hi)]
    return tuple(tabs)


def kernel(x_prompt, x_sample, c, cache_attn_k, cache_attn_v, state_ret_fwd, state_ret_bwd, cache_mla_ckv, cache_mla_krope, c_ctx, w_mod, b_mod, g_pre_mix, g_post_mix, g_pre_mlp, g_post_mlp, w_in, attn_q_norm, attn_k_norm, ret_decay_fwd, ret_decay_bwd, ret_gn, mla_q_norm, mla_kv_norm, w_mla_uq, w_mla_ukv, w_branch_a, w_branch_b, w_branch_c, w_out, w_mlp_up, w_mlp_down):
    bp, n_p, d = x_prompt.shape
    bs, n_s, _ = x_sample.shape
    depth = w_in.shape[0]
    past = cache_attn_k.shape[2]
    assert n_p % B_CHUNK == 0 and n_s % B_CHUNK == 0

    w_in_t = jnp.transpose(w_in, (0, 2, 1))
    gate_w = w_in.shape[-1] - IN_MAIN
    w_uq = jnp.pad(w_mla_uq.reshape(depth, C_Q_RANK, C_HEADS, C_NOPE_DIM + C_ROPE_DIM),
                   ((0, 0), (0, 0), (0, 0), (0, C_QK_PAD - C_NOPE_DIM - C_ROPE_DIM)))
    w_uq = w_uq.reshape(depth, C_Q_RANK, C_HEADS * C_QK_PAD).astype(BF16)
    w_ukv = w_mla_ukv.astype(BF16)
    wa, wb, wc = w_branch_a.astype(BF16), w_branch_b.astype(BF16), w_branch_c.astype(BF16)
    w_o = w_out.astype(BF16)
    vec = lambda g: g.reshape(depth, 1, g.shape[-1])
    g_pre_mix, g_post_mix, g_pre_mlp, g_post_mlp = map(vec, (g_pre_mix, g_post_mix, g_pre_mlp, g_post_mlp))
    gqn, gkn, gcq, gckv, gret = map(vec, (attn_q_norm, attn_k_norm, mla_q_norm, mla_kv_norm, ret_gn))
    lg = jnp.stack([jax.nn.log_sigmoid(ret_decay_fwd.astype(F32)),
                    jax.nn.log_sigmoid(ret_decay_bwd.astype(F32))], axis=1)

    rows = -(-(1 + bs) // 8) * 8
    cvec = jnp.zeros((rows, d), F32).at[0].set(c_ctx).at[1:1 + bs].set(c)
    mod = _modulation(cvec, w_mod, b_mod).reshape(depth, rows, 6, d)
    mod = jnp.pad(mod, ((0, 0), (0, 0), (0, MOD_ROWS - 6), (0, 0)))

    tabs128 = _rope_tables(n_s, HEAD_DIM, pad=False)
    tabs64 = _rope_tables(n_s, C_ROPE_DIM, pad=False)
    tabs64p = _rope_tables(n_s, C_ROPE_DIM, pad=True)

    to_layers = lambda t: jnp.swapaxes(t.reshape(bs, depth, past, -1), 0, 1).reshape(depth, bs * past, -1)
    ctx_k, ctx_v = to_layers(cache_attn_k.astype(BF16)), to_layers(cache_attn_v.astype(BF16))
    ctx_kr = jnp.pad(cache_mla_krope, ((0, 0), (0, 0), (0, 0), (0, LANE - C_ROPE_DIM)))
    ctx_c = _prep_ctx_ckv(to_layers(cache_mla_ckv), to_layers(ctx_kr), gckv, w_ukv)
    zero_state = jnp.zeros((bp, 1, B_HEADS, B_QK_DIM, B_V_DIM), F32)

    streams = {
        "p": dict(x=x_prompt.reshape(bp * n_p, d), batch=bp, n=n_p, grp=lambda r: 0, latent=False),
        "s": dict(x=x_sample.reshape(bs * n_s, d), batch=bs, n=n_s, grp=lambda r: 1 + r // n_s, latent=True),
    }
    for st in streams.values():
        st["h"] = _prenorm(st["x"], g_pre_mix, mod, 0, st["grp"])
    kvw = A_KV_HEADS * HEAD_DIM
    cache_bufs = tuple(jnp.zeros((bp, depth, n_p, w), F32) for w in (kvw, kvw, C_KV_RANK, C_ROPE_DIM))
    state_bufs = tuple(jnp.zeros((bp, depth, B_HEADS, B_QK_DIM, B_V_DIM), F32) for _ in range(2))

    for l in range(depth):
        for st in streams.values():
            batch, n, grp, latent = st["batch"], st["n"], st["grp"], st["latent"]
            proj = _matmul(st["h"], w_in_t, l, tm=TILE_MATMUL_M, tn=TILE_W_IN_N, w_transposed=True, n=MAIN_W,
                           name="w_in")
            gates = _matmul(st["h"], w_in_t, l, tm=TILE_MATMUL_M, tn=TILE_WIDE_N, w_transposed=True,
                            w_row0=IN_MAIN, n=gate_w, name="w_gates")
            t128 = tabs128 if latent else None

            tabs = (tabs128, tabs64, tabs64p) if latent else None
            pa, qc, pc = _prep_all(proj, gqn, gkn, gcq, gckv, w_uq, w_ukv, l, tabs, n, cache=not latent,
                                   depth=depth, cache_bufs=None if latent else cache_bufs)
            if not latent:
                cache_bufs = (pa[2], pa[3], pc[2], pc[3])

            qa, ka = pa[0], pa[1]
            ctx_a = (ctx_k, ctx_v) if latent else None
            o_a = _attention(qa, ka, 0, proj, OFF_AV // HEAD_DIM, ctx_a, batch, n,
                             groups=A_KV_HEADS, gps=1 if latent else A_KV_HEADS, rep=A_HEADS // A_KV_HEADS,
                             dqk=HEAD_DIM, dv=HEAD_DIM, rows=ATTN_ROWS, name="attn_a", ctx_layer=l)

            if latent:
                s0f, s0b, s0_layer = state_ret_fwd, state_ret_bwd, l
            else:
                s0f, s0b, s0_layer = zero_state, zero_state, 0
            rb = _retention(proj, lg[l], gret, s0f, s0b, s0_layer, l, t128, batch, n,
                            emit_state=not latent, all_heads=not latent, depth=depth,
                            state_bufs=None if latent else state_bufs)
            o_b = rb[0]
            if not latent:
                state_bufs = (rb[1], rb[2])

            kc, vc = pc[0], pc[1]
            o_c = _attention(qc, kc, 0, vc, 0, ctx_c if latent else None, batch, n,
                             groups=C_HEADS, gps=1 if latent else C_HEADS, rep=1,
                             dqk=C_QK_PAD, dv=C_V_DIM, rows=ATTN_ROWS, name="attn_c", ctx_layer=l)

            merged = _merge(o_a, o_b, o_c, wa, wb, wc, gates, l, d)
            group_rows = n if latent else batch * n
            x, h2 = _matmul_row(merged, w_o, st["x"], g_post_mix, g_pre_mlp, mod, l, l, grp,
                                gate_row=2, next_rows=(3, 4), tm=min(TILE_W_OUT_M, group_rows), tk=TILE_W_OUT_K,
                                name="w_out")
            hid = _matmul(h2, w_mlp_up, l, tm=TILE_MATMUL_M, tn=TILE_WIDE_N, relu2=True, name="mlp_up")
            last = l == depth - 1
            res = _matmul_row(hid, w_mlp_down, x, g_post_mlp, g_pre_mix, mod, l, min(l + 1, depth - 1), grp,
                              gate_row=5, next_rows=None if last else (0, 1),
                              tm=min(TILE_MLP_DOWN_M, group_rows), tk=TILE_MLP_DOWN_K, name="mlp_down")
            st["x"] = res[0]
            st["h"] = None if last else res[1]

    y_p = streams["p"]["x"].reshape(bp, n_p, d)
    y_s = streams["s"]["x"].reshape(bs, n_s, d)
    nk, nv, new_mla_ckv, new_mla_krope = cache_bufs
    new_attn_k = nk.reshape(bp, depth, n_p, A_KV_HEADS, HEAD_DIM)
    new_attn_v = nv.reshape(bp, depth, n_p, A_KV_HEADS, HEAD_DIM)
    new_ret_fwd, new_ret_bwd = state_bufs
    return (y_p, y_s, new_attn_k, new_attn_v, new_ret_fwd, new_ret_bwd, new_mla_ckv, new_mla_krope)
```

```python
import functools

import jax
import jax.numpy as jnp
from jax import lax
from jax.experimental import pallas as pl
from jax.experimental.pallas import tpu as pltpu

F32 = jnp.float32
BF16 = jnp.bfloat16

EPS = 1e-6
LOG2E = 1.4426950408889634
ROPE_THETA = 10000.0
GRID_W = 64
LANE = 128
HEAD_DIM = 128
A_HEADS = 8
A_KV_HEADS = 2
B_HEADS = 4
B_QK_DIM = 128
B_V_DIM = 256
B_CHUNK = 128
C_HEADS = 8
C_Q_RANK = 512
C_KV_RANK = 256
C_NOPE_DIM = 128
C_ROPE_DIM = 64
C_V_DIM = 128
C_QK_PAD = 256

OFF_AQ = 0
OFF_AK = OFF_AQ + A_HEADS * HEAD_DIM
OFF_AV = OFF_AK + A_KV_HEADS * HEAD_DIM
OFF_BQ = OFF_AV + A_KV_HEADS * HEAD_DIM
OFF_BK = OFF_BQ + B_HEADS * B_QK_DIM
OFF_BV = OFF_BK + B_HEADS * B_QK_DIM
OFF_BG = OFF_BV + B_HEADS * B_V_DIM
OFF_CQ = OFF_BG + B_HEADS * B_V_DIM
OFF_CKV = OFF_CQ + C_Q_RANK
OFF_CKR = OFF_CKV + C_KV_RANK
IN_MAIN = OFF_CKR + C_ROPE_DIM
MAIN_W = 5632
MOD_ROWS = 8
EPILOGUE_ROWS = 16

VMEM_LIMIT = 56 * 1024 * 1024

TILE_ROWS_ELEMENTWISE = 512
TILE_MOD_N = 1024
TILE_MATMUL_M = 2048
TILE_W_IN_N = 512
TILE_WIDE_N = 1024
TILE_MERGE_M, TILE_MERGE_N = 1024, 1024
TILE_W_OUT_M, TILE_W_OUT_K = 512, 2048
TILE_MLP_DOWN_M, TILE_MLP_DOWN_K = 1024, 1024
ATTN_ROWS = 2048
ATTN_KEYS = 512


def _params(*sem):
    return pltpu.CompilerParams(dimension_semantics=sem, vmem_limit_bytes=VMEM_LIMIT)


def _rms(x, g):
    return x * lax.rsqrt(jnp.mean(x * x, axis=-1, keepdims=True) + EPS) * g


def _fit(total, want):
    while total % want:
        want //= 2
    return want


def _rope(x, cos, sin_lo, sin_hi, shift):
    return (x * cos + pltpu.roll(x, LANE - shift, 1) * sin_lo + pltpu.roll(x, shift, 1) * sin_hi)


def _mod_kernel(c_ref, w_ref, b_ref, o_ref):
    c = c_ref[...]
    a = (c * jax.nn.sigmoid(c)).astype(BF16)
    o_ref[...] = jnp.dot(a, w_ref[...].astype(BF16), preferred_element_type=F32) + b_ref[...]


def _modulation(cvec, w_mod, b_mod):
    depth, d, n = w_mod.shape
    rows = cvec.shape[0]
    tn = min(TILE_MOD_N, n)
    return pl.pallas_call(
        _mod_kernel,
        grid=(depth, n // tn),
        in_specs=[
            pl.BlockSpec((rows, d), lambda l, j: (0, 0)),
            pl.BlockSpec((None, d, tn), lambda l, j: (l, 0, j)),
            pl.BlockSpec((None, 1, tn), lambda l, j: (l, 0, j)),
        ],
        out_specs=pl.BlockSpec((None, rows, tn), lambda l, j: (l, 0, j)),
        out_shape=jax.ShapeDtypeStruct((depth, rows, n), F32),
        compiler_params=_params("arbitrary", "arbitrary"),
        name="modulation",
    )(cvec, w_mod, b_mod.reshape(depth, 1, n))


def _prenorm_kernel(x_ref, g_ref, mod_ref, h_ref):
    y = _rms(x_ref[...], g_ref[...])
    h_ref[...] = (y * (1.0 + mod_ref[1:2, :]) + mod_ref[0:1, :]).astype(BF16)


def _prenorm(x, g, mod, layer, grp):
    m, d = x.shape
    tm = min(TILE_ROWS_ELEMENTWISE, m)
    return pl.pallas_call(
        _prenorm_kernel,
        grid=(m // tm,),
        in_specs=[
            pl.BlockSpec((tm, d), lambda i: (i, 0)),
            pl.BlockSpec((None, 1, d), lambda i: (layer, 0, 0)),
            pl.BlockSpec((None, None, MOD_ROWS, d), lambda i: (layer, grp(i * tm), 0, 0)),
        ],
        out_specs=pl.BlockSpec((tm, d), lambda i: (i, 0)),
        out_shape=jax.ShapeDtypeStruct((m, d), BF16),
        compiler_params=_params("arbitrary"),
        name="prenorm",
    )(x, g, mod)


def _mm_kernel(a_ref, w_ref, o_ref, *, relu2, w_transposed):
    w = (w_ref[0] if w_transposed else w_ref[...]).astype(BF16)
    contract = (((1,), (1 if w_transposed else 0,)), ((), ()))
    y = lax.dot_general(a_ref[...], w, contract, preferred_element_type=F32)
    if relu2:
        y = jnp.square(jnp.maximum(y, 0.0))
    o_ref[...] = y.astype(o_ref.dtype)


def _matmul(a, w, layer, *, tm, tn, relu2=False, w_transposed=False, w_row0=0, n=None, name):
    m, k = a.shape
    n = n if w_transposed else w.shape[2]
    tm, tn = _fit(m, min(tm, m)), _fit(n, min(tn, n))
    if w_transposed:
        w_spec = pl.BlockSpec((pl.Element(1), pl.Element(tn), pl.Element(k)),
                              lambda i, j: (layer, pl.multiple_of(w_row0 + j * tn, 64), 0))
    else:
        w_spec = pl.BlockSpec((None, k, tn), lambda i, j: (layer, 0, j))
    return pl.pallas_call(
        functools.partial(_mm_kernel, relu2=relu2, w_transposed=w_transposed),
        grid=(m // tm, n // tn),
        in_specs=[
            pl.BlockSpec((tm, k), lambda i, j: (i, 0)),
            w_spec,
        ],
        out_specs=pl.BlockSpec((tm, tn), lambda i, j: (i, j)),
        out_shape=jax.ShapeDtypeStruct((m, n), BF16),
        compiler_params=_params("arbitrary", "arbitrary"),
        name=name,
    )(a, w)


def _mm_row_kernel(a_ref, w_ref, x_ref, gpost_ref, mod_ref, gnext_ref, modn_ref, *rest,
                   gate_row, next_rows, nk, nt):
    if next_rows is None:
        xo_ref, acc0_ref, acc1_ref = rest
    else:
        xo_ref, ho_ref, acc0_ref, acc1_ref = rest
    i = pl.program_id(0)
    k = pl.program_id(1)
    tm, d = acc0_ref.shape
    sub = tm // nk

    @pl.when((i == 0) & (k == 0))
    def _():
        acc0_ref[...] = jnp.zeros_like(acc0_ref)
        acc1_ref[...] = jnp.zeros_like(acc1_ref)

    def epilogue(acc_ref):
        gain = gpost_ref[...] * mod_ref[gate_row:gate_row + 1, :]
        if next_rows is not None:
            shift_row, scale_row = next_rows
            gain_next = gnext_ref[...] * (1.0 + modn_ref[scale_row:scale_row + 1, :])
            shift_next = modn_ref[shift_row:shift_row + 1, :]
        base = 0 if nk == 1 else pl.multiple_of(k * sub, sub)
        for c in range(sub // EPILOGUE_ROWS):
            rows = pl.ds(base + c * EPILOGUE_ROWS, EPILOGUE_ROWS)
            f = acc_ref[rows, :]
            acc_ref[rows, :] = jnp.zeros((EPILOGUE_ROWS, d), F32)
            blk = pl.ds(c * EPILOGUE_ROWS, EPILOGUE_ROWS)
            xn = x_ref[blk, :] + _rms(f, gain)
            xo_ref[blk, :] = xn
            if next_rows is not None:
                ho_ref[blk, :] = (_rms(xn, gain_next) + shift_next).astype(BF16)

    @pl.when(i == 0)
    def _():
        acc0_ref[...] += jnp.dot(a_ref[...], w_ref[...].astype(BF16), preferred_element_type=F32)

    for parity, (acc_mm, acc_ep) in enumerate(((acc0_ref, acc1_ref), (acc1_ref, acc0_ref))):
        @pl.when((i > 0) & (i < nt) & (i % 2 == parity))
        def _():
            epilogue(acc_ep)
            acc_mm[...] += jnp.dot(a_ref[...], w_ref[...].astype(BF16), preferred_element_type=F32)

    @pl.when(i == nt)
    def _():
        epilogue(acc0_ref if (nt - 1) % 2 == 0 else acc1_ref)


def _matmul_row(a, w, x, gpost, gnext, mod, layer, next_layer, grp, *, gate_row, next_rows, tm, tk, name):
    m, kdim = a.shape
    d = w.shape[-1]
    tm, tk = min(tm, m), min(tk, kdim)
    nk, nt = kdim // tk, m // tm
    assert tm % nk == 0 and (tm // nk) % 16 == 0
    cur = lambda i: jnp.minimum(i, nt - 1)
    kcur = lambda i, k: jnp.where(i < nt, k, nk - 1)
    prev = lambda i: jnp.maximum(i - 1, 0)
    sub = tm // nk
    slab = lambda i, k: (prev(i) * nk + jnp.where(i == 0, 0, k), 0)
    vec = pl.BlockSpec((None, 1, d), lambda i, k: (layer, 0, 0))
    vecn = pl.BlockSpec((None, 1, d), lambda i, k: (next_layer, 0, 0))
    out_specs = [pl.BlockSpec((sub, d), slab)]
    out_shape = [jax.ShapeDtypeStruct((m, d), F32)]
    if next_rows is not None:
        out_specs.append(pl.BlockSpec((sub, d), slab))
        out_shape.append(jax.ShapeDtypeStruct((m, d), BF16))
    return pl.pallas_call(
        functools.partial(_mm_row_kernel, gate_row=gate_row, next_rows=next_rows, nk=nk, nt=nt),
        grid=(nt + 1, nk),
        in_specs=[
            pl.BlockSpec((tm, tk), lambda i, k: (cur(i), kcur(i, k))),
            pl.BlockSpec((None, tk, d), lambda i, k: (layer, kcur(i, k), 0)),
            pl.BlockSpec((sub, d), slab),
            vec,
            pl.BlockSpec((None, None, MOD_ROWS, d), lambda i, k: (layer, grp(prev(i) * tm), 0, 0)),
            vecn,
            pl.BlockSpec((None, None, MOD_ROWS, d), lambda i, k: (next_layer, grp(prev(i) * tm), 0, 0)),
        ],
        out_specs=out_specs,
        out_shape=out_shape,
        scratch_shapes=[pltpu.VMEM((tm, d), F32), pltpu.VMEM((tm, d), F32)],
        compiler_params=_params("arbitrary", "arbitrary"),
        name=name,
    )(a, w, x, gpost, mod, gnext, mod)


def _prep_a_kernel(p_ref, gq_ref, gk_ref, *rest, rope, cache):
    rest = list(rest)
    tabs = [rest.pop(0) for _ in range(3)] if rope else None
    q_ref, k_ref = rest[:2]
    scale = HEAD_DIM ** -0.5 * LOG2E
    for h in range(A_HEADS):
        y = _rms(p_ref[:, h * HEAD_DIM:(h + 1) * HEAD_DIM].astype(F32), gq_ref[...])
        if rope:
            y = _rope(y, tabs[0][...], tabs[1][...], tabs[2][...], HEAD_DIM // 4)
        q_ref[:, h * HEAD_DIM:(h + 1) * HEAD_DIM] = (y * scale).astype(BF16)
    for g in range(A_KV_HEADS):
        lo = OFF_AK + g * HEAD_DIM
        y = _rms(p_ref[:, lo:lo + HEAD_DIM].astype(F32), gk_ref[...])
        if cache:
            rest[2][:, g * HEAD_DIM:(g + 1) * HEAD_DIM] = y
        if rope:
            y = _rope(y, tabs[0][...], tabs[1][...], tabs[2][...], HEAD_DIM // 4)
        k_ref[:, g * HEAD_DIM:(g + 1) * HEAD_DIM] = y.astype(BF16)
    if cache:
        rest[3][...] = p_ref[:, OFF_AV:OFF_BQ].astype(F32)


def _prep_cq_kernel(p_ref, g_ref, w_ref, *rest, rope):
    rest = list(rest)
    tabs = [rest.pop(0) for _ in range(3)] if rope else None
    q_ref = rest[0]
    scale = (C_NOPE_DIM + C_ROPE_DIM) ** -0.5 * LOG2E
    y = _rms(p_ref[...].astype(F32), g_ref[...]).astype(BF16)
    z = jnp.dot(y, w_ref[...], preferred_element_type=F32)
    for h in range(C_HEADS):
        lo = h * C_QK_PAD
        q_ref[:, lo:lo + C_NOPE_DIM] = (z[:, lo:lo + C_NOPE_DIM] * scale).astype(BF16)
        r = z[:, lo + C_NOPE_DIM:lo + C_QK_PAD]
        if rope:
            r = _rope(r, tabs[0][...], tabs[1][...], tabs[2][...], C_ROPE_DIM // 4)
        q_ref[:, lo + C_NOPE_DIM:lo + C_QK_PAD] = (r * scale).astype(BF16)


def _prep_ckv_kernel(ckv_ref, kr_ref, g_ref, w_ref, *rest, norm, rope, cache):
    rest = list(rest)
    tabs = [rest.pop(0) for _ in range(3)] if rope else None
    kc_ref, vc_ref = rest[:2]
    x = ckv_ref[...].astype(F32)
    if norm:
        x = _rms(x, g_ref[...])
    kr = kr_ref[...].astype(F32)
    if cache:
        rest[2][...] = x
        rest[3][...] = kr[:, :C_ROPE_DIM]
    if rope:
        kr = _rope(kr, tabs[0][...], tabs[1][...], tabs[2][...], C_ROPE_DIM // 4)
    krb = kr.astype(BF16)
    z = jnp.dot(x.astype(BF16), w_ref[...], preferred_element_type=F32)
    up = C_NOPE_DIM + C_V_DIM
    for h in range(C_HEADS):
        kc_ref[:, h * C_QK_PAD:h * C_QK_PAD + C_NOPE_DIM] = z[:, h * up:h * up + C_NOPE_DIM].astype(BF16)
        kc_ref[:, h * C_QK_PAD + C_NOPE_DIM:(h + 1) * C_QK_PAD] = krb
        vc_ref[:, h * C_V_DIM:(h + 1) * C_V_DIM] = z[:, h * up + C_NOPE_DIM:(h + 1) * up].astype(BF16)


def _prep_ctx_ckv(ckv, kr, g, w_ukv):
    depth, m, _ = ckv.shape
    tr = min(TILE_ROWS_ELEMENTWISE, m)
    row = lambda w: pl.BlockSpec((None, tr, w), lambda l, i: (l, i, 0))
    return pl.pallas_call(
        functools.partial(_prep_ckv_kernel, norm=False, rope=False, cache=False),
        grid=(depth, m // tr),
        in_specs=[row(C_KV_RANK), row(LANE),
                  pl.BlockSpec((None, 1, C_KV_RANK), lambda l, i: (l, 0, 0)),
                  pl.BlockSpec((None,) + w_ukv.shape[1:], lambda l, i: (l, 0, 0))],
        out_specs=[row(C_HEADS * C_QK_PAD), row(C_HEADS * C_V_DIM)],
        out_shape=[jax.ShapeDtypeStruct((depth, m, C_HEADS * C_QK_PAD), BF16),
                   jax.ShapeDtypeStruct((depth, m, C_HEADS * C_V_DIM), BF16)],
        compiler_params=_params("arbitrary", "arbitrary"),
        name="prep_ctx_ckv",
    )(ckv, kr, g, w_ukv)


def _prep_all_kernel(pa_ref, pcq_ref, ckv_ref, kr_ref, gq_ref, gk_ref, gcq_ref, gckv_ref, wuq_ref, wukv_ref,
                     *rest, rope, cache, n_alias):
    rest = list(rest)
    t128, t64, t64p = ([rest.pop(0) for _ in range(3)] if rope else [] for _ in range(3))
    del rest[:n_alias]
    n_out = 4 if cache else 2
    a_out = [rest.pop(0) for _ in range(n_out)]
    cq_out = [rest.pop(0)]
    ckv_out = [rest.pop(0) for _ in range(n_out)]
    _prep_a_kernel(pa_ref, gq_ref, gk_ref, *t128, *a_out, rope=rope, cache=cache)
    _prep_cq_kernel(pcq_ref, gcq_ref, wuq_ref, *t64p, *cq_out, rope=rope)
    _prep_ckv_kernel(ckv_ref, kr_ref, gckv_ref, wukv_ref, *t64, *ckv_out, norm=True, rope=rope, cache=cache)


def _prep_all(proj, gq, gk, gcq, gckv, w_uq, w_ukv, layer, tabs, n_per, *, cache, depth=None, cache_bufs=None):
    m = proj.shape[0]
    tr = min(TILE_ROWS_ELEMENTWISE, n_per)
    nt = n_per // tr
    rope = tabs is not None
    kvw = A_KV_HEADS * HEAD_DIM
    vec = lambda w: pl.BlockSpec((None, 1, w), lambda i: (layer, 0, 0))
    mat = lambda w: pl.BlockSpec((None,) + w.shape[1:], lambda i: (layer, 0, 0))
    row = lambda w, blk=0: pl.BlockSpec((tr, w), lambda i: (i, blk))
    in_specs = [row(OFF_BQ), row(C_Q_RANK, OFF_CQ // C_Q_RANK), row(C_KV_RANK, OFF_CKV // C_KV_RANK),
                row(LANE, OFF_CKR // LANE), vec(HEAD_DIM), vec(HEAD_DIM), vec(C_Q_RANK), vec(C_KV_RANK),
                mat(w_uq), mat(w_ukv)]
    args = [proj, proj, proj, proj, gq, gk, gcq, gckv, w_uq, w_ukv]
    if rope:
        in_specs += [pl.BlockSpec((tr, LANE), lambda i: (i % nt, 0))] * 9
        args += [t for group in tabs for t in group]
    out = lambda w, dt: (row(w), jax.ShapeDtypeStruct((m, w), dt))
    stacked = lambda w: (pl.BlockSpec((None, None, tr, w), lambda i: (i // nt, layer, i % nt, 0)),
                         jax.ShapeDtypeStruct((m // n_per, depth, n_per, w), F32))
    outs = [out(A_HEADS * HEAD_DIM, BF16), out(kvw, BF16)]
    outs += [stacked(kvw), stacked(kvw)] if cache else []
    outs += [out(C_HEADS * C_QK_PAD, BF16), out(C_HEADS * C_QK_PAD, BF16), out(C_HEADS * C_V_DIM, BF16)]
    outs += [stacked(C_KV_RANK), stacked(C_ROPE_DIM)] if cache else []
    aliases = {}
    if cache_bufs is not None:
        for buf, out_idx in zip(cache_bufs, (2, 3, 7, 8)):
            aliases[len(args)] = out_idx
            in_specs.append(pl.BlockSpec(memory_space=pl.ANY))
            args.append(buf)
    res = pl.pallas_call(
        functools.partial(_prep_all_kernel, rope=rope, cache=cache, n_alias=len(aliases)),
        grid=(m // tr,),
        in_specs=in_specs,
        out_specs=[o[0] for o in outs],
        out_shape=[o[1] for o in outs],
        input_output_aliases=aliases,
        compiler_params=_params("arbitrary"),
        name="prep_all",
    )(*args)
    n_out = 4 if cache else 2
    return res[:n_out], res[n_out], res[n_out + 1:]


def _attn_kernel(q_ref, k_ref, v_ref, *rest, gps, rep, dqk, dv, tq, tk, ctx):
    if ctx:
        k2_ref, v2_ref, o_ref, acc_ref, s0_ref, s1_ref = rest
    else:
        o_ref, acc_ref, s0_ref, s1_ref = rest
    s_refs = (s0_ref, s1_ref)

    items = []
    for g in range(gps):
        chunks = [(k_ref, v_ref, c * tk, tk) for c in range(k_ref.shape[0] // tk)]
        if ctx:
            nctx = k2_ref.shape[0]
            chunks += [(k2_ref, v2_ref, lo, min(tk, nctx - lo)) for lo in range(0, nctx, tk)]
        items += [(g, c == 0, c == len(chunks) - 1) + ch for c, ch in enumerate(chunks)]

    def queries(g):
        cols = [q_ref[:, (g * rep + r) * dqk:(g * rep + r + 1) * dqk] for r in range(rep)]
        return cols[0] if rep == 1 else jnp.concatenate(cols, axis=0)

    def scores(t):
        g, _, _, kr, _, lo, size = items[t]
        s_refs[t % 2][:, :size] = lax.dot_general(
            queries(g), kr[lo:lo + size, g * dqk:(g + 1) * dqk], (((1,), (1,)), ((), ())),
            preferred_element_type=F32)

    scores(0)
    m = None
    for t, (g, first, last, _, vr, lo, size) in enumerate(items):
        if t + 1 < len(items):
            scores(t + 1)
        s = s_refs[t % 2][:, :size]
        m_blk = jnp.max(s, axis=-1, keepdims=True)
        m_new = m_blk if first else jnp.maximum(m, m_blk)
        p = jnp.exp2(s - m_new).astype(BF16)
        v_ext = jnp.concatenate([vr[lo:lo + size, g * dv:(g + 1) * dv], jnp.ones((size, dv), BF16)], axis=1)
        pv = jnp.dot(p, v_ext, preferred_element_type=F32)
        if first:
            acc_ref[g] = pv
        else:
            acc_ref[g] = jnp.exp2(m - m_new) * acc_ref[g] + pv
        m = m_new
        if last:
            o = acc_ref[g, :, :dv] / acc_ref[g, :, dv:]
            for r in range(rep):
                o_ref[:, (g * rep + r) * dv:(g * rep + r + 1) * dv] = o[r * tq:(r + 1) * tq].astype(o_ref.dtype)


def _attention(q, k, kblk0, v, vblk0, ctx_kv, batch, n, *, groups, gps, rep, dqk, dv, rows, name, ctx_layer=0):
    tq = min(rows // rep, n)
    tk = min(ATTN_KEYS, n)
    nq = n // tq
    assert groups % gps == 0 and kblk0 % gps == 0 and vblk0 % gps == 0 and (ctx_kv is None or gps == 1)
    in_specs = [
        pl.BlockSpec((tq, gps * rep * dqk), lambda b, g, i: (b * nq + i, g)),
        pl.BlockSpec((n, gps * dqk), lambda b, g, i: (b, kblk0 // gps + g)),
        pl.BlockSpec((n, gps * dv), lambda b, g, i: (b, vblk0 // gps + g)),
    ]
    args = [q, k, v]
    if ctx_kv is not None:
        k2, v2 = ctx_kv
        nc = k2.shape[-2] // batch
        if k2.ndim == 3:
            in_specs += [pl.BlockSpec((None, nc, dqk), lambda b, g, i: (ctx_layer, b, g)),
                         pl.BlockSpec((None, nc, dv), lambda b, g, i: (ctx_layer, b, g))]
        else:
            in_specs += [pl.BlockSpec((nc, dqk), lambda b, g, i: (b, g)),
                         pl.BlockSpec((nc, dv), lambda b, g, i: (b, g))]
        args += [k2, v2]
    return pl.pallas_call(
        functools.partial(_attn_kernel, gps=gps, rep=rep, dqk=dqk, dv=dv, tq=tq, tk=tk, ctx=ctx_kv is not None),
        grid=(batch, groups // gps, nq),
        in_specs=in_specs,
        out_specs=pl.BlockSpec((tq, gps * rep * dv), lambda b, g, i: (b * nq + i, g)),
        out_shape=jax.ShapeDtypeStruct((batch * n, groups * rep * dv), BF16),
        scratch_shapes=[pltpu.VMEM((gps, rep * tq, 2 * dv), F32),
                        pltpu.VMEM((rep * tq, tk), F32),
                        pltpu.VMEM((rep * tq, tk), F32)],
        compiler_params=_params("arbitrary", "arbitrary", "arbitrary"),
        name=name,
    )(*args)


def _ret_head(lgf, lgb, io, tabs, nc, kvf_ref, kvb_ref, state_ref):
    c_len = B_CHUNK
    rope = tabs is not None
    ii = lax.broadcasted_iota(jnp.int32, (c_len, c_len), 0)
    jj = lax.broadcasted_iota(jnp.int32, (c_len, c_len), 1)
    diff = (ii - jj).astype(F32)
    dmat = (jnp.where(diff >= 0, jnp.exp(jnp.maximum(diff, 0.0) * lgf), 0.0)
            + jnp.where(diff <= 0, jnp.exp(jnp.maximum(-diff, 0.0) * lgb), 0.0))
    ri = lax.broadcasted_iota(jnp.int32, (c_len, 1), 0).astype(F32)
    qdec_f = jnp.exp((ri + 1.0) * lgf)
    qdec_b = jnp.exp((c_len - ri) * lgb)
    kdec_f = jnp.exp((c_len - 1.0 - ri) * lgf)
    kdec_b = jnp.exp(ri * lgb)
    one = jnp.ones((1, 1), F32)
    cdec_f = jnp.exp(one * (c_len * lgf))
    cdec_b = jnp.exp(one * (c_len * lgb))
    kscale = B_QK_DIM ** -0.5

    def roped(load, lo):
        x = load(lo).astype(F32)
        if rope:
            t = [tab[pl.ds(lo, c_len), :] for tab in tabs]
            x = _rope(x, t[0], t[1], t[2], B_QK_DIM // 4)
        return x

    def increments(c, carry):
        lo = pl.multiple_of(c * c_len, c_len)
        kc = roped(io["k"], lo) * kscale
        kk = jnp.concatenate([kc * kdec_f, kc * kdec_b], axis=1).T.astype(BF16)
        kv = jnp.dot(kk, io["v"](lo), preferred_element_type=F32)
        kvf_ref[c] = kv[:B_QK_DIM]
        kvb_ref[c] = kv[B_QK_DIM:]
        return carry

    lax.fori_loop(0, nc, increments, 0, unroll=_fit(nc, 8))

    def scan_f(c, s):
        state_ref[c, :B_QK_DIM, :] = s.astype(BF16)
        return cdec_f * s + kvf_ref[c]

    def scan_b(t, s):
        c = nc - 1 - t
        state_ref[c, B_QK_DIM:, :] = s.astype(BF16)
        return cdec_b * s + kvb_ref[c]

    s_f = lax.fori_loop(0, nc, scan_f, io["s0f"], unroll=_fit(nc, 2))
    s_b = lax.fori_loop(0, nc, scan_b, io["s0b"], unroll=_fit(nc, 2))
    io["store_state"](s_f, s_b)

    def outputs(c, carry):
        lo = pl.multiple_of(c * c_len, c_len)
        qc = roped(io["q"], lo)
        kc = (roped(io["k"], lo) * kscale).astype(BF16)
        s = lax.dot_general(qc.astype(BF16), kc, (((1,), (1,)), ((), ())), preferred_element_type=F32) * dmat
        o = jnp.dot(s.astype(BF16), io["v"](lo), preferred_element_type=F32)
        qq = jnp.concatenate([qc * qdec_f, qc * qdec_b], axis=1).astype(BF16)
        o = o + jnp.dot(qq, state_ref[c], preferred_element_type=F32)
        gate = io["gate"](lo).astype(F32)
        io["store_o"](lo, (gate * jax.nn.sigmoid(gate) * _rms(o, io["gn"])).astype(BF16))
        return carry

    lax.fori_loop(0, nc, outputs, 0, unroll=_fit(nc, 4))


def _ret_kernel(lg_ref, *refs, n, rope, emit_state, all_heads, n_alias):
    refs = list(refs)
    data = [refs.pop(0) for _ in range(2 if all_heads else 4)]
    gn_ref, s0f_ref, s0b_ref = refs.pop(0), refs.pop(0), refs.pop(0)
    tabs = [refs.pop(0) for _ in range(3)] if rope else None
    del refs[:n_alias]
    o_ref = refs.pop(0)
    sfo_ref, sbo_ref = (refs.pop(0), refs.pop(0)) if emit_state else (None, None)
    kvf_ref, kvb_ref, state_ref = refs
    nc = n // B_CHUNK
    rows = lambda lo: pl.ds(lo, B_CHUNK)
    qw, vw = B_HEADS * B_QK_DIM, B_V_DIM

    for hh in range(B_HEADS if all_heads else 1):
        if all_heads:
            h = hh
            blk1, blk2 = data
            v_src = (blk1, 2 * qw + hh * vw) if hh < 2 else (blk2, (hh - 2) * vw)
            col = lambda ref, c0, w: (lambda lo: ref[rows(lo), c0:c0 + w])
            io = dict(
                q=col(blk1, hh * B_QK_DIM, B_QK_DIM), k=col(blk1, qw + hh * B_QK_DIM, B_QK_DIM),
                v=col(v_src[0], v_src[1], vw), gate=col(blk2, 2 * vw + hh * vw, vw),
                gn=gn_ref[:, hh * vw:(hh + 1) * vw], s0f=s0f_ref[hh], s0b=s0b_ref[hh])

            def store_o(lo, val, hh=hh):
                o_ref[rows(lo), hh * vw:(hh + 1) * vw] = val

            def store_state(sf, sb, hh=hh):
                if emit_state:
                    sfo_ref[hh] = sf
                    sbo_ref[hh] = sb
        else:
            h = pl.program_id(1)
            whole = lambda ref: (lambda lo: ref[rows(lo), :])
            io = dict(q=whole(data[0]), k=whole(data[1]), v=whole(data[2]), gate=whole(data[3]),
                      gn=gn_ref[...], s0f=s0f_ref[...], s0b=s0b_ref[...])

            def store_o(lo, val):
                o_ref[rows(lo), :] = val

            def store_state(sf, sb):
                if emit_state:
                    sfo_ref[...] = sf
                    sbo_ref[...] = sb
        io.update(store_o=store_o, store_state=store_state)
        _ret_head(lg_ref[0, h], lg_ref[1, h], io, tabs, nc, kvf_ref.at[hh], kvb_ref.at[hh], state_ref.at[hh])


def _retention(proj, lg, gn, s0f, s0b, s0_layer, layer, tabs, batch, n, *, emit_state, all_heads, depth=None,
               state_bufs=None):
    rope = tabs is not None
    assert all_heads or not emit_state
    if all_heads:
        wide = (OFF_CQ - OFF_BQ) // 2
        assert OFF_BQ % wide == 0 and wide == 2 * B_HEADS * B_QK_DIM + 2 * B_V_DIM
        st_spec = pl.BlockSpec((None, None, B_HEADS, B_QK_DIM, B_V_DIM), lambda b, h, lg_: (b, s0_layer, 0, 0, 0))
        in_specs = [
            pl.BlockSpec((n, wide), lambda b, h, lg_: (b, OFF_BQ // wide)),
            pl.BlockSpec((n, wide), lambda b, h, lg_: (b, OFF_BQ // wide + 1)),
            pl.BlockSpec((None, 1, B_HEADS * B_V_DIM), lambda b, h, lg_: (layer, 0, 0)),
        ]
        args = [proj, proj, gn]
        out_specs = [pl.BlockSpec((n, B_HEADS * B_V_DIM), lambda b, h, lg_: (b, 0))]
        heads_per_step = B_HEADS
    else:
        qb, kb = OFF_BQ // B_QK_DIM, OFF_BK // B_QK_DIM
        vb, gb = OFF_BV // B_V_DIM, OFF_BG // B_V_DIM
        st_spec = pl.BlockSpec((None, None, None, B_QK_DIM, B_V_DIM), lambda b, h, lg_: (b, s0_layer, h, 0, 0))
        in_specs = [
            pl.BlockSpec((n, B_QK_DIM), lambda b, h, lg_: (b, qb + h)),
            pl.BlockSpec((n, B_QK_DIM), lambda b, h, lg_: (b, kb + h)),
            pl.BlockSpec((n, B_V_DIM), lambda b, h, lg_: (b, vb + h)),
            pl.BlockSpec((n, B_V_DIM), lambda b, h, lg_: (b, gb + h)),
            pl.BlockSpec((None, 1, B_V_DIM), lambda b, h, lg_: (layer, 0, h)),
        ]
        args = [proj, proj, proj, proj, gn]
        out_specs = [pl.BlockSpec((n, B_V_DIM), lambda b, h, lg_: (b, h))]
        heads_per_step = 1
    in_specs += [st_spec, st_spec]
    args += [s0f, s0b]
    if rope:
        in_specs += [pl.BlockSpec((n, LANE), lambda b, h, lg_: (0, 0))] * 3
        args += list(tabs)
    out_shape = [jax.ShapeDtypeStruct((batch * n, B_HEADS * B_V_DIM), BF16)]
    aliases = {}
    if emit_state:
        out_specs += [pl.BlockSpec((None, None, B_HEADS, B_QK_DIM, B_V_DIM),
                                   lambda b, h, lg_: (b, layer, 0, 0, 0))] * 2
        out_shape += [jax.ShapeDtypeStruct((batch, depth, B_HEADS, B_QK_DIM, B_V_DIM), F32)] * 2
        for j, buf in enumerate(state_bufs or ()):
            aliases[1 + len(args)] = 1 + j
            in_specs.append(pl.BlockSpec(memory_space=pl.ANY))
            args.append(buf)
    nc = n // B_CHUNK
    return pl.pallas_call(
        functools.partial(_ret_kernel, n=n, rope=rope, emit_state=emit_state, all_heads=all_heads,
                          n_alias=len(aliases)),
        grid_spec=pltpu.PrefetchScalarGridSpec(
            num_scalar_prefetch=1,
            grid=(batch, B_HEADS // heads_per_step),
            in_specs=in_specs,
            out_specs=out_specs,
            scratch_shapes=[pltpu.VMEM((heads_per_step, nc, B_QK_DIM, B_V_DIM), F32),
                            pltpu.VMEM((heads_per_step, nc, B_QK_DIM, B_V_DIM), F32),
                            pltpu.VMEM((heads_per_step, nc, 2 * B_QK_DIM, B_V_DIM), BF16)],
        ),
        out_shape=out_shape,
        input_output_aliases=aliases,
        compiler_params=_params("arbitrary", "arbitrary"),
        name="retention",
    )(lg, *args)


def _merge_kernel(oa_ref, ob_ref, oc_ref, wa_ref, wb_ref, wc_ref, ga_ref, gb_ref, gc_ref, o_ref):
    def branch(o, w, g):
        return jax.nn.sigmoid(g[...].astype(F32)) * jnp.dot(o[...], w[...], preferred_element_type=F32)

    o_ref[...] = (branch(oa_ref, wa_ref, ga_ref) + branch(ob_ref, wb_ref, gb_ref)
                  + branch(oc_ref, wc_ref, gc_ref)).astype(BF16)


def _merge(oa, ob, oc, wa, wb, wc, gates, layer, d):
    m = oa.shape[0]
    tm = min(TILE_MERGE_M, m)
    tn = min(TILE_MERGE_N, d)
    nb = d // tn
    o_spec = lambda o: pl.BlockSpec((tm, o.shape[1]), lambda i, j: (i, 0))
    w_spec = lambda w: pl.BlockSpec((None, w.shape[1], tn), lambda i, j: (layer, 0, j))
    g_spec = lambda br: pl.BlockSpec((tm, tn), lambda i, j: (i, br * nb + j))
    return pl.pallas_call(
        _merge_kernel,
        grid=(m // tm, nb),
        in_specs=[o_spec(oa), o_spec(ob), o_spec(oc), w_spec(wa), w_spec(wb), w_spec(wc),
                  g_spec(0), g_spec(1), g_spec(2)],
        out_specs=pl.BlockSpec((tm, tn), lambda i, j: (i, j)),
        out_shape=jax.ShapeDtypeStruct((m, d), BF16),
        compiler_params=_params("arbitrary", "arbitrary"),
        name="merge",
    )(oa, ob, oc, wa, wb, wc, gates, gates, gates)


def _rope_tables(n_tokens, dim, pad):
    pos = jnp.arange(n_tokens, dtype=jnp.int32)
    row = (pos // GRID_W).astype(F32)[:, None]
    col = (pos % GRID_W).astype(F32)[:, None]
    quarter = dim // 4
    inv = ROPE_THETA ** (-jnp.arange(quarter, dtype=F32) / quarter)[None, :]
    ang = jnp.concatenate([row * inv, row * inv, col * inv, col * inv], axis=1)
    cos, sin = jnp.cos(ang), jnp.sin(ang)
    first = (jnp.arange(dim) // quarter) % 2 == 0
    sin_lo = jnp.where(first[None, :], -sin, 0.0)
    sin_hi = jnp.where(first[None, :], 0.0, sin)
    if pad:
        z = jnp.zeros_like(cos)
        tabs = [jnp.concatenate([t, z], axis=1) for t in (cos, sin_lo, sin_hi)]
    else:
        tabs = [jnp.tile(t, (1, LANE // dim)) for t in (cos, sin_lo, sin_hi)]
    return tuple(tabs)


def kernel(x_prompt, x_sample, c, cache_attn_k, cache_attn_v, state_ret_fwd, state_ret_bwd, cache_mla_ckv, cache_mla_krope, c_ctx, w_mod, b_mod, g_pre_mix, g_post_mix, g_pre_mlp, g_post_mlp, w_in, attn_q_norm, attn_k_norm, ret_decay_fwd, ret_decay_bwd, ret_gn, mla_q_norm, mla_kv_norm, w_mla_uq, w_mla_ukv, w_branch_a, w_branch_b, w_branch_c, w_out, w_mlp_up, w_mlp_down):
    bp, n_p, d = x_prompt.shape
    bs, n_s, _ = x_sample.shape
    depth = w_in.shape[0]
    past = cache_attn_k.shape[2]
    assert n_p % B_CHUNK == 0 and n_s % B_CHUNK == 0

    w_in_t = jnp.transpose(w_in, (0, 2, 1))
    gate_w = w_in.shape[-1] - IN_MAIN
    w_uq = jnp.pad(w_mla_uq.reshape(depth, C_Q_RANK, C_HEADS, C_NOPE_DIM + C_ROPE_DIM),
                   ((0, 0), (0, 0), (0, 0), (0, C_QK_PAD - C_NOPE_DIM - C_ROPE_DIM)))
    w_uq = w_uq.reshape(depth, C_Q_RANK, C_HEADS * C_QK_PAD).astype(BF16)
    w_ukv = w_mla_ukv.astype(BF16)
    wa, wb, wc = w_branch_a.astype(BF16), w_branch_b.astype(BF16), w_branch_c.astype(BF16)
    w_o = w_out.astype(BF16)
    vec = lambda g: g.reshape(depth, 1, g.shape[-1])
    g_pre_mix, g_post_mix, g_pre_mlp, g_post_mlp = map(vec, (g_pre_mix, g_post_mix, g_pre_mlp, g_post_mlp))
    gqn, gkn, gcq, gckv, gret = map(vec, (attn_q_norm, attn_k_norm, mla_q_norm, mla_kv_norm, ret_gn))
    lg = jnp.stack([jax.nn.log_sigmoid(ret_decay_fwd.astype(F32)),
                    jax.nn.log_sigmoid(ret_decay_bwd.astype(F32))], axis=1)

    rows = -(-(1 + bs) // 8) * 8
    cvec = jnp.zeros((rows, d), F32).at[0].set(c_ctx).at[1:1 + bs].set(c)
    mod = _modulation(cvec, w_mod, b_mod).reshape(depth, rows, 6, d)
    mod = jnp.pad(mod, ((0, 0), (0, 0), (0, MOD_ROWS - 6), (0, 0)))

    tabs128 = _rope_tables(n_s, HEAD_DIM, pad=False)
    tabs64 = _rope_tables(n_s, C_ROPE_DIM, pad=False)
    tabs64p = _rope_tables(n_s, C_ROPE_DIM, pad=True)

    to_layers = lambda t: jnp.swapaxes(t.reshape(bs, depth, past, -1), 0, 1).reshape(depth, bs * past, -1)
    ctx_k, ctx_v = to_layers(cache_attn_k.astype(BF16)), to_layers(cache_attn_v.astype(BF16))
    ctx_kr = jnp.pad(cache_mla_krope, ((0, 0), (0, 0), (0, 0), (0, LANE - C_ROPE_DIM)))
    ctx_c = _prep_ctx_ckv(to_layers(cache_mla_ckv), to_layers(ctx_kr), gckv, w_ukv)
    zero_state = jnp.zeros((bp, 1, B_HEADS, B_QK_DIM, B_V_DIM), F32)

    streams = {
        "p": dict(x=x_prompt.reshape(bp * n_p, d), batch=bp, n=n_p, grp=lambda r: 0, latent=False),
        "s": dict(x=x_sample.reshape(bs * n_s, d), batch=bs, n=n_s, grp=lambda r: 1 + r // n_s, latent=True),
    }
    for st in streams.values():
        st["h"] = _prenorm(st["x"], g_pre_mix, mod, 0, st["grp"])
    kvw = A_KV_HEADS * HEAD_DIM
    cache_bufs = tuple(jnp.zeros((bp, depth, n_p, w), F32) for w in (kvw, kvw, C_KV_RANK, C_ROPE_DIM))
    state_bufs = tuple(jnp.zeros((bp, depth, B_HEADS, B_QK_DIM, B_V_DIM), F32) for _ in range(2))

    for l in range(depth):
        for st in streams.values():
            batch, n, grp, latent = st["batch"], st["n"], st["grp"], st["latent"]
            proj = _matmul(st["h"], w_in_t, l, tm=TILE_MATMUL_M, tn=TILE_W_IN_N, w_transposed=True, n=MAIN_W,
                           name="w_in")
            gates = _matmul(st["h"], w_in_t, l, tm=TILE_MATMUL_M, tn=TILE_WIDE_N, w_transposed=True,
                            w_row0=IN_MAIN, n=gate_w, name="w_gates")
            t128 = tabs128 if latent else None

            tabs = (tabs128, tabs64, tabs64p) if latent else None
            pa, qc, pc = _prep_all(proj, gqn, gkn, gcq, gckv, w_uq, w_ukv, l, tabs, n, cache=not latent,
                                   depth=depth, cache_bufs=None if latent else cache_bufs)
            if not latent:
                cache_bufs = (pa[2], pa[3], pc[2], pc[3])

            qa, ka = pa[0], pa[1]
            ctx_a = (ctx_k, ctx_v) if latent else None
            o_a = _attention(qa, ka, 0, proj, OFF_AV // HEAD_DIM, ctx_a, batch, n,
                             groups=A_KV_HEADS, gps=1 if latent else A_KV_HEADS, rep=A_HEADS // A_KV_HEADS,
                             dqk=HEAD_DIM, dv=HEAD_DIM, rows=ATTN_ROWS, name="attn_a", ctx_layer=l)

            if latent:
                s0f, s0b, s0_layer = state_ret_fwd, state_ret_bwd, l
            else:
                s0f, s0b, s0_layer = zero_state, zero_state, 0
            rb = _retention(proj, lg[l], gret, s0f, s0b, s0_layer, l, t128, batch, n,
                            emit_state=not latent, all_heads=not latent, depth=depth,
                            state_bufs=None if latent else state_bufs)
            o_b = rb[0]
            if not latent:
                state_bufs = (rb[1], rb[2])

            kc, vc = pc[0], pc[1]
            o_c = _attention(qc, kc, 0, vc, 0, ctx_c if latent else None, batch, n,
                             groups=C_HEADS, gps=1 if latent else C_HEADS, rep=1,
                             dqk=C_QK_PAD, dv=C_V_DIM, rows=ATTN_ROWS, name="attn_c", ctx_layer=l)

            merged = _merge(o_a, o_b, o_c, wa, wb, wc, gates, l, d)
            group_rows = n if latent else batch * n
            x, h2 = _matmul_row(merged, w_o, st["x"], g_post_mix, g_pre_mlp, mod, l, l, grp,
                                gate_row=2, next_rows=(3, 4), tm=min(TILE_W_OUT_M, group_rows), tk=TILE_W_OUT_K,
                                name="w_out")
            hid = _matmul(h2, w_mlp_up, l, tm=TILE_MATMUL_M, tn=TILE_WIDE_N, relu2=True, name="mlp_up")
            last = l == depth - 1
            res = _matmul_row(hid, w_mlp_down, x, g_post_mlp, g_pre_mix, mod, l, min(l + 1, depth - 1), grp,
                              gate_row=5, next_rows=None if last else (0, 1),
                              tm=min(TILE_MLP_DOWN_M, group_rows), tk=TILE_MLP_DOWN_K, name="mlp_down")
            st["x"] = res[0]
            st["h"] = None if last else res[1]

    y_p = streams["p"]["x"].reshape(bp, n_p, d)
    y_s = streams["s"]["x"].reshape(bs, n_s, d)
    nk, nv, new_mla_ckv, new_mla_krope = cache_bufs
    new_attn_k = nk.reshape(bp, depth, n_p, A_KV_HEADS, HEAD_DIM)
    new_attn_v = nv.reshape(bp, depth, n_p, A_KV_HEADS, HEAD_DIM)
    new_ret_fwd, new_ret_bwd = state_bufs
    return (y_p, y_s, new_attn_k, new_attn_v, new_ret_fwd, new_ret_bwd, new_mla_ckv, new_mla_krope)
```
